```python
import jax, jax.numpy as jnp
from jax import lax
import numpy as np

D_MODEL = 2048
BATCH = 4
SEQ = 2048
DEPTH = 4
DEC_BATCH = 32
DEC_SEQ = 1
PAST_LEN = 16384
PAGE_SIZE = 128

HEAD_DIM = 64
A_HEADS = 16
A_KV_HEADS = 4
A_GROUP = A_HEADS // A_KV_HEADS
WINDOW = 128
BLOCK = WINDOW
ROPE_THETA = 10000.0
A_Q = A_HEADS * HEAD_DIM
A_KV = A_KV_HEADS * HEAD_DIM
A_COLS = A_Q + 2 * A_KV
B_HEADS = 16
B_WIDTH = B_HEADS * HEAD_DIM
D_DECAY = 64
D_AAA = 64
D_GATE = 160
B_COLS = 3 * B_WIDTH + D_DECAY + D_AAA + D_GATE
GN_EPS = 64e-5
AB_COLS = A_COLS + B_COLS
MIX_WIDTH = A_Q + B_WIDTH
CONV_W = 3
D_CONV = D_MODEL
D_FF = 5632
NORM_EPS = 1e-6
N_AB = (DEPTH + 1) // 2
N_CV = DEPTH // 2

kernel_name = 'hybrid_swa_rwkv7_shortconv_macaron_step'

F32 = jnp.float32


def rmsnorm(x, g):
    x32 = x.astype(F32)
    y = x32 * lax.rsqrt(jnp.mean(x32 * x32, axis=-1, keepdims=True) + NORM_EPS)
    return (y * g.astype(F32)).astype(x.dtype)


def swiglu(x, w_gu, w_down):
    gate, up = jnp.split(x @ w_gu, 2, axis=-1)
    return (jax.nn.silu(gate) * up) @ w_down


def rope(x, pos):
    half = HEAD_DIM // 2
    freqs = ROPE_THETA ** (-jnp.arange(half, dtype=F32) / half)
    ang = pos.astype(F32)[:, None] * freqs[None, :]
    cos = jnp.cos(ang)[None, :, None, :]
    sin = jnp.sin(ang)[None, :, None, :]
    x32 = x.astype(F32)
    x1, x2 = x32[..., :half], x32[..., half:]
    return jnp.concatenate([x1 * cos - x2 * sin, x2 * cos + x1 * sin], axis=-1).astype(x.dtype)


def sink_softmax(s, sink):
    m = jnp.maximum(jnp.max(s, axis=-1, keepdims=True), sink)
    e = jnp.exp(s - m)
    return e / (jnp.sum(e, axis=-1, keepdims=True) + jnp.exp(sink - m))


def swa_banded(q, k, v, sinks):
    b, S = q.shape[:2]
    nb = S // BLOCK
    qb = q.reshape(b, nb, BLOCK, A_KV_HEADS, A_GROUP, HEAD_DIM)

    def band(t):
        tb = t.reshape(b, nb, BLOCK, A_KV_HEADS, HEAD_DIM)
        prev = jnp.concatenate([jnp.zeros_like(tb[:, :1]), tb[:, :-1]], axis=1)
        return jnp.concatenate([prev, tb], axis=2)

    kw, vw = band(k), band(v)
    s = jnp.einsum('bnqkgd,bnckd->bnkgqc', qb, kw).astype(F32) * (HEAD_DIM ** -0.5)
    qi = jnp.arange(BLOCK)[:, None]
    kc = jnp.arange(2 * BLOCK)[None, :]
    diff = BLOCK + qi - kc
    in_band = (diff >= 0) & (diff <= WINDOW)
    has_prev = jnp.arange(nb)[:, None, None] > 0
    mask = in_band[None] & (has_prev | (kc >= BLOCK)[None])
    s = jnp.where(mask[None, :, None, None], s, -jnp.inf)
    p = sink_softmax(s, sinks.astype(F32).reshape(A_KV_HEADS, A_GROUP)[None, None, :, :, None, None])
    o = jnp.einsum('bnkgqc,bnckd->bnqkgd', p.astype(vw.dtype), vw)
    return o.reshape(b, S, A_Q)


def swa_cached(q, k, v, ck, cv, sinks):
    b, T = q.shape[:2]
    kw = jnp.concatenate([ck.astype(k.dtype), k], axis=1)
    vw = jnp.concatenate([cv.astype(v.dtype), v], axis=1)
    qg = q.reshape(b, T, A_KV_HEADS, A_GROUP, HEAD_DIM)
    s = jnp.einsum('btkgd,bckd->bkgtc', qg, kw).astype(F32) * (HEAD_DIM ** -0.5)
    diff = (jnp.arange(T)[:, None] + WINDOW) - jnp.arange(WINDOW + T)[None, :]
    mask = (diff >= 0) & (diff <= WINDOW)
    s = jnp.where(mask, s, -jnp.inf)
    p = sink_softmax(s, sinks.astype(F32).reshape(A_KV_HEADS, A_GROUP)[None, :, :, None, None])
    o = jnp.einsum('bkgtc,bckd->btkgd', p.astype(vw.dtype), vw).reshape(b, T, A_Q)
    return o, kw[:, -WINDOW:], vw[:, -WINDOW:]


def rwkv_mix(pb, shift0, s0, mu, w0, w_decay, a0, w_aaa, w_gate, k_k, k_a, r_k, gn_w, gn_b):
    b, T, _ = pb.shape
    prev = jnp.concatenate([shift0[:, None].astype(pb.dtype), pb[:, :-1]], axis=1)
    xm = pb + (prev - pb) * mu
    r, k, v, dw, da, dg = jnp.split(
        xm, [B_WIDTH, 2 * B_WIDTH, 3 * B_WIDTH, 3 * B_WIDTH + D_DECAY, 3 * B_WIDTH + D_DECAY + D_AAA], axis=-1)
    w_log = -jax.nn.softplus(-(w0 + jnp.tanh(dw) @ w_decay).astype(F32)) - 0.5
    decay = jnp.exp(-jnp.exp(w_log))
    a = jax.nn.sigmoid((a0 + da @ w_aaa).astype(F32))
    g = jax.nn.sigmoid(dg) @ w_gate

    def heads(t):
        return t.astype(F32).reshape(b, T, B_HEADS, HEAD_DIM)

    def per_head(wv):
        return wv.astype(F32).reshape(B_HEADS, HEAD_DIM)

    r, k, v, decay, a = heads(r), heads(k), heads(v), heads(decay), heads(a)
    kk = k * per_head(k_k)
    kk = kk / jnp.maximum(jnp.sqrt(jnp.sum(kk * kk, axis=-1, keepdims=True)), 1e-12)
    k = k * (1.0 + (a - 1.0) * per_head(k_a))

    def step(S, inp):
        r_t, k_t, v_t, w_t, kk_t, a_t = inp
        S = (S * w_t[:, :, None, :]
             + jnp.einsum('bhvk,bhk->bhv', S, -kk_t)[..., None] * (kk_t * a_t)[:, :, None, :]
             + v_t[..., None] * k_t[:, :, None, :])
        return S, jnp.einsum('bhvk,bhk->bhv', S, r_t)

    seq = tuple(jnp.moveaxis(t, 1, 0) for t in (r, k, v, decay, kk, a))
    S, y = lax.scan(step, s0.astype(F32), seq)
    y = jnp.moveaxis(y, 0, 1)
    mean = jnp.mean(y, axis=-1, keepdims=True)
    var = jnp.mean((y - mean) ** 2, axis=-1, keepdims=True)
    y = (y - mean) * lax.rsqrt(var + GN_EPS) * per_head(gn_w) + per_head(gn_b)
    y = y + jnp.sum(r * k * per_head(r_k), axis=-1, keepdims=True) * v
    out = y.reshape(b, T, B_WIDTH).astype(pb.dtype) * g
    return out, S, pb[:, -1]


def ab_mixer(h, pos, swa_cache, s0, shift0, p, i):
    b, T, _ = h.shape
    proj = h @ p['ab_w_in'][i]
    q, k, v, pb = jnp.split(proj, [A_Q, A_Q + A_KV, A_COLS], axis=-1)
    q = rope(q.reshape(b, T, A_HEADS, HEAD_DIM), pos)
    k = rope(k.reshape(b, T, A_KV_HEADS, HEAD_DIM), pos)
    v = v.reshape(b, T, A_KV_HEADS, HEAD_DIM)
    sinks = p['attn_sinks'][i]
    if swa_cache is None:
        a_out = swa_banded(q, k, v, sinks)
        nk, nv = k[:, -WINDOW:], v[:, -WINDOW:]
    else:
        a_out, nk, nv = swa_cached(q, k, v, swa_cache[0], swa_cache[1], sinks)
    b_out, ns, nsh = rwkv_mix(pb, shift0, s0, p['rwkv_mu'][i], p['rwkv_w0'][i], p['rwkv_w_decay'][i],
                              p['rwkv_a0'][i], p['rwkv_w_aaa'][i], p['rwkv_w_gate'][i], p['rwkv_k_k'][i],
                              p['rwkv_k_a'][i], p['rwkv_r_k'][i], p['rwkv_gn_w'][i], p['rwkv_gn_b'][i])
    out = jnp.concatenate([a_out, b_out], axis=-1) @ p['ab_w_out'][i]
    return out, nk, nv, ns.astype(h.dtype), nsh


def conv_mixer(h, buf, w_in, w_conv, w_out):
    bg, cg, hv = jnp.split(h @ w_in, 3, axis=-1)
    u = cg * hv
    T = u.shape[1]
    full = jnp.concatenate([buf.astype(u.dtype), u], axis=1)
    y = sum(full[:, j:j + T] * w_conv[j] for j in range(CONV_W))
    return (bg * y) @ w_out, full[:, -(CONV_W - 1):]


def _trunk(x, past, p):
    b, T, _ = x.shape
    start = 0 if past is None else PAST_LEN
    pos = start + jnp.arange(T, dtype=jnp.int32)
    out_k, out_v, out_s, out_sh, out_c = [], [], [], [], []
    for l in range(DEPTH):
        g = p['norm_g'][l]
        x = x + 0.5 * rmsnorm(swiglu(rmsnorm(x, g[0]), p['ffn_w_gu'][l, 0], p['ffn_w_down'][l, 0]), g[1])
        h = rmsnorm(x, g[2])
        i = l // 2
        if l % 2 == 0:
            if past is None:
                swa = None
                s0 = jnp.zeros((b, B_HEADS, HEAD_DIM, HEAD_DIM), F32)
                sh0 = jnp.zeros((b, B_COLS), x.dtype)
            else:
                swa = (past[0][i], past[1][i])
                s0, sh0 = past[2][i], past[3][i]
            m, nk, nv, ns, nsh = ab_mixer(h, pos, swa, s0, sh0, p, i)
            out_k.append(nk)
            out_v.append(nv)
            out_s.append(ns)
            out_sh.append(nsh)
        else:
            buf = jnp.zeros((b, CONV_W - 1, D_CONV), x.dtype) if past is None else past[4][i]
            m, nc = conv_mixer(h, buf, p['conv_w_in'][i], p['conv_w'][i], p['conv_w_out'][i])
            out_c.append(nc)
        x = x + rmsnorm(m, g[3])
        x = x + 0.5 * rmsnorm(swiglu(rmsnorm(x, g[4]), p['ffn_w_gu'][l, 1], p['ffn_w_down'][l, 1]), g[5])
    return x, (jnp.stack(out_k), jnp.stack(out_v), jnp.stack(out_s), jnp.stack(out_sh), jnp.stack(out_c))


def setup_inputs(seed: int = 0) -> dict:
    key = jax.random.key(seed)
    ks = iter(jax.random.split(key, 32))

    def nrm(shape, scale):
        return jax.random.normal(next(ks), shape, F32) * scale

    return {
        'x_prompt': nrm((BATCH, SEQ, D_MODEL), 1.0),
        'x_sample': nrm((DEC_BATCH, DEC_SEQ, D_MODEL), 1.0),
        'cache_swa_k': nrm((N_AB, DEC_BATCH, WINDOW, A_KV_HEADS, HEAD_DIM), 1.0),
        'cache_swa_v': nrm((N_AB, DEC_BATCH, WINDOW, A_KV_HEADS, HEAD_DIM), 1.0),
        'state_rwkv': nrm((N_AB, DEC_BATCH, B_HEADS, HEAD_DIM, HEAD_DIM), 0.5),
        'state_rwkv_shift': nrm((N_AB, DEC_BATCH, B_COLS), 1.0),
        'state_conv': nrm((N_CV, DEC_BATCH, CONV_W - 1, D_CONV), 1.0),
        'norm_g': 1.0 + nrm((DEPTH, 6, D_MODEL), 0.05),
        'ffn_w_gu': nrm((DEPTH, 2, D_MODEL, 2 * D_FF), D_MODEL ** -0.5),
        'ffn_w_down': nrm((DEPTH, 2, D_FF, D_MODEL), D_FF ** -0.5),
        'ab_w_in': nrm((N_AB, D_MODEL, AB_COLS), D_MODEL ** -0.5),
        'ab_w_out': nrm((N_AB, MIX_WIDTH, D_MODEL), MIX_WIDTH ** -0.5),
        'attn_sinks': nrm((N_AB, A_HEADS), 1.0),
        'rwkv_mu': jax.random.uniform(next(ks), (N_AB, B_COLS), F32),
        'rwkv_w0': nrm((N_AB, B_WIDTH), 1.0) - 1.0,
        'rwkv_w_decay': nrm((N_AB, D_DECAY, B_WIDTH), 0.1),
        'rwkv_a0': nrm((N_AB, B_WIDTH), 0.5),
        'rwkv_w_aaa': nrm((N_AB, D_AAA, B_WIDTH), 0.5 * D_AAA ** -0.5),
        'rwkv_w_gate': nrm((N_AB, D_GATE, B_WIDTH), D_GATE ** -0.5),
        'rwkv_k_k': 0.85 + nrm((N_AB, B_WIDTH), 0.05),
        'rwkv_k_a': 1.0 + nrm((N_AB, B_WIDTH), 0.05),
        'rwkv_r_k': nrm((N_AB, B_WIDTH), 0.1),
        'rwkv_gn_w': 1.0 + nrm((N_AB, B_WIDTH), 0.05),
        'rwkv_gn_b': nrm((N_AB, B_WIDTH), 0.01),
        'conv_w_in': nrm((N_CV, D_MODEL, 3 * D_CONV), D_MODEL ** -0.5),
        'conv_w': nrm((N_CV, CONV_W, D_CONV), CONV_W ** -0.5),
        'conv_w_out': nrm((N_CV, D_CONV, D_MODEL), D_CONV ** -0.5),
    }


def reference(x_prompt, x_sample, cache_swa_k, cache_swa_v, state_rwkv, state_rwkv_shift, state_conv,
              norm_g, ffn_w_gu, ffn_w_down, ab_w_in, ab_w_out, attn_sinks,
              rwkv_mu, rwkv_w0, rwkv_w_decay, rwkv_a0, rwkv_w_aaa, rwkv_w_gate,
              rwkv_k_k, rwkv_k_a, rwkv_r_k, rwkv_gn_w, rwkv_gn_b,
              conv_w_in, conv_w, conv_w_out):
    p = {
        'norm_g': norm_g, 'ffn_w_gu': ffn_w_gu, 'ffn_w_down': ffn_w_down,
        'ab_w_in': ab_w_in, 'ab_w_out': ab_w_out, 'attn_sinks': attn_sinks,
        'rwkv_mu': rwkv_mu, 'rwkv_w0': rwkv_w0, 'rwkv_w_decay': rwkv_w_decay, 'rwkv_a0': rwkv_a0,
        'rwkv_w_aaa': rwkv_w_aaa, 'rwkv_w_gate': rwkv_w_gate, 'rwkv_k_k': rwkv_k_k, 'rwkv_k_a': rwkv_k_a,
        'rwkv_r_k': rwkv_r_k, 'rwkv_gn_w': rwkv_gn_w, 'rwkv_gn_b': rwkv_gn_b,
        'conv_w_in': conv_w_in, 'conv_w': conv_w, 'conv_w_out': conv_w_out,
    }
    y_prompt, (p_swa_k, p_swa_v, p_rwkv, p_shift, p_conv) = _trunk(x_prompt, None, p)
    y_sample, (s_swa_k, s_swa_v, s_rwkv, s_shift, s_conv) = _trunk(
        x_sample, (cache_swa_k, cache_swa_v, state_rwkv, state_rwkv_shift, state_conv), p)
    return (y_prompt, y_sample, p_swa_k, p_swa_v, p_rwkv, p_shift, p_conv,
            s_swa_k, s_swa_v, s_rwkv, s_shift, s_conv)
```

```python
import functools

import jax
import jax.numpy as jnp
from jax import lax
from jax.experimental import pallas as pl
from jax.experimental.pallas import tpu as pltpu

F32 = jnp.float32
BF16 = jnp.bfloat16

HEAD_DIM = 64
A_HEADS = 16
A_KV_HEADS = 4
A_GROUP = A_HEADS // A_KV_HEADS
WINDOW = 128
ROPE_THETA = 10000.0
PAST_LEN = 16384
A_Q = A_HEADS * HEAD_DIM
A_KV = A_KV_HEADS * HEAD_DIM
A_COLS = A_Q + 2 * A_KV
B_HEADS = 16
B_WIDTH = B_HEADS * HEAD_DIM
D_DECAY = 64
D_AAA = 64
D_GATE = 160
B_COLS = 3 * B_WIDTH + D_DECAY + D_AAA + D_GATE
GN_EPS = 64e-5
AB_COLS = A_COLS + B_COLS
CONV_W = 3
NORM_EPS = 1e-6

LANE = 128
AB_COLS_PAD = 5120
B_COLS_PAD = AB_COLS_PAD - A_COLS
LORA_IN = D_DECAY + D_AAA
GATE_PAD = AB_COLS_PAD - A_COLS - 3 * B_WIDTH - LORA_IN
RWKV_CHUNK = 64
RWKV_INV_BLOCK = 16
VMEM_LIMIT = 56 * 1024 * 1024


def _params(sem):
    return pltpu.CompilerParams(dimension_semantics=sem, vmem_limit_bytes=VMEM_LIMIT)


def _rms(x, g):
    return x * lax.rsqrt(jnp.mean(x * x, axis=-1, keepdims=True) + NORM_EPS) * g


def _dot(a, b):
    return jnp.dot(a, b, preferred_element_type=F32)


def _ffn_kernel(x_ref, g0_ref, g1_ref, wg_ref, wu_ref, wd_ref, o_ref, h_ref, acc_ref):
    f = pl.program_id(1)

    @pl.when(f == 0)
    def _():
        h_ref[...] = _rms(x_ref[...], g0_ref[...]).astype(BF16)
        acc_ref[...] = jnp.zeros_like(acc_ref)

    h = h_ref[...]
    gate = _dot(h, wg_ref[...])
    up = _dot(h, wu_ref[...])
    act = (gate * jax.nn.sigmoid(gate) * up).astype(BF16)
    acc_ref[...] += _dot(act, wd_ref[...])

    @pl.when(f == pl.num_programs(1) - 1)
    def _():
        o_ref[...] = x_ref[...] + 0.5 * _rms(acc_ref[...], g1_ref[...])


def _ffn(x, g0, g1, w_gu, w_down, tm, tf):
    m, d = x.shape
    d_ff = w_down.shape[0]
    nf = d_ff // tf
    return pl.pallas_call(
        _ffn_kernel,
        grid=(m // tm, nf),
        in_specs=[
            pl.BlockSpec((tm, d), lambda i, f: (i, 0)),
            pl.BlockSpec((1, d), lambda i, f: (0, 0)),
            pl.BlockSpec((1, d), lambda i, f: (0, 0)),
            pl.BlockSpec((d, tf), lambda i, f: (0, f)),
            pl.BlockSpec((d, tf), lambda i, f: (0, nf + f)),
            pl.BlockSpec((tf, d), lambda i, f: (f, 0)),
        ],
        out_specs=pl.BlockSpec((tm, d), lambda i, f: (i, 0)),
        out_shape=jax.ShapeDtypeStruct((m, d), F32),
        scratch_shapes=[pltpu.VMEM((tm, d), BF16), pltpu.VMEM((tm, d), F32)],
        compiler_params=_params(("parallel", "arbitrary")),
        name="ffn",
    )(x, g0, g1, w_gu, w_gu, w_down)


def _norm_mm_kernel(x_ref, g_ref, w_ref, o_ref, h_ref):
    @pl.when(pl.program_id(1) == 0)
    def _():
        h_ref[...] = _rms(x_ref[...], g_ref[...]).astype(BF16)

    o_ref[...] = _dot(h_ref[...], w_ref[...])


def _norm_mm(x, g, w, tm, tn):
    m, d = x.shape
    n = w.shape[1]
    return pl.pallas_call(
        _norm_mm_kernel,
        grid=(m // tm, n // tn),
        in_specs=[
            pl.BlockSpec((tm, d), lambda i, j: (i, 0)),
            pl.BlockSpec((1, d), lambda i, j: (0, 0)),
            pl.BlockSpec((d, tn), lambda i, j: (0, j)),
        ],
        out_specs=pl.BlockSpec((tm, tn), lambda i, j: (i, j)),
        out_shape=jax.ShapeDtypeStruct((m, n), F32),
        scratch_shapes=[pltpu.VMEM((tm, d), BF16)],
        compiler_params=_params(("parallel", "arbitrary")),
        name="ab_in",
    )(x, g, w)


def _conv_in_kernel(x_ref, g_ref, wb_ref, wc_ref, wh_ref, bg_ref, u_ref, h_ref):
    @pl.when(pl.program_id(1) == 0)
    def _():
        h_ref[...] = _rms(x_ref[...], g_ref[...]).astype(BF16)

    h = h_ref[...]
    bg_ref[...] = _dot(h, wb_ref[...])
    u_ref[...] = _dot(h, wc_ref[...]) * _dot(h, wh_ref[...])


def _conv_in(x, g, w_in, tm, tn):
    m, d = x.shape
    dc = w_in.shape[1] // 3
    nj = dc // tn
    return pl.pallas_call(
        _conv_in_kernel,
        grid=(m // tm, nj),
        in_specs=[
            pl.BlockSpec((tm, d), lambda i, j: (i, 0)),
            pl.BlockSpec((1, d), lambda i, j: (0, 0)),
            pl.BlockSpec((d, tn), lambda i, j: (0, j)),
            pl.BlockSpec((d, tn), lambda i, j: (0, nj + j)),
            pl.BlockSpec((d, tn), lambda i, j: (0, 2 * nj + j)),
        ],
        out_specs=[pl.BlockSpec((tm, tn), lambda i, j: (i, j)),
                   pl.BlockSpec((tm, tn), lambda i, j: (i, j))],
        out_shape=[jax.ShapeDtypeStruct((m, dc), F32), jax.ShapeDtypeStruct((m, dc), F32)],
        scratch_shapes=[pltpu.VMEM((tm, d), BF16)],
        compiler_params=_params(("parallel", "arbitrary")),
        name="conv_in",
    )(x, g, w_in, w_in, w_in)


def _conv_taps(u, p1, p2, cw):
    return p2 * cw[0:1, :] + p1 * cw[1:2, :] + u * cw[2:3, :]


def _conv_out_seq_kernel(x_ref, bg_ref, u_ref, halo_ref, cw_ref, w_ref, g_ref, o_ref, *, tiles_per_seq):
    u = u_ref[...]
    first = (pl.program_id(0) % tiles_per_seq) == 0
    h1 = jnp.where(first, 0.0, halo_ref[7:8, :])
    h2 = jnp.where(first, 0.0, halo_ref[6:7, :])
    row = lax.broadcasted_iota(jnp.int32, u.shape, 0)
    p1 = jnp.where(row == 0, h1, pltpu.roll(u, 1, 0))
    p2 = jnp.where(row == 0, h2, jnp.where(row == 1, h1, pltpu.roll(u, 2, 0)))
    z = (bg_ref[...] * _conv_taps(u, p1, p2, cw_ref[...])).astype(BF16)
    o_ref[...] = x_ref[...] + _rms(_dot(z, w_ref[...]), g_ref[...])


def _conv_out_step_kernel(x_ref, bg_ref, u_ref, p1_ref, p2_ref, cw_ref, w_ref, g_ref, o_ref):
    z = (bg_ref[...] * _conv_taps(u_ref[...], p1_ref[...], p2_ref[...], cw_ref[...])).astype(BF16)
    o_ref[...] = x_ref[...] + _rms(_dot(z, w_ref[...]), g_ref[...])


def _conv_out_seq(x, bg, u, cw, w_out, g, tm, seq):
    m, d = x.shape
    dc = u.shape[1]
    hb = tm // 8
    row_spec = pl.BlockSpec((tm, dc), lambda i: (i, 0))
    return pl.pallas_call(
        functools.partial(_conv_out_seq_kernel, tiles_per_seq=seq // tm),
        grid=(m // tm,),
        in_specs=[
            pl.BlockSpec((tm, d), lambda i: (i, 0)),
            row_spec, row_spec,
            pl.BlockSpec((8, dc), lambda i: (jnp.maximum(i * hb - 1, 0), 0)),
            pl.BlockSpec((CONV_W, dc), lambda i: (0, 0)),
            pl.BlockSpec((dc, d), lambda i: (0, 0)),
            pl.BlockSpec((1, d), lambda i: (0, 0)),
        ],
        out_specs=pl.BlockSpec((tm, d), lambda i: (i, 0)),
        out_shape=jax.ShapeDtypeStruct((m, d), F32),
        compiler_params=_params(("parallel",)),
        name="conv_out",
    )(x, bg, u, u, cw, w_out, g)


def _conv_out_step(x, bg, u, p1, p2, cw, w_out, g):
    m, d = x.shape
    dc = u.shape[1]
    row_spec = pl.BlockSpec((m, dc), lambda i: (0, 0))
    return pl.pallas_call(
        _conv_out_step_kernel,
        grid=(1,),
        in_specs=[
            pl.BlockSpec((m, d), lambda i: (0, 0)),
            row_spec, row_spec, row_spec, row_spec,
            pl.BlockSpec((CONV_W, dc), lambda i: (0, 0)),
            pl.BlockSpec((dc, d), lambda i: (0, 0)),
            pl.BlockSpec((1, d), lambda i: (0, 0)),
        ],
        out_specs=pl.BlockSpec((m, d), lambda i: (0, 0)),
        out_shape=jax.ShapeDtypeStruct((m, d), F32),
        compiler_params=_params(("arbitrary",)),
        name="conv_out_step",
    )(x, bg, u, p1, p2, cw, w_out, g)


def _rope(x, cos, sin):
    width = x.shape[-1]
    half = HEAD_DIM // 2
    lane = lax.broadcasted_iota(jnp.int32, x.shape, x.ndim - 1)
    swapped = jnp.where((lane % HEAD_DIM) < half,
                        pltpu.roll(x, width - half, x.ndim - 1),
                        pltpu.roll(x, half, x.ndim - 1))
    return x * cos + swapped * sin


def _swa_seq_kernel(sink_ref, q_ref, kc_ref, kp_ref, vc_ref, vp_ref, cq_ref, sq_ref, cp_ref, sp_ref,
                    o_ref, nk_ref, nv_ref):
    n = pl.program_id(1)
    q = _rope(q_ref[...], cq_ref[...], sq_ref[...])
    kc = _rope(kc_ref[...], cq_ref[:, :A_KV], sq_ref[:, :A_KV])
    kp = _rope(kp_ref[...], cp_ref[...], sp_ref[...])
    kw = jnp.concatenate([kp, kc], axis=0).astype(BF16)
    vw = jnp.concatenate([vp_ref[...], vc_ref[...]], axis=0).astype(BF16)
    qi = lax.broadcasted_iota(jnp.int32, (WINDOW, 2 * WINDOW), 0)
    kj = lax.broadcasted_iota(jnp.int32, (WINDOW, 2 * WINDOW), 1)
    diff = WINDOW + qi - kj
    lo = jnp.where(n > 0, 0, WINDOW)
    valid = (diff >= 0) & (diff <= WINDOW) & (kj >= lo)
    outs = []
    for h in range(A_HEADS):
        kv = h // A_GROUP
        qh = q[:, h * HEAD_DIM:(h + 1) * HEAD_DIM].astype(BF16)
        kh = kw[:, kv * HEAD_DIM:(kv + 1) * HEAD_DIM]
        vh = vw[:, kv * HEAD_DIM:(kv + 1) * HEAD_DIM]
        s = lax.dot_general(qh, kh, (((1,), (1,)), ((), ())), preferred_element_type=F32)
        s = jnp.where(valid, s * (HEAD_DIM ** -0.5), -jnp.inf)
        sink = sink_ref[h]
        mx = jnp.maximum(jnp.max(s, axis=-1, keepdims=True), sink)
        e = jnp.exp(s - mx)
        p = e / (jnp.sum(e, axis=-1, keepdims=True) + jnp.exp(sink - mx))
        outs.append(_dot(p.astype(BF16), vh))
    o_ref[...] = jnp.concatenate(outs, axis=-1)

    @pl.when(n == pl.num_programs(1) - 1)
    def _():
        nk_ref[...] = kc
        nv_ref[...] = vc_ref[...]


def _swa_seq(proj, sinks, cos, sin, batch, seq):
    nb = seq // WINDOW
    kcol = A_Q // A_KV
    vcol = kcol + 1
    cur = lambda b, n: b * nb + n
    prev = lambda b, n: b * nb + jnp.maximum(n - 1, 0)
    return pl.pallas_call(
        _swa_seq_kernel,
        grid=(batch, nb),
        in_specs=[
            pl.BlockSpec(memory_space=pltpu.SMEM),
            pl.BlockSpec((WINDOW, A_Q), lambda b, n: (cur(b, n), 0)),
            pl.BlockSpec((WINDOW, A_KV), lambda b, n: (cur(b, n), kcol)),
            pl.BlockSpec((WINDOW, A_KV), lambda b, n: (prev(b, n), kcol)),
            pl.BlockSpec((WINDOW, A_KV), lambda b, n: (cur(b, n), vcol)),
            pl.BlockSpec((WINDOW, A_KV), lambda b, n: (prev(b, n), vcol)),
            pl.BlockSpec((WINDOW, A_Q), lambda b, n: (n, 0)),
            pl.BlockSpec((WINDOW, A_Q), lambda b, n: (n, 0)),
            pl.BlockSpec((WINDOW, A_KV), lambda b, n: (jnp.maximum(n - 1, 0), 0)),
            pl.BlockSpec((WINDOW, A_KV), lambda b, n: (jnp.maximum(n - 1, 0), 0)),
        ],
        out_specs=[
            pl.BlockSpec((WINDOW, A_Q), lambda b, n: (cur(b, n), 0)),
            pl.BlockSpec((WINDOW, A_KV), lambda b, n: (b, 0)),
            pl.BlockSpec((WINDOW, A_KV), lambda b, n: (b, 0)),
        ],
        out_shape=[
            jax.ShapeDtypeStruct((batch * seq, A_Q), F32),
            jax.ShapeDtypeStruct((batch * WINDOW, A_KV), F32),
            jax.ShapeDtypeStruct((batch * WINDOW, A_KV), F32),
        ],
        compiler_params=_params(("parallel", "arbitrary")),
        name="swa_seq",
    )(sinks, proj, proj, proj, proj, proj, cos, sin, cos, sin)


def _swa_step_kernel(sink_ref, p_ref, ck_ref, cv_ref, cos_ref, sin_ref, o_ref, nk_ref, nv_ref):
    q = _rope(p_ref[0, :, 0:A_Q], cos_ref[...], sin_ref[...])
    kn = _rope(p_ref[0, :, A_Q:A_Q + A_KV], cos_ref[:, :A_KV], sin_ref[:, :A_KV])
    vn = p_ref[0, :, A_Q + A_KV:A_COLS]
    ck = ck_ref[0]
    cv = cv_ref[0]
    outs = []
    for h in range(A_HEADS):
        kv = h // A_GROUP
        hs = slice(h * HEAD_DIM, (h + 1) * HEAD_DIM)
        ks = slice(kv * HEAD_DIM, (kv + 1) * HEAD_DIM)
        qh = q[:, hs] * (HEAD_DIM ** -0.5)
        s = jnp.sum(ck[:, ks] * qh, axis=-1, keepdims=True)
        s_new = jnp.sum(kn[:, ks] * qh, axis=-1, keepdims=True)
        sink = sink_ref[h]
        mx = jnp.maximum(jnp.maximum(jnp.max(s, axis=0, keepdims=True), s_new), sink)
        e = jnp.exp(s - mx)
        e_new = jnp.exp(s_new - mx)
        den = jnp.sum(e, axis=0, keepdims=True) + e_new + jnp.exp(sink - mx)
        outs.append((jnp.sum(e * cv[:, ks], axis=0, keepdims=True) + e_new * vn[:, ks]) / den)
    o_ref[0] = jnp.concatenate(outs, axis=-1)
    nk_ref[0, 0:WINDOW - 1, :] = ck_ref[0, 1:WINDOW, :]
    nk_ref[0, WINDOW - 1:WINDOW, :] = kn
    nv_ref[0, 0:WINDOW - 1, :] = cv_ref[0, 1:WINDOW, :]
    nv_ref[0, WINDOW - 1:WINDOW, :] = vn


def _swa_step(proj, sinks, cache_k, cache_v, cos, sin):
    batch, width = proj.shape
    cache_spec = pl.BlockSpec((1, WINDOW, A_KV), lambda b: (b, 0, 0))
    return pl.pallas_call(
        _swa_step_kernel,
        grid=(batch,),
        in_specs=[
            pl.BlockSpec(memory_space=pltpu.SMEM),
            pl.BlockSpec((1, 1, width), lambda b: (b, 0, 0)),
            cache_spec, cache_spec,
            pl.BlockSpec((1, A_Q), lambda b: (0, 0)),
            pl.BlockSpec((1, A_Q), lambda b: (0, 0)),
        ],
        out_specs=[pl.BlockSpec((1, 1, A_Q), lambda b: (b, 0, 0)), cache_spec, cache_spec],
        out_shape=[
            jax.ShapeDtypeStruct((batch, 1, A_Q), F32),
            jax.ShapeDtypeStruct((batch, WINDOW, A_KV), F32),
            jax.ShapeDtypeStruct((batch, WINDOW, A_KV), F32),
        ],
        compiler_params=_params(("parallel",)),
        name="swa_step",
    )(sinks, proj.reshape(batch, 1, width), cache_k, cache_v, cos, sin)


def _softplus(z):
    return jnp.maximum(z, 0.0) + jnp.log(1.0 + jnp.exp(-jnp.abs(z)))


def _rwkv_in_math(pb, prev, mu, w0, wdec, a0, waaa, wgate, r_ref, k_ref, v_ref, lw_ref, a_ref, g_ref):
    xm = pb + (prev - pb) * mu
    r_ref[...] = xm[:, 0:B_WIDTH]
    k_ref[...] = xm[:, B_WIDTH:2 * B_WIDTH]
    v_ref[...] = xm[:, 2 * B_WIDTH:3 * B_WIDTH]
    lora = xm[:, 3 * B_WIDTH:3 * B_WIDTH + LORA_IN]
    dg = xm[:, 3 * B_WIDTH + LORA_IN:]
    w_log = -_softplus(-(w0 + _dot(jnp.tanh(lora).astype(BF16), wdec))) - 0.5
    lw_ref[...] = -jnp.exp(w_log)
    a_ref[...] = jax.nn.sigmoid(a0 + _dot(lora.astype(BF16), waaa))
    g_ref[...] = _dot(jax.nn.sigmoid(dg).astype(BF16), wgate)


def _rwkv_in_seq_kernel(p_ref, halo_ref, mu_ref, w0_ref, wdec_ref, a0_ref, waaa_ref, wgate_ref,
                        r_ref, k_ref, v_ref, lw_ref, a_ref, g_ref, *, tiles_per_seq):
    pb = p_ref[:, A_COLS:]
    first = (pl.program_id(0) % tiles_per_seq) == 0
    hrow = jnp.where(first, 0.0, halo_ref[7:8, A_COLS:])
    row = lax.broadcasted_iota(jnp.int32, pb.shape, 0)
    prev = jnp.where(row == 0, hrow, pltpu.roll(pb, 1, 0))
    _rwkv_in_math(pb, prev, mu_ref[...], w0_ref[...], wdec_ref[...], a0_ref[...], waaa_ref[...],
                  wgate_ref[...], r_ref, k_ref, v_ref, lw_ref, a_ref, g_ref)


def _rwkv_in_step_kernel(p_ref, prev_ref, mu_ref, w0_ref, wdec_ref, a0_ref, waaa_ref, wgate_ref,
                         r_ref, k_ref, v_ref, lw_ref, a_ref, g_ref):
    _rwkv_in_math(p_ref[:, A_COLS:], prev_ref[...], mu_ref[...], w0_ref[...], wdec_ref[...], a0_ref[...],
                  waaa_ref[...], wgate_ref[...], r_ref, k_ref, v_ref, lw_ref, a_ref, g_ref)


def _rwkv_in(proj, prev, wts, tt, seq):
    m, width = proj.shape
    const = lambda a: pl.BlockSpec(a.shape, lambda i: (0,) * a.ndim)
    out_spec = pl.BlockSpec((tt, B_WIDTH), lambda i: (i, 0))
    if prev is None:
        kern = functools.partial(_rwkv_in_seq_kernel, tiles_per_seq=seq // tt)
        second = proj
        second_spec = pl.BlockSpec((8, width), lambda i: (jnp.maximum(i * (tt // 8) - 1, 0), 0))
    else:
        kern = _rwkv_in_step_kernel
        second = prev
        second_spec = pl.BlockSpec((tt, B_COLS_PAD), lambda i: (i, 0))
    return pl.pallas_call(
        kern,
        grid=(m // tt,),
        in_specs=[pl.BlockSpec((tt, width), lambda i: (i, 0)), second_spec] + [const(a) for a in wts],
        out_specs=[out_spec] * 6,
        out_shape=[jax.ShapeDtypeStruct((m, B_WIDTH), F32)] * 6,
        compiler_params=_params(("parallel",)),
        name="rwkv_in",
    )(proj, second, *wts)


def _split(a):
    hi = a.astype(BF16)
    return hi, (a - hi.astype(F32)).astype(BF16)


def _bdot(a, b, dn):
    ah, al = _split(a)
    bh, bl = _split(b)
    d = lambda x, y: lax.dot_general(x, y, dn, preferred_element_type=F32)
    return d(ah, bh) + (d(ah, bl) + d(al, bh))


_NT = (((2,), (2,)), ((0,), (0,)))
_NN = (((2,), (1,)), ((0,), (0,)))
_TN = (((1,), (1,)), ((0,), (0,)))


def _unit_lower_inverse(a, eye, same_block):
    d = jnp.where(same_block, a, 0.0)
    e = a - d
    x = eye + d
    dp = d
    p = 1
    while 2 * p < RWKV_INV_BLOCK:
        dp = _bdot(dp, dp, _NN)
        x = _bdot(x, eye + dp, _NN)
        p *= 2
    nn = _bdot(x, e, _NN)
    return _bdot(_bdot(eye + nn, eye + _bdot(nn, nn, _NN), _NN), x, _NN)


def _rwkv_scan_kernel(r_ref, k_ref, v_ref, lw_ref, a_ref, s0_ref, kk_ref, ka_ref, rk_ref, gw_ref, gb_ref,
                      y_ref, sT_ref, st_ref):
    c = pl.program_id(2)

    @pl.when(c == 0)
    def _():
        st_ref[...] = s0_ref[0]

    r = r_ref[0]
    k = k_ref[0]
    v = v_ref[0]
    lw = lw_ref[0]
    a = a_ref[0]
    hg, cs, _ = r.shape
    st = st_ref[...]

    ti = lax.broadcasted_iota(jnp.int32, (hg, cs, cs), 1)
    si = lax.broadcasted_iota(jnp.int32, (hg, cs, cs), 2)
    lower = si <= ti
    strict = si < ti
    eye = jnp.where(si == ti, 1.0, 0.0).astype(F32)
    same_block = (si // RWKV_INV_BLOCK) == (ti // RWKV_INV_BLOCK)

    lcum = _bdot(jnp.where(lower, 1.0, 0.0).astype(F32), lw, _NN)
    p_t = jnp.exp(lcum)
    p_inv = jnp.exp(-lcum)
    p_prev = jnp.exp(lcum - lw)
    p_end = jnp.exp(lcum[:, cs - 1:cs, :])

    kk = k * kk_ref[...]
    kk = kk / jnp.maximum(jnp.sqrt(jnp.sum(kk * kk, axis=-1, keepdims=True)), 1e-12)
    k2 = k * (1.0 + (a - 1.0) * ka_ref[...])
    al = -kk * p_prev
    be = kk * a * p_inv
    kt = k2 * p_inv
    rt = r * p_t

    a_ab = jnp.where(strict, _bdot(al, be, _NT), 0.0)
    a_ak = jnp.where(strict, _bdot(al, kt, _NT), 0.0)
    m_rb = jnp.where(lower, _bdot(rt, be, _NT), 0.0)
    m_rk = jnp.where(lower, _bdot(rt, kt, _NT), 0.0)
    tinv = _unit_lower_inverse(a_ab, eye, same_block)
    u = _bdot(tinv, _bdot(al, st, _NN) + _bdot(a_ak, v, _NN), _NN)
    y = _bdot(rt, st, _NN) + _bdot(m_rb, u, _NN) + _bdot(m_rk, v, _NN)

    ki = lax.broadcasted_iota(jnp.int32, (hg, HEAD_DIM, HEAD_DIM), 1)
    kj = lax.broadcasted_iota(jnp.int32, (hg, HEAD_DIM, HEAD_DIM), 2)
    dmat = jnp.where(ki == kj, 1.0, 0.0).astype(F32) * p_end
    lhs = jnp.concatenate([dmat, be * p_end, kt * p_end], axis=1)
    rhs = jnp.concatenate([st, u, v], axis=1)
    st_new = _bdot(lhs, rhs, _TN)
    st_ref[...] = st_new

    mean = jnp.mean(y, axis=-1, keepdims=True)
    yc = y - mean
    var = jnp.mean(yc * yc, axis=-1, keepdims=True)
    yn = yc * lax.rsqrt(var + GN_EPS) * gw_ref[...] + gb_ref[...]
    y_ref[0] = yn + jnp.sum(r * k2 * rk_ref[...], axis=-1, keepdims=True) * v

    @pl.when(c == pl.num_programs(2) - 1)
    def _():
        sT_ref[0] = st_new


def _rwkv_scan(r, k, v, lw, a, s0t, head_wts, cs, hg):
    batch, heads, t, n = r.shape
    seq_spec = pl.BlockSpec((1, hg, cs, n), lambda b, h, c: (b, h, c, 0))
    st_spec = pl.BlockSpec((1, hg, n, n), lambda b, h, c: (b, h, 0, 0))
    hw_spec = pl.BlockSpec((hg, 1, n), lambda b, h, c: (h, 0, 0))
    return pl.pallas_call(
        _rwkv_scan_kernel,
        grid=(batch, heads // hg, t // cs),
        in_specs=[seq_spec] * 5 + [st_spec] + [hw_spec] * 5,
        out_specs=[seq_spec, st_spec],
        out_shape=[jax.ShapeDtypeStruct((batch, heads, t, n), F32),
                   jax.ShapeDtypeStruct((batch, heads, n, n), F32)],
        scratch_shapes=[pltpu.VMEM((hg, n, n), F32)],
        compiler_params=_params(("parallel", "parallel", "arbitrary")),
        name="rwkv_scan",
    )(r, k, v, lw, a, s0t, *head_wts)


def _ab_out_kernel(x_ref, ao_ref, yb_ref, gt_ref, wa_ref, wb_ref, g_ref, o_ref):
    mix = _dot(ao_ref[...].astype(BF16), wa_ref[...])
    mix += _dot((yb_ref[...] * gt_ref[...]).astype(BF16), wb_ref[...])
    o_ref[...] = x_ref[...] + _rms(mix, g_ref[...])


def _ab_out(x, ao, yb, gt, w_out, g, tm):
    m, d = x.shape
    return pl.pallas_call(
        _ab_out_kernel,
        grid=(m // tm,),
        in_specs=[
            pl.BlockSpec((tm, d), lambda i: (i, 0)),
            pl.BlockSpec((tm, A_Q), lambda i: (i, 0)),
            pl.BlockSpec((tm, B_WIDTH), lambda i: (i, 0)),
            pl.BlockSpec((tm, B_WIDTH), lambda i: (i, 0)),
            pl.BlockSpec((A_Q, d), lambda i: (0, 0)),
            pl.BlockSpec((B_WIDTH, d), lambda i: (1, 0)),
            pl.BlockSpec((1, d), lambda i: (0, 0)),
        ],
        out_specs=pl.BlockSpec((tm, d), lambda i: (i, 0)),
        out_shape=jax.ShapeDtypeStruct((m, d), F32),
        compiler_params=_params(("parallel",)),
        name="ab_out",
    )(x, ao, yb, gt, w_out, w_out, g)


def _rope_tables(pos):
    half = HEAD_DIM // 2
    freqs = ROPE_THETA ** (-jnp.arange(half, dtype=F32) / half)
    ang = pos.astype(F32)[:, None] * freqs[None, :]
    cos = jnp.cos(ang)
    sin = jnp.sin(ang)
    return (jnp.tile(jnp.concatenate([cos, cos], axis=-1), (1, A_HEADS)),
            jnp.tile(jnp.concatenate([-sin, sin], axis=-1), (1, A_HEADS)))


def _heads(t, batch, seq):
    return t.reshape(batch, seq, B_HEADS, HEAD_DIM).transpose(0, 2, 1, 3)


def _trunk(x3, past, w):
    batch, seq, d = x3.shape
    m = batch * seq
    step = past is not None
    x = x3.reshape(m, d)
    tm = min(512, m)
    tt = min(256, m)
    start = PAST_LEN if step else 0
    if not step:
        assert seq % tm == 0 and seq % tt == 0 and seq % RWKV_CHUNK == 0 and seq % WINDOW == 0
    cos, sin = _rope_tables(start + jnp.arange(seq, dtype=jnp.int32))
    out_k, out_v, out_s, out_sh, out_c = [], [], [], [], []
    depth = w["norm_g"].shape[0]
    for l in range(depth):
        g = w["norm_g"][l][:, None, :]
        i = l // 2
        x = _ffn(x, g[0], g[1], w["ffn_w_gu"][l, 0], w["ffn_w_down"][l, 0], tm, 512)
        if l % 2 == 0:
            proj = _norm_mm(x, g[2], w["ab_w_in"][i], tm, 1280)
            if step:
                ao, nk, nv = _swa_step(proj, w["attn_sinks"][i], past[0][i].reshape(batch, WINDOW, A_KV),
                                       past[1][i].reshape(batch, WINDOW, A_KV), cos, sin)
                ao = ao.reshape(m, A_Q)
                prev = jnp.pad(past[3][i], ((0, 0), (0, B_COLS_PAD - B_COLS)))
                s0t = jnp.swapaxes(past[2][i], -1, -2)
            else:
                ao, nk, nv = _swa_seq(proj, w["attn_sinks"][i], cos, sin, batch, seq)
                prev = None
                s0t = jnp.zeros((batch, B_HEADS, HEAD_DIM, HEAD_DIM), F32)
            r, k, v, lw, a, gt = _rwkv_in(proj, prev, w["rwkv_in"][i], tt, seq)
            if step:
                cs, tpad = 8, 8
                hm = lambda t: jnp.pad(_heads(t, batch, seq), ((0, 0), (0, 0), (0, tpad - seq), (0, 0)))
            else:
                cs, tpad = RWKV_CHUNK, seq
                hm = lambda t: _heads(t, batch, seq)
            yb, st = _rwkv_scan(hm(r), hm(k), hm(v), hm(lw), hm(a), s0t, w["rwkv_head"][i], cs, 8)
            yb = yb[:, :, :seq].transpose(0, 2, 1, 3).reshape(m, B_WIDTH)
            x = _ab_out(x, ao, yb, gt, w["ab_w_out"][i], g[3], tm)
            out_k.append(nk.reshape(batch, WINDOW, A_KV_HEADS, HEAD_DIM))
            out_v.append(nv.reshape(batch, WINDOW, A_KV_HEADS, HEAD_DIM))
            out_s.append(jnp.swapaxes(st, -1, -2))
            out_sh.append(proj.reshape(batch, seq, AB_COLS_PAD)[:, -1, A_COLS:AB_COLS])
        else:
            bg, u = _conv_in(x, g[2], w["conv_w_in"][i], tm, 512)
            if step:
                buf = past[4][i]
                x = _conv_out_step(x, bg, u, buf[:, 1], buf[:, 0], w["conv_w"][i], w["conv_w_out"][i], g[3])
                out_c.append(jnp.stack([buf[:, 1], u], axis=1))
            else:
                x = _conv_out_seq(x, bg, u, w["conv_w"][i], w["conv_w_out"][i], g[3], tm, seq)
                out_c.append(u.reshape(batch, seq, -1)[:, -(CONV_W - 1):])
        x = _ffn(x, g[4], g[5], w["ffn_w_gu"][l, 1], w["ffn_w_down"][l, 1], tm, 512)
    return x.reshape(batch, seq, d), (jnp.stack(out_k), jnp.stack(out_v), jnp.stack(out_s),
                                      jnp.stack(out_sh), jnp.stack(out_c))


def _prep_weights(norm_g, ffn_w_gu, ffn_w_down, ab_w_in, ab_w_out, attn_sinks, rwkv_mu, rwkv_w0, rwkv_w_decay,
                  rwkv_a0, rwkv_w_aaa, rwkv_w_gate, rwkv_k_k, rwkv_k_a, rwkv_r_k, rwkv_gn_w, rwkv_gn_b,
                  conv_w_in, conv_w, conv_w_out):
    n_ab = ab_w_in.shape[0]
    padc = lambda t, n: jnp.pad(t, [(0, 0)] * (t.ndim - 1) + [(0, n - t.shape[-1])])
    per_head = lambda t: t.reshape(n_ab, B_HEADS, 1, HEAD_DIM)
    row = lambda t: t[:, None, :]
    wdec = jnp.pad(rwkv_w_decay, ((0, 0), (0, D_AAA), (0, 0))).astype(BF16)
    waaa = jnp.pad(rwkv_w_aaa, ((0, 0), (D_DECAY, 0), (0, 0))).astype(BF16)
    wgate = jnp.pad(rwkv_w_gate, ((0, 0), (0, GATE_PAD - D_GATE), (0, 0))).astype(BF16)
    return {
        "norm_g": norm_g,
        "ffn_w_gu": ffn_w_gu.astype(BF16),
        "ffn_w_down": ffn_w_down.astype(BF16),
        "ab_w_in": padc(ab_w_in, AB_COLS_PAD).astype(BF16),
        "ab_w_out": ab_w_out.astype(BF16),
        "attn_sinks": attn_sinks,
        "rwkv_in": [(row(padc(rwkv_mu, B_COLS_PAD))[j], row(rwkv_w0)[j], wdec[j], row(rwkv_a0)[j], waaa[j],
                     wgate[j]) for j in range(n_ab)],
        "rwkv_head": [(per_head(rwkv_k_k)[j], per_head(rwkv_k_a)[j], per_head(rwkv_r_k)[j],
                       per_head(rwkv_gn_w)[j], per_head(rwkv_gn_b)[j]) for j in range(n_ab)],
        "conv_w_in": conv_w_in.astype(BF16),
        "conv_w": conv_w,
        "conv_w_out": conv_w_out.astype(BF16),
    }


def kernel(x_prompt, x_sample, cache_swa_k, cache_swa_v, state_rwkv, state_rwkv_shift, state_conv, norm_g, ffn_w_gu, ffn_w_down, ab_w_in, ab_w_out, attn_sinks, rwkv_mu, rwkv_w0, rwkv_w_decay, rwkv_a0, rwkv_w_aaa, rwkv_w_gate, rwkv_k_k, rwkv_k_a, rwkv_r_k, rwkv_gn_w, rwkv_gn_b, conv_w_in, conv_w, conv_w_out):
    w = _prep_weights(norm_g, ffn_w_gu, ffn_w_down, ab_w_in, ab_w_out, attn_sinks, rwkv_mu, rwkv_w0, rwkv_w_decay,
                      rwkv_a0, rwkv_w_aaa, rwkv_w_gate, rwkv_k_k, rwkv_k_a, rwkv_r_k, rwkv_gn_w, rwkv_gn_b,
                      conv_w_in, conv_w, conv_w_out)
    y_prompt, (p_k, p_v, p_s, p_sh, p_c) = _trunk(x_prompt, None, w)
    y_sample, (s_k, s_v, s_s, s_sh, s_c) = _trunk(
        x_sample, (cache_swa_k, cache_swa_v, state_rwkv, state_rwkv_shift, state_conv), w)
    return (y_prompt, y_sample, p_k, p_v, p_s, p_sh, p_c, s_k, s_v, s_s, s_sh, s_c)
```

```python
import functools

import jax
import jax.numpy as jnp
from jax import lax
from jax.experimental import pallas as pl
from jax.experimental.pallas import tpu as pltpu

F32 = jnp.float32
BF16 = jnp.bfloat16

HEAD_DIM = 64
A_HEADS = 16
A_KV_HEADS = 4
A_GROUP = A_HEADS // A_KV_HEADS
WINDOW = 128
ROPE_THETA = 10000.0
PAST_LEN = 16384
A_Q = A_HEADS * HEAD_DIM
A_KV = A_KV_HEADS * HEAD_DIM
A_COLS = A_Q + 2 * A_KV
B_HEADS = 16
B_WIDTH = B_HEADS * HEAD_DIM
D_DECAY = 64
D_AAA = 64
D_GATE = 160
B_COLS = 3 * B_WIDTH + D_DECAY + D_AAA + D_GATE
GN_EPS = 64e-5
AB_COLS = A_COLS + B_COLS
CONV_W = 3
NORM_EPS = 1e-6

LANE = 128
HEAD_PAIRS = B_WIDTH // LANE
AB_COLS_PAD = 5120
B_COLS_PAD = AB_COLS_PAD - A_COLS
LORA_IN = D_DECAY + D_AAA
GATE_PAD = AB_COLS_PAD - A_COLS - 3 * B_WIDTH - LORA_IN
RWKV_CHUNK = 64
RWKV_STEP_CHUNK = 16
RWKV_INV_BLOCK = 16
VMEM_LIMIT = 56 * 1024 * 1024
FFN_VMEM_LIMIT = 60 * 1024 * 1024


def _tiles(m):
    return dict(ffn_m=min(1024, m), ffn_f=256 if m >= 1024 else 512, mix_m=min(512, m), rwkv_m=min(256, m))


def _params(sem, vmem_limit=VMEM_LIMIT):
    return pltpu.CompilerParams(dimension_semantics=sem, vmem_limit_bytes=vmem_limit)


def _layer(arr, idx):
    idx = tuple(idx)
    rest = arr.shape[len(idx):]
    return pl.BlockSpec((None,) * len(idx) + rest, lambda *_: idx + (0,) * len(rest))


def _rms(x, g):
    return x * lax.rsqrt(jnp.mean(x * x, axis=-1, keepdims=True) + NORM_EPS) * g


def _dot(a, b):
    return jnp.dot(a, b, preferred_element_type=F32)


def _ffn_kernel(x_ref, g0_ref, g1_ref, wg_ref, wu_ref, wd_ref, o_ref, h_ref):
    f = pl.program_id(1)

    @pl.when(f == 0)
    def _():
        h_ref[...] = _rms(x_ref[...], g0_ref[...]).astype(BF16)
        o_ref[...] = jnp.zeros_like(o_ref)

    h = h_ref[...]
    gate = _dot(h, wg_ref[...].astype(BF16))
    up = _dot(h, wu_ref[...].astype(BF16))
    act = (gate * jax.nn.sigmoid(gate) * up).astype(BF16)
    o_ref[...] += _dot(act, wd_ref[...].astype(BF16))

    @pl.when(f == pl.num_programs(1) - 1)
    def _():
        o_ref[...] = x_ref[...] + 0.5 * _rms(o_ref[...], g1_ref[...])


def _ffn(x, norm_g, g_rows, w_gu, w_down, lj, tm, tf):
    m, d = x.shape
    l, j = lj
    nf = w_down.shape[2] // tf
    return pl.pallas_call(
        _ffn_kernel,
        grid=(m // tm, nf),
        in_specs=[
            pl.BlockSpec((tm, d), lambda i, f: (i, 0)),
            _layer(norm_g, (g_rows[0],)),
            _layer(norm_g, (g_rows[1],)),
            pl.BlockSpec((None, None, d, tf), lambda i, f: (l, j, 0, f)),
            pl.BlockSpec((None, None, d, tf), lambda i, f: (l, j, 0, nf + f)),
            pl.BlockSpec((None, None, tf, d), lambda i, f: (l, j, f, 0)),
        ],
        out_specs=pl.BlockSpec((tm, d), lambda i, f: (i, 0)),
        out_shape=jax.ShapeDtypeStruct((m, d), F32),
        scratch_shapes=[pltpu.VMEM((tm, d), BF16)],
        compiler_params=_params(("parallel", "arbitrary"), FFN_VMEM_LIMIT),
        name="ffn",
    )(x, norm_g, norm_g, w_gu, w_gu, w_down)


def _norm_mm_kernel(x_ref, g_ref, w_ref, o_ref, h_ref):
    @pl.when(pl.program_id(1) == 0)
    def _():
        h_ref[...] = _rms(x_ref[...], g_ref[...]).astype(BF16)

    o_ref[...] = _dot(h_ref[...], w_ref[...])


def _norm_mm(x, norm_g, g_row, w, li, tm, tn):
    m, d = x.shape
    n = w.shape[-1]
    return pl.pallas_call(
        _norm_mm_kernel,
        grid=(m // tm, n // tn),
        in_specs=[
            pl.BlockSpec((tm, d), lambda i, j: (i, 0)),
            _layer(norm_g, (g_row,)),
            pl.BlockSpec((None, d, tn), lambda i, j: (li, 0, j)),
        ],
        out_specs=pl.BlockSpec((tm, tn), lambda i, j: (i, j)),
        out_shape=jax.ShapeDtypeStruct((m, n), F32),
        scratch_shapes=[pltpu.VMEM((tm, d), BF16)],
        compiler_params=_params(("parallel", "arbitrary")),
        name="ab_in",
    )(x, norm_g, w)


def _conv_in_kernel(x_ref, g_ref, wb_ref, wc_ref, wh_ref, bg_ref, u_ref, h_ref):
    @pl.when(pl.program_id(1) == 0)
    def _():
        h_ref[...] = _rms(x_ref[...], g_ref[...]).astype(BF16)

    h = h_ref[...]
    bg_ref[...] = _dot(h, wb_ref[...])
    u_ref[...] = _dot(h, wc_ref[...]) * _dot(h, wh_ref[...])


def _conv_in(x, norm_g, g_row, w_in, li, tm, tn):
    m, d = x.shape
    dc = w_in.shape[-1] // 3
    nj = dc // tn
    return pl.pallas_call(
        _conv_in_kernel,
        grid=(m // tm, nj),
        in_specs=[
            pl.BlockSpec((tm, d), lambda i, j: (i, 0)),
            _layer(norm_g, (g_row,)),
            pl.BlockSpec((None, d, tn), lambda i, j: (li, 0, j)),
            pl.BlockSpec((None, d, tn), lambda i, j: (li, 0, nj + j)),
            pl.BlockSpec((None, d, tn), lambda i, j: (li, 0, 2 * nj + j)),
        ],
        out_specs=[pl.BlockSpec((tm, tn), lambda i, j: (i, j)),
                   pl.BlockSpec((tm, tn), lambda i, j: (i, j))],
        out_shape=[jax.ShapeDtypeStruct((m, dc), F32), jax.ShapeDtypeStruct((m, dc), F32)],
        scratch_shapes=[pltpu.VMEM((tm, d), BF16)],
        compiler_params=_params(("parallel", "arbitrary")),
        name="conv_in",
    )(x, norm_g, w_in, w_in, w_in)


def _conv_taps(u, p1, p2, cw):
    return p2 * cw[0:1, :] + p1 * cw[1:2, :] + u * cw[2:3, :]


def _conv_out_seq_kernel(x_ref, bg_ref, u_ref, halo_ref, cw_ref, w_ref, g_ref, o_ref, *, tiles_per_seq):
    u = u_ref[...]
    first = (pl.program_id(0) % tiles_per_seq) == 0
    h1 = jnp.where(first, 0.0, halo_ref[7:8, :])
    h2 = jnp.where(first, 0.0, halo_ref[6:7, :])
    row = lax.broadcasted_iota(jnp.int32, u.shape, 0)
    p1 = jnp.where(row == 0, h1, pltpu.roll(u, 1, 0))
    p2 = jnp.where(row == 0, h2, jnp.where(row == 1, h1, pltpu.roll(u, 2, 0)))
    z = (bg_ref[...] * _conv_taps(u, p1, p2, cw_ref[...])).astype(BF16)
    o_ref[...] = x_ref[...] + _rms(_dot(z, w_ref[...]), g_ref[...])


def _conv_out_step_kernel(x_ref, bg_ref, u_ref, p1_ref, p2_ref, cw_ref, w_ref, g_ref, o_ref):
    z = (bg_ref[...] * _conv_taps(u_ref[...], p1_ref[...], p2_ref[...], cw_ref[...])).astype(BF16)
    o_ref[...] = x_ref[...] + _rms(_dot(z, w_ref[...]), g_ref[...])


def _conv_out_seq(x, bg, u, conv_w, w_out, norm_g, g_row, li, tm, seq):
    m, d = x.shape
    dc = u.shape[1]
    hb = tm // 8
    row_spec = pl.BlockSpec((tm, dc), lambda i: (i, 0))
    return pl.pallas_call(
        functools.partial(_conv_out_seq_kernel, tiles_per_seq=seq // tm),
        grid=(m // tm,),
        in_specs=[
            pl.BlockSpec((tm, d), lambda i: (i, 0)),
            row_spec, row_spec,
            pl.BlockSpec((8, dc), lambda i: (jnp.maximum(i * hb - 1, 0), 0)),
            _layer(conv_w, (li,)), _layer(w_out, (li,)), _layer(norm_g, (g_row,)),
        ],
        out_specs=pl.BlockSpec((tm, d), lambda i: (i, 0)),
        out_shape=jax.ShapeDtypeStruct((m, d), F32),
        compiler_params=_params(("parallel",)),
        name="conv_out",
    )(x, bg, u, u, conv_w, w_out, norm_g)


def _conv_out_step(x, bg, u, p1, p2, conv_w, w_out, norm_g, g_row, li):
    m, d = x.shape
    dc = u.shape[1]
    row_spec = pl.BlockSpec((m, dc), lambda i: (0, 0))
    return pl.pallas_call(
        _conv_out_step_kernel,
        grid=(1,),
        in_specs=[
            pl.BlockSpec((m, d), lambda i: (0, 0)),
            row_spec, row_spec, row_spec, row_spec,
            _layer(conv_w, (li,)), _layer(w_out, (li,)), _layer(norm_g, (g_row,)),
        ],
        out_specs=pl.BlockSpec((m, d), lambda i: (0, 0)),
        out_shape=jax.ShapeDtypeStruct((m, d), F32),
        compiler_params=_params(("arbitrary",)),
        name="conv_out_step",
    )(x, bg, u, p1, p2, conv_w, w_out, norm_g)


def _rope(x, cos, sin):
    width = x.shape[-1]
    half = HEAD_DIM // 2
    lane = lax.broadcasted_iota(jnp.int32, x.shape, x.ndim - 1)
    swapped = jnp.where((lane % HEAD_DIM) < half,
                        pltpu.roll(x, width - half, x.ndim - 1),
                        pltpu.roll(x, half, x.ndim - 1))
    return x * cos + swapped * sin


def _swa_seq_kernel(sink_ref, q_ref, kc_ref, kp_ref, vc_ref, vp_ref, cq_ref, sq_ref, cp_ref, sp_ref,
                    o_ref, nk_ref, nv_ref, *, li):
    n = pl.program_id(1)
    q = _rope(q_ref[...], cq_ref[...], sq_ref[...])
    kc = _rope(kc_ref[...], cq_ref[:, :A_KV], sq_ref[:, :A_KV])
    kp = _rope(kp_ref[...], cp_ref[...], sp_ref[...])
    kw = jnp.concatenate([kp, kc], axis=0).astype(BF16)
    vw = jnp.concatenate([vp_ref[...], vc_ref[...]], axis=0).astype(BF16)
    qi = lax.broadcasted_iota(jnp.int32, (WINDOW, 2 * WINDOW), 0)
    kj = lax.broadcasted_iota(jnp.int32, (WINDOW, 2 * WINDOW), 1)
    diff = WINDOW + qi - kj
    lo = jnp.where(n > 0, 0, WINDOW)
    valid = (diff >= 0) & (diff <= WINDOW) & (kj >= lo)
    outs = []
    for h in range(A_HEADS):
        kv = h // A_GROUP
        qh = q[:, h * HEAD_DIM:(h + 1) * HEAD_DIM].astype(BF16)
        kh = kw[:, kv * HEAD_DIM:(kv + 1) * HEAD_DIM]
        vh = vw[:, kv * HEAD_DIM:(kv + 1) * HEAD_DIM]
        s = lax.dot_general(qh, kh, (((1,), (1,)), ((), ())), preferred_element_type=F32)
        s = jnp.where(valid, s * (HEAD_DIM ** -0.5), -jnp.inf)
        sink = sink_ref[li, h]
        mx = jnp.maximum(jnp.max(s, axis=-1, keepdims=True), sink)
        e = jnp.exp(s - mx)
        p = e / (jnp.sum(e, axis=-1, keepdims=True) + jnp.exp(sink - mx))
        outs.append(_dot(p.astype(BF16), vh))
    o_ref[...] = jnp.concatenate(outs, axis=-1)

    @pl.when(n == pl.num_programs(1) - 1)
    def _():
        nk_ref[...] = kc
        nv_ref[...] = vc_ref[...]


def _swa_seq(proj, sinks, li, cos, sin, batch, seq):
    nb = seq // WINDOW
    kcol = A_Q // A_KV
    vcol = kcol + 1
    cur = lambda b, n: b * nb + n
    prev = lambda b, n: b * nb + jnp.maximum(n - 1, 0)
    return pl.pallas_call(
        functools.partial(_swa_seq_kernel, li=li),
        grid=(batch, nb),
        in_specs=[
            pl.BlockSpec(memory_space=pltpu.SMEM),
            pl.BlockSpec((WINDOW, A_Q), lambda b, n: (cur(b, n), 0)),
            pl.BlockSpec((WINDOW, A_KV), lambda b, n: (cur(b, n), kcol)),
            pl.BlockSpec((WINDOW, A_KV), lambda b, n: (prev(b, n), kcol)),
            pl.BlockSpec((WINDOW, A_KV), lambda b, n: (cur(b, n), vcol)),
            pl.BlockSpec((WINDOW, A_KV), lambda b, n: (prev(b, n), vcol)),
            pl.BlockSpec((WINDOW, A_Q), lambda b, n: (n, 0)),
            pl.BlockSpec((WINDOW, A_Q), lambda b, n: (n, 0)),
            pl.BlockSpec((WINDOW, A_KV), lambda b, n: (jnp.maximum(n - 1, 0), 0)),
            pl.BlockSpec((WINDOW, A_KV), lambda b, n: (jnp.maximum(n - 1, 0), 0)),
        ],
        out_specs=[
            pl.BlockSpec((WINDOW, A_Q), lambda b, n: (cur(b, n), 0)),
            pl.BlockSpec((WINDOW, A_KV), lambda b, n: (b, 0)),
            pl.BlockSpec((WINDOW, A_KV), lambda b, n: (b, 0)),
        ],
        out_shape=[
            jax.ShapeDtypeStruct((batch * seq, A_Q), F32),
            jax.ShapeDtypeStruct((batch * WINDOW, A_KV), F32),
            jax.ShapeDtypeStruct((batch * WINDOW, A_KV), F32),
        ],
        compiler_params=_params(("parallel", "arbitrary")),
        name="swa_seq",
    )(sinks, proj, proj, proj, proj, proj, cos, sin, cos, sin)


def _swa_step_kernel(sink_ref, p_ref, ck_ref, cv_ref, cos_ref, sin_ref, o_ref, nk_ref, nv_ref, *, li):
    q = _rope(p_ref[0, :, 0:A_Q], cos_ref[...], sin_ref[...])
    kn = _rope(p_ref[0, :, A_Q:A_Q + A_KV], cos_ref[:, :A_KV], sin_ref[:, :A_KV])
    vn = p_ref[0, :, A_Q + A_KV:A_COLS]
    ck = ck_ref[0]
    cv = cv_ref[0]
    outs = []
    for h in range(A_HEADS):
        kv = h // A_GROUP
        hs = slice(h * HEAD_DIM, (h + 1) * HEAD_DIM)
        ks = slice(kv * HEAD_DIM, (kv + 1) * HEAD_DIM)
        qh = q[:, hs] * (HEAD_DIM ** -0.5)
        s = jnp.sum(ck[:, ks] * qh, axis=-1, keepdims=True)
        s_new = jnp.sum(kn[:, ks] * qh, axis=-1, keepdims=True)
        sink = sink_ref[li, h]
        mx = jnp.maximum(jnp.maximum(jnp.max(s, axis=0, keepdims=True), s_new), sink)
        e = jnp.exp(s - mx)
        e_new = jnp.exp(s_new - mx)
        den = jnp.sum(e, axis=0, keepdims=True) + e_new + jnp.exp(sink - mx)
        outs.append((jnp.sum(e * cv[:, ks], axis=0, keepdims=True) + e_new * vn[:, ks]) / den)
    o_ref[0] = jnp.concatenate(outs, axis=-1)
    nk_ref[0, 0:WINDOW - 1, :] = ck_ref[0, 1:WINDOW, :]
    nk_ref[0, WINDOW - 1:WINDOW, :] = kn
    nv_ref[0, 0:WINDOW - 1, :] = cv_ref[0, 1:WINDOW, :]
    nv_ref[0, WINDOW - 1:WINDOW, :] = vn


def _swa_step(proj, sinks, li, cache_k, cache_v, cos, sin):
    batch, width = proj.shape
    cache_in = pl.BlockSpec((None, 1, WINDOW, A_KV), lambda b: (li, b, 0, 0))
    cache_out = pl.BlockSpec((1, WINDOW, A_KV), lambda b: (b, 0, 0))
    return pl.pallas_call(
        functools.partial(_swa_step_kernel, li=li),
        grid=(batch,),
        in_specs=[
            pl.BlockSpec(memory_space=pltpu.SMEM),
            pl.BlockSpec((1, 1, width), lambda b: (b, 0, 0)),
            cache_in, cache_in,
            pl.BlockSpec((1, A_Q), lambda b: (0, 0)),
            pl.BlockSpec((1, A_Q), lambda b: (0, 0)),
        ],
        out_specs=[pl.BlockSpec((1, 1, A_Q), lambda b: (b, 0, 0)), cache_out, cache_out],
        out_shape=[
            jax.ShapeDtypeStruct((batch, 1, A_Q), F32),
            jax.ShapeDtypeStruct((batch, WINDOW, A_KV), F32),
            jax.ShapeDtypeStruct((batch, WINDOW, A_KV), F32),
        ],
        compiler_params=_params(("parallel",)),
        name="swa_step",
    )(sinks, proj.reshape(batch, 1, width), cache_k, cache_v, cos, sin)


def _softplus(z):
    return jnp.maximum(z, 0.0) + jnp.log(1.0 + jnp.exp(-jnp.abs(z)))


def _rwkv_in_math(pb, prev, mu, w0, wdec, a0, waaa, wgate, r_ref, k_ref, v_ref, lw_ref, a_ref, g_ref):
    xm = pb + (prev - pb) * mu
    r_ref[...] = xm[:, 0:B_WIDTH]
    k_ref[...] = xm[:, B_WIDTH:2 * B_WIDTH]
    v_ref[...] = xm[:, 2 * B_WIDTH:3 * B_WIDTH]
    lora = xm[:, 3 * B_WIDTH:3 * B_WIDTH + LORA_IN]
    dg = xm[:, 3 * B_WIDTH + LORA_IN:]
    w_log = -_softplus(-(w0 + _dot(jnp.tanh(lora).astype(BF16), wdec))) - 0.5
    lw_ref[...] = -jnp.exp(w_log)
    a_ref[...] = jax.nn.sigmoid(a0 + _dot(lora.astype(BF16), waaa))
    g_ref[...] = _dot(jax.nn.sigmoid(dg).astype(BF16), wgate)


def _rwkv_in_seq_kernel(p_ref, halo_ref, mu_ref, w0_ref, wdec_ref, a0_ref, waaa_ref, wgate_ref,
                        r_ref, k_ref, v_ref, lw_ref, a_ref, g_ref, *, tiles_per_seq):
    pb = p_ref[:, A_COLS:]
    first = (pl.program_id(0) % tiles_per_seq) == 0
    hrow = jnp.where(first, 0.0, halo_ref[7:8, A_COLS:])
    row = lax.broadcasted_iota(jnp.int32, pb.shape, 0)
    prev = jnp.where(row == 0, hrow, pltpu.roll(pb, 1, 0))
    _rwkv_in_math(pb, prev, mu_ref[...], w0_ref[...], wdec_ref[...], a0_ref[...], waaa_ref[...],
                  wgate_ref[...], r_ref, k_ref, v_ref, lw_ref, a_ref, g_ref)


def _rwkv_in_step_kernel(p_ref, prev_ref, mu_ref, w0_ref, wdec_ref, a0_ref, waaa_ref, wgate_ref,
                         r_ref, k_ref, v_ref, lw_ref, a_ref, g_ref):
    _rwkv_in_math(p_ref[:, A_COLS:], prev_ref[...], mu_ref[...], w0_ref[...], wdec_ref[...], a0_ref[...],
                  waaa_ref[...], wgate_ref[...], r_ref, k_ref, v_ref, lw_ref, a_ref, g_ref)


def _rwkv_in(proj, prev, wts, li, tt, seq):
    m, width = proj.shape
    out_spec = pl.BlockSpec((tt, B_WIDTH), lambda i: (i, 0))
    if prev is None:
        kern = functools.partial(_rwkv_in_seq_kernel, tiles_per_seq=seq // tt)
        second = proj
        second_spec = pl.BlockSpec((8, width), lambda i: (jnp.maximum(i * (tt // 8) - 1, 0), 0))
    else:
        kern = _rwkv_in_step_kernel
        second = prev
        second_spec = pl.BlockSpec((tt, B_COLS_PAD), lambda i: (i, 0))
    return pl.pallas_call(
        kern,
        grid=(m // tt,),
        in_specs=[pl.BlockSpec((tt, width), lambda i: (i, 0)), second_spec] + [_layer(a, (li,)) for a in wts],
        out_specs=[out_spec] * 6,
        out_shape=[jax.ShapeDtypeStruct((m, B_WIDTH), F32)] * 6,
        compiler_params=_params(("parallel",)),
        name="rwkv_in",
    )(proj, second, *wts)


def _bdg(a, b, dn):
    return lax.dot_general(a.astype(BF16), b.astype(BF16), dn, preferred_element_type=F32)


def _bmm(a, b):
    return _bdg(a, b, (((2,), (1,)), ((0,), (0,))))


def _bmm_nt(a, b):
    return _bdg(a, b, (((2,), (2,)), ((0,), (0,))))


def _bmm_tn(a, b):
    return _bdg(a, b, (((1,), (1,)), ((0,), (0,))))


def _unit_lower_inverse(a, eye, same_block):
    d = jnp.where(same_block, a, 0.0)
    e = a - d
    x = eye + d
    dp = d
    p = 1
    while 2 * p < RWKV_INV_BLOCK:
        dp = _bmm(dp, dp)
        x = _bmm(x, eye + dp)
        p *= 2
    nn = _bmm(x, e)
    return _bmm(_bmm(eye + nn, eye + _bmm(nn, nn)), x)


def _head_sums(x, first_half):
    outs = []
    for p in range(HEAD_PAIRS):
        xp = x[:, p * LANE:(p + 1) * LANE]
        s0 = jnp.sum(jnp.where(first_half, xp, 0.0), axis=-1, keepdims=True)
        s1 = jnp.sum(jnp.where(first_half, 0.0, xp), axis=-1, keepdims=True)
        outs.append(jnp.where(first_half, s0, s1))
    return jnp.concatenate(outs, axis=-1)


def _rwkv_scan_kernel(r_ref, k_ref, v_ref, lw_ref, a_ref, s0_ref, kk_ref, ka_ref, rk_ref, gw_ref, gb_ref,
                      y_ref, sT_ref, st_ref):
    c = pl.program_id(1)

    @pl.when(c == 0)
    def _():
        st_ref[...] = s0_ref[0]

    r = r_ref[...]
    k = k_ref[...]
    v = v_ref[...]
    lw = lw_ref[...]
    a = a_ref[...]
    cs = r.shape[0]
    c2 = 2 * cs
    iota = lambda shape, dim: lax.broadcasted_iota(jnp.int32, shape, dim)

    tril = jnp.where(iota((cs, cs), 1) <= iota((cs, cs), 0), 1.0, 0.0).astype(BF16)
    lw_hi = lw.astype(BF16)
    rem = lw - lw_hi.astype(F32)
    lw_mid = rem.astype(BF16)
    lw_lo = (rem - lw_mid.astype(F32)).astype(BF16)
    lcum = _dot(tril, lw_hi) + (_dot(tril, lw_mid) + _dot(tril, lw_lo))
    p_t = jnp.exp(lcum)
    p_inv = jnp.exp(-lcum)
    p_prev = jnp.exp(lcum - lw)
    p_end = jnp.exp(lcum[cs - 1:cs, :])

    first_half = iota((cs, LANE), 1) < HEAD_DIM
    kk = k * kk_ref[...]
    kk = kk / jnp.maximum(jnp.sqrt(_head_sums(kk * kk, first_half)), 1e-12)
    k2 = k * (1.0 + (a - 1.0) * ka_ref[...])
    al = -kk * p_prev
    be = kk * a * p_inv
    kt = k2 * p_inv
    rt = r * p_t
    be_end = be * p_end
    kt_end = kt * p_end
    bonus = _head_sums(r * k2 * rk_ref[...], first_half) * v

    np_, one = HEAD_PAIRS, (1,)
    own_half = (iota(one + (c2, LANE), 1) // cs) == (iota(one + (c2, LANE), 2) // HEAD_DIM)
    t2 = iota(one + (c2, c2), 1)
    s2 = iota(one + (c2, c2), 2)
    same_head = (t2 // cs) == (s2 // cs)
    strict = same_head & (s2 < t2)
    lower = same_head & (s2 <= t2)
    same_block = (t2 // RWKV_INV_BLOCK) == (s2 // RWKV_INV_BLOCK)
    eye = jnp.where(t2 == s2, 1.0, 0.0)
    kl = iota(one + (LANE, LANE), 1)
    vl = iota(one + (LANE, LANE), 2)
    st_diag = kl == vl
    st_blocks = (kl // HEAD_DIM) == (vl // HEAD_DIM)

    pairs = lambda x: jnp.stack([x[:, p * LANE:(p + 1) * LANE] for p in range(np_)], axis=0)
    dup = lambda x: jnp.concatenate([x, x], axis=1)
    own = lambda x: jnp.where(own_half, dup(pairs(x)), 0.0).astype(BF16)
    al2 = own(al)
    rt2 = own(rt)
    v2 = own(v)
    prod = _bmm_nt(jnp.concatenate([al2, rt2], axis=1),
                   jnp.concatenate([dup(pairs(be)), dup(pairs(kt))], axis=1))
    a_ab = jnp.where(strict, prod[:, :c2, :c2], 0.0)
    a_ak = jnp.where(strict, prod[:, :c2, c2:], 0.0)
    m_rb = jnp.where(lower, prod[:, c2:, :c2], 0.0)
    m_rk = jnp.where(lower, prod[:, c2:, c2:], 0.0)
    tinv = _unit_lower_inverse(a_ab, eye, same_block)
    st = st_ref[...]
    u2 = _bmm(tinv, _bmm(al2, st) + _bmm(a_ak, v2))
    y2 = _bmm(rt2, st) + _bmm(m_rb, u2) + _bmm(m_rk, v2)
    y3 = y2[:, :cs] + y2[:, cs:]
    u = u2[:, :cs] + u2[:, cs:]
    p_col = jnp.sum(jnp.where(st_diag, pairs(p_end), 0.0), axis=2, keepdims=True)
    inc = _bmm_tn(pairs(be_end), u) + _bmm_tn(pairs(kt_end), pairs(v))
    st_ref[...] = st * p_col + jnp.where(st_blocks, inc, 0.0)

    y = jnp.concatenate([y3[p] for p in range(np_)], axis=-1)
    mean = _head_sums(y, first_half) * (1.0 / HEAD_DIM)
    yc = y - mean
    var = _head_sums(yc * yc, first_half) * (1.0 / HEAD_DIM)
    y_ref[...] = yc * lax.rsqrt(var + GN_EPS) * gw_ref[...] + gb_ref[...] + bonus

    @pl.when(c == pl.num_programs(1) - 1)
    def _():
        sT_ref[0] = st_ref[...]


def _rwkv_scan(r, k, v, lw, a, s0, head_wts, li, batch, cs):
    m = r.shape[0]
    nc = m // batch // cs
    seq_spec = pl.BlockSpec((cs, B_WIDTH), lambda b, c: (b * nc + c, 0))
    st_spec = pl.BlockSpec((1, HEAD_PAIRS, LANE, LANE), lambda b, c: (b, 0, 0, 0))
    return pl.pallas_call(
        _rwkv_scan_kernel,
        grid=(batch, nc),
        in_specs=[seq_spec] * 5 + [st_spec] + [_layer(t, (li,)) for t in head_wts],
        out_specs=[seq_spec, st_spec],
        out_shape=[jax.ShapeDtypeStruct((m, B_WIDTH), F32),
                   jax.ShapeDtypeStruct((batch, HEAD_PAIRS, LANE, LANE), F32)],
        scratch_shapes=[pltpu.VMEM((HEAD_PAIRS, LANE, LANE), F32)],
        compiler_params=_params(("parallel", "arbitrary")),
        name="rwkv_scan",
    )(r, k, v, lw, a, s0, *head_wts)


def _state_to_pairs(s):
    b = s.shape[0]
    st = jnp.swapaxes(s, -1, -2).reshape(b, HEAD_PAIRS, 2, HEAD_DIM, 1, HEAD_DIM)
    sel = jnp.eye(2, dtype=s.dtype).reshape(1, 1, 2, 1, 2, 1)
    return (st * sel).reshape(b, HEAD_PAIRS, LANE, LANE)


def _pairs_to_state(sp):
    b = sp.shape[0]
    t = sp.reshape(b, HEAD_PAIRS, 2, HEAD_DIM, 2, HEAD_DIM)
    st = jnp.stack([t[:, :, 0, :, 0, :], t[:, :, 1, :, 1, :]], axis=2)
    return jnp.swapaxes(st.reshape(b, B_HEADS, HEAD_DIM, HEAD_DIM), -1, -2)


def _ab_out_kernel(x_ref, ao_ref, yb_ref, gt_ref, wa_ref, wb_ref, g_ref, o_ref):
    mix = _dot(ao_ref[...].astype(BF16), wa_ref[...])
    mix += _dot((yb_ref[...] * gt_ref[...]).astype(BF16), wb_ref[...])
    o_ref[...] = x_ref[...] + _rms(mix, g_ref[...])


def _ab_out(x, ao, yb, gt, w_out, norm_g, g_row, li, tm):
    m, d = x.shape
    return pl.pallas_call(
        _ab_out_kernel,
        grid=(m // tm,),
        in_specs=[
            pl.BlockSpec((tm, d), lambda i: (i, 0)),
            pl.BlockSpec((tm, A_Q), lambda i: (i, 0)),
            pl.BlockSpec((tm, B_WIDTH), lambda i: (i, 0)),
            pl.BlockSpec((tm, B_WIDTH), lambda i: (i, 0)),
            pl.BlockSpec((None, A_Q, d), lambda i: (li, 0, 0)),
            pl.BlockSpec((None, B_WIDTH, d), lambda i: (li, 1, 0)),
            _layer(norm_g, (g_row,)),
        ],
        out_specs=pl.BlockSpec((tm, d), lambda i: (i, 0)),
        out_shape=jax.ShapeDtypeStruct((m, d), F32),
        compiler_params=_params(("parallel",)),
        name="ab_out",
    )(x, ao, yb, gt, w_out, w_out, norm_g)


def _rope_tables(pos):
    half = HEAD_DIM // 2
    freqs = ROPE_THETA ** (-jnp.arange(half, dtype=F32) / half)
    ang = pos.astype(F32)[:, None] * freqs[None, :]
    cos = jnp.cos(ang)
    sin = jnp.sin(ang)
    return (jnp.tile(jnp.concatenate([cos, cos], axis=-1), (1, A_HEADS)),
            jnp.tile(jnp.concatenate([-sin, sin], axis=-1), (1, A_HEADS)))


def _trunk(x3, past, w):
    batch, seq, d = x3.shape
    m = batch * seq
    step = past is not None
    x = x3.reshape(m, d)
    t = _tiles(m)
    if not step:
        assert all(seq % n == 0 for n in (t["mix_m"], t["rwkv_m"], RWKV_CHUNK, WINDOW))
    assert m % t["ffn_m"] == 0
    cos, sin = _rope_tables((PAST_LEN if step else 0) + jnp.arange(seq, dtype=jnp.int32))
    ng = w["norm_g"]
    out_k, out_v, out_s, out_sh, out_c = [], [], [], [], []
    for l in range(w["depth"]):
        i = l // 2
        grow = lambda j: l * 6 + j
        x = _ffn(x, ng, (grow(0), grow(1)), w["ffn_w_gu"], w["ffn_w_down"], (l, 0), t["ffn_m"], t["ffn_f"])
        if l % 2 == 0:
            proj = _norm_mm(x, ng, grow(2), w["ab_w_in"], i, t["mix_m"], 1280)
            if step:
                ao, nk, nv = _swa_step(proj, w["attn_sinks"], i, past[0], past[1], cos, sin)
                ao = ao.reshape(m, A_Q)
                prev = jnp.pad(past[3][i], ((0, 0), (0, B_COLS_PAD - B_COLS)))
                s0 = _state_to_pairs(past[2][i])
            else:
                ao, nk, nv = _swa_seq(proj, w["attn_sinks"], i, cos, sin, batch, seq)
                prev = None
                s0 = jnp.zeros((batch, HEAD_PAIRS, LANE, LANE), F32)
            r, k, v, lw, a, gt = _rwkv_in(proj, prev, w["rwkv_in"], i, t["rwkv_m"], seq)
            if step:
                cs = RWKV_STEP_CHUNK
                chunked = lambda z: jnp.pad(z[:, None, :], ((0, 0), (0, cs - 1), (0, 0))).reshape(m * cs, B_WIDTH)
                yb, st = _rwkv_scan(chunked(r), chunked(k), chunked(v), chunked(lw), chunked(a), s0,
                                    w["rwkv_head"], i, batch, cs)
                yb = yb.reshape(m, cs, B_WIDTH)[:, 0]
            else:
                yb, st = _rwkv_scan(r, k, v, lw, a, s0, w["rwkv_head"], i, batch, RWKV_CHUNK)
            x = _ab_out(x, ao, yb, gt, w["ab_w_out"], ng, grow(3), i, t["mix_m"])
            out_k.append(nk.reshape(batch, WINDOW, A_KV_HEADS, HEAD_DIM))
            out_v.append(nv.reshape(batch, WINDOW, A_KV_HEADS, HEAD_DIM))
            out_s.append(_pairs_to_state(st))
            out_sh.append(proj.reshape(batch, seq, AB_COLS_PAD)[:, -1, A_COLS:AB_COLS])
        else:
            bg, u = _conv_in(x, ng, grow(2), w["conv_w_in"], i, t["mix_m"], 512)
            if step:
                buf = past[4][i]
                x = _conv_out_step(x, bg, u, buf[:, 1], buf[:, 0], w["conv_w"], w["conv_w_out"], ng, grow(3), i)
                out_c.append(jnp.stack([buf[:, 1], u], axis=1))
            else:
                x = _conv_out_seq(x, bg, u, w["conv_w"], w["conv_w_out"], ng, grow(3), i, t["mix_m"], seq)
                out_c.append(u.reshape(batch, seq, -1)[:, -(CONV_W - 1):])
        x = _ffn(x, ng, (grow(4), grow(5)), w["ffn_w_gu"], w["ffn_w_down"], (l, 1), t["ffn_m"], t["ffn_f"])
    return x.reshape(batch, seq, d), (jnp.stack(out_k), jnp.stack(out_v), jnp.stack(out_s),
                                      jnp.stack(out_sh), jnp.stack(out_c))


def _prep_weights(norm_g, ffn_w_gu, ffn_w_down, ab_w_in, ab_w_out, attn_sinks, rwkv_mu, rwkv_w0, rwkv_w_decay,
                  rwkv_a0, rwkv_w_aaa, rwkv_w_gate, rwkv_k_k, rwkv_k_a, rwkv_r_k, rwkv_gn_w, rwkv_gn_b,
                  conv_w_in, conv_w, conv_w_out):
    depth, n_norm, d = norm_g.shape
    padc = lambda t, n: jnp.pad(t, [(0, 0)] * (t.ndim - 1) + [(0, n - t.shape[-1])])
    row = lambda t: t[:, None, :]
    wdec = jnp.pad(rwkv_w_decay, ((0, 0), (0, D_AAA), (0, 0))).astype(BF16)
    waaa = jnp.pad(rwkv_w_aaa, ((0, 0), (D_DECAY, 0), (0, 0))).astype(BF16)
    wgate = jnp.pad(rwkv_w_gate, ((0, 0), (0, GATE_PAD - D_GATE), (0, 0))).astype(BF16)
    return {
        "depth": depth,
        "norm_g": norm_g.reshape(depth * n_norm, 1, d),
        "ffn_w_gu": ffn_w_gu,
        "ffn_w_down": ffn_w_down,
        "ab_w_in": padc(ab_w_in, AB_COLS_PAD).astype(BF16),
        "ab_w_out": ab_w_out.astype(BF16),
        "attn_sinks": attn_sinks,
        "rwkv_in": (row(padc(rwkv_mu, B_COLS_PAD)), row(rwkv_w0), wdec, row(rwkv_a0), waaa, wgate),
        "rwkv_head": (row(rwkv_k_k), row(rwkv_k_a), row(rwkv_r_k), row(rwkv_gn_w), row(rwkv_gn_b)),
        "conv_w_in": conv_w_in.astype(BF16),
        "conv_w": conv_w,
        "conv_w_out": conv_w_out.astype(BF16),
    }


def kernel(x_prompt, x_sample, cache_swa_k, cache_swa_v, state_rwkv, state_rwkv_shift, state_conv, norm_g, ffn_w_gu, ffn_w_down, ab_w_in, ab_w_out, attn_sinks, rwkv_mu, rwkv_w0, rwkv_w_decay, rwkv_a0, rwkv_w_aaa, rwkv_w_gate, rwkv_k_k, rwkv_k_a, rwkv_r_k, rwkv_gn_w, rwkv_gn_b, conv_w_in, conv_w, conv_w_out):
    w = _prep_weights(norm_g, ffn_w_gu, ffn_w_down, ab_w_in, ab_w_out, attn_sinks, rwkv_mu, rwkv_w0, rwkv_w_decay,
                      rwkv_a0, rwkv_w_aaa, rwkv_w_gate, rwkv_k_k, rwkv_k_a, rwkv_r_k, rwkv_gn_w, rwkv_gn_b,
                      conv_w_in, conv_w, conv_w_out)
    y_prompt, (p_k, p_v, p_s, p_sh, p_c) = _trunk(x_prompt, None, w)
    n_ab, dec_batch = cache_swa_k.shape[:2]
    past = (cache_swa_k.reshape(n_ab, dec_batch, WINDOW, A_KV), cache_swa_v.reshape(n_ab, dec_batch, WINDOW, A_KV),
            state_rwkv, state_rwkv_shift, state_conv)
    y_sample, (s_k, s_v, s_s, s_sh, s_c) = _trunk(x_sample, past, w)
    return (y_prompt, y_sample, p_k, p_v, p_s, p_sh, p_c, s_k, s_v, s_s, s_sh, s_c)
```

```python
import functools

import jax
import jax.numpy as jnp
from jax import lax
from jax.experimental import pallas as pl
from jax.experimental.pallas import tpu as pltpu

F32 = jnp.float32
BF16 = jnp.bfloat16

HEAD_DIM = 64
A_HEADS = 16
A_KV_HEADS = 4
A_GROUP = A_HEADS // A_KV_HEADS
WINDOW = 128
ROPE_THETA = 10000.0
PAST_LEN = 16384
A_Q = A_HEADS * HEAD_DIM
A_KV = A_KV_HEADS * HEAD_DIM
A_COLS = A_Q + 2 * A_KV
B_HEADS = 16
B_WIDTH = B_HEADS * HEAD_DIM
D_DECAY = 64
D_AAA = 64
D_GATE = 160
B_COLS = 3 * B_WIDTH + D_DECAY + D_AAA + D_GATE
GN_EPS = 64e-5
AB_COLS = A_COLS + B_COLS
CONV_W = 3
NORM_EPS = 1e-6

LANE = 128
HEAD_PAIRS = B_WIDTH // LANE
AB_COLS_PAD = 5120
B_COLS_PAD = AB_COLS_PAD - A_COLS
LORA_IN = D_DECAY + D_AAA
GATE_PAD = AB_COLS_PAD - A_COLS - 3 * B_WIDTH - LORA_IN
RWKV_CHUNK = 64
RWKV_INV_BLOCK = 16
VMEM_LIMIT = 56 * 1024 * 1024
FFN_VMEM_LIMIT = 60 * 1024 * 1024


def _tiles(m):
    return dict(ffn_m=min(1024, m), ffn_f=256 if m >= 1024 else 512, mix_m=min(512, m), rwkv_m=min(256, m))


def _params(sem, vmem_limit=VMEM_LIMIT):
    return pltpu.CompilerParams(dimension_semantics=sem, vmem_limit_bytes=vmem_limit)


def _layer(arr, idx):
    idx = tuple(idx)
    rest = arr.shape[len(idx):]
    return pl.BlockSpec((None,) * len(idx) + rest, lambda *_: idx + (0,) * len(rest))


def _rms(x, g):
    return x * lax.rsqrt(jnp.mean(x * x, axis=-1, keepdims=True) + NORM_EPS) * g


def _dot(a, b):
    return jnp.dot(a, b, preferred_element_type=F32)


def _ffn_kernel(x_ref, xs_ref, g0_ref, g1_ref, wg_ref, wu_ref, wd_ref, o_ref, os_ref, h_ref, hs_ref):
    i = pl.program_id(0)
    f = pl.program_id(1)
    g0 = g0_ref[...]
    g1 = g1_ref[...]
    cast = lambda w_ref: w_ref[...].astype(BF16)

    def group(x_ref, o_ref, h_ref):
        @pl.when(f == 0)
        def _():
            h_ref[...] = _rms(x_ref[...], g0).astype(BF16)
            o_ref[...] = jnp.zeros_like(o_ref)

        h = h_ref[...]
        gate = _dot(h, cast(wg_ref))
        act = (gate * jax.nn.sigmoid(gate) * _dot(h, cast(wu_ref))).astype(BF16)
        o_ref[...] += _dot(act, cast(wd_ref))

        @pl.when(f == pl.num_programs(1) - 1)
        def _():
            o_ref[...] = x_ref[...] + 0.5 * _rms(o_ref[...], g1)

    group(x_ref, o_ref, h_ref)

    @pl.when(i == 0)
    def _():
        group(xs_ref, os_ref, hs_ref)


def _ffn(x, xs, norm_g, g_rows, w_gu, w_down, lj, tm, tf):
    m, d = x.shape
    ms = xs.shape[0]
    l, j = lj
    nf = w_down.shape[2] // tf
    return pl.pallas_call(
        _ffn_kernel,
        grid=(m // tm, nf),
        in_specs=[
            pl.BlockSpec((tm, d), lambda i, f: (i, 0)),
            pl.BlockSpec((ms, d), lambda i, f: (0, 0)),
            _layer(norm_g, (g_rows[0],)),
            _layer(norm_g, (g_rows[1],)),
            pl.BlockSpec((None, None, d, tf), lambda i, f: (l, j, 0, f)),
            pl.BlockSpec((None, None, d, tf), lambda i, f: (l, j, 0, nf + f)),
            pl.BlockSpec((None, None, tf, d), lambda i, f: (l, j, f, 0)),
        ],
        out_specs=[pl.BlockSpec((tm, d), lambda i, f: (i, 0)),
                   pl.BlockSpec((ms, d), lambda i, f: (0, 0))],
        out_shape=[jax.ShapeDtypeStruct((m, d), F32), jax.ShapeDtypeStruct((ms, d), F32)],
        scratch_shapes=[pltpu.VMEM((tm, d), BF16), pltpu.VMEM((ms, d), BF16)],
        compiler_params=_params(("arbitrary", "arbitrary"), FFN_VMEM_LIMIT),
        name="ffn",
    )(x, xs, norm_g, norm_g, w_gu, w_gu, w_down)


def _norm_mm_kernel(x_ref, g_ref, w_ref, o_ref, h_ref):
    @pl.when(pl.program_id(1) == 0)
    def _():
        h_ref[...] = _rms(x_ref[...], g_ref[...]).astype(BF16)

    o_ref[...] = _dot(h_ref[...], w_ref[...])


def _norm_mm(x, norm_g, g_row, w, li, tm, tn):
    m, d = x.shape
    n = w.shape[-1]
    return pl.pallas_call(
        _norm_mm_kernel,
        grid=(m // tm, n // tn),
        in_specs=[
            pl.BlockSpec((tm, d), lambda i, j: (i, 0)),
            _layer(norm_g, (g_row,)),
            pl.BlockSpec((None, d, tn), lambda i, j: (li, 0, j)),
        ],
        out_specs=pl.BlockSpec((tm, tn), lambda i, j: (i, j)),
        out_shape=jax.ShapeDtypeStruct((m, n), F32),
        scratch_shapes=[pltpu.VMEM((tm, d), BF16)],
        compiler_params=_params(("parallel", "arbitrary")),
        name="ab_in",
    )(x, norm_g, w)


def _conv_in_kernel(x_ref, g_ref, wb_ref, wc_ref, wh_ref, bg_ref, u_ref, h_ref):
    @pl.when(pl.program_id(1) == 0)
    def _():
        h_ref[...] = _rms(x_ref[...], g_ref[...]).astype(BF16)

    h = h_ref[...]
    bg_ref[...] = _dot(h, wb_ref[...])
    u_ref[...] = _dot(h, wc_ref[...]) * _dot(h, wh_ref[...])


def _conv_in(x, norm_g, g_row, w_in, li, tm, tn):
    m, d = x.shape
    dc = w_in.shape[-1] // 3
    nj = dc // tn
    return pl.pallas_call(
        _conv_in_kernel,
        grid=(m // tm, nj),
        in_specs=[
            pl.BlockSpec((tm, d), lambda i, j: (i, 0)),
            _layer(norm_g, (g_row,)),
            pl.BlockSpec((None, d, tn), lambda i, j: (li, 0, j)),
            pl.BlockSpec((None, d, tn), lambda i, j: (li, 0, nj + j)),
            pl.BlockSpec((None, d, tn), lambda i, j: (li, 0, 2 * nj + j)),
        ],
        out_specs=[pl.BlockSpec((tm, tn), lambda i, j: (i, j)),
                   pl.BlockSpec((tm, tn), lambda i, j: (i, j))],
        out_shape=[jax.ShapeDtypeStruct((m, dc), F32), jax.ShapeDtypeStruct((m, dc), F32)],
        scratch_shapes=[pltpu.VMEM((tm, d), BF16)],
        compiler_params=_params(("parallel", "arbitrary")),
        name="conv_in",
    )(x, norm_g, w_in, w_in, w_in)


def _conv_taps(u, p1, p2, cw):
    return p2 * cw[0:1, :] + p1 * cw[1:2, :] + u * cw[2:3, :]


def _conv_out_seq_kernel(x_ref, bg_ref, u_ref, halo_ref, cw_ref, w_ref, g_ref, o_ref, *, tiles_per_seq):
    u = u_ref[...]
    first = (pl.program_id(0) % tiles_per_seq) == 0
    h1 = jnp.where(first, 0.0, halo_ref[7:8, :])
    h2 = jnp.where(first, 0.0, halo_ref[6:7, :])
    row = lax.broadcasted_iota(jnp.int32, u.shape, 0)
    p1 = jnp.where(row == 0, h1, pltpu.roll(u, 1, 0))
    p2 = jnp.where(row == 0, h2, jnp.where(row == 1, h1, pltpu.roll(u, 2, 0)))
    z = (bg_ref[...] * _conv_taps(u, p1, p2, cw_ref[...])).astype(BF16)
    o_ref[...] = x_ref[...] + _rms(_dot(z, w_ref[...]), g_ref[...])


def _conv_out_step_kernel(x_ref, bg_ref, u_ref, p1_ref, p2_ref, cw_ref, w_ref, g_ref, o_ref):
    z = (bg_ref[...] * _conv_taps(u_ref[...], p1_ref[...], p2_ref[...], cw_ref[...])).astype(BF16)
    o_ref[...] = x_ref[...] + _rms(_dot(z, w_ref[...]), g_ref[...])


def _conv_out_seq(x, bg, u, conv_w, w_out, norm_g, g_row, li, tm, seq):
    m, d = x.shape
    dc = u.shape[1]
    hb = tm // 8
    row_spec = pl.BlockSpec((tm, dc), lambda i: (i, 0))
    return pl.pallas_call(
        functools.partial(_conv_out_seq_kernel, tiles_per_seq=seq // tm),
        grid=(m // tm,),
        in_specs=[
            pl.BlockSpec((tm, d), lambda i: (i, 0)),
            row_spec, row_spec,
            pl.BlockSpec((8, dc), lambda i: (jnp.maximum(i * hb - 1, 0), 0)),
            _layer(conv_w, (li,)), _layer(w_out, (li,)), _layer(norm_g, (g_row,)),
        ],
        out_specs=pl.BlockSpec((tm, d), lambda i: (i, 0)),
        out_shape=jax.ShapeDtypeStruct((m, d), F32),
        compiler_params=_params(("parallel",)),
        name="conv_out",
    )(x, bg, u, u, conv_w, w_out, norm_g)


def _conv_out_step(x, bg, u, p1, p2, conv_w, w_out, norm_g, g_row, li):
    m, d = x.shape
    dc = u.shape[1]
    row_spec = pl.BlockSpec((m, dc), lambda i: (0, 0))
    return pl.pallas_call(
        _conv_out_step_kernel,
        grid=(1,),
        in_specs=[
            pl.BlockSpec((m, d), lambda i: (0, 0)),
            row_spec, row_spec, row_spec, row_spec,
            _layer(conv_w, (li,)), _layer(w_out, (li,)), _layer(norm_g, (g_row,)),
        ],
        out_specs=pl.BlockSpec((m, d), lambda i: (0, 0)),
        out_shape=jax.ShapeDtypeStruct((m, d), F32),
        compiler_params=_params(("arbitrary",)),
        name="conv_out_step",
    )(x, bg, u, p1, p2, conv_w, w_out, norm_g)


def _rope(x, cos, sin):
    width = x.shape[-1]
    half = HEAD_DIM // 2
    lane = lax.broadcasted_iota(jnp.int32, x.shape, x.ndim - 1)
    swapped = jnp.where((lane % HEAD_DIM) < half,
                        pltpu.roll(x, width - half, x.ndim - 1),
                        pltpu.roll(x, half, x.ndim - 1))
    return x * cos + swapped * sin


def _swa_seq_kernel(sink_ref, q_ref, kc_ref, kp_ref, vc_ref, vp_ref, cq_ref, sq_ref, cp_ref, sp_ref,
                    o_ref, nk_ref, nv_ref, *, li):
    n = pl.program_id(1)
    q = _rope(q_ref[...], cq_ref[...], sq_ref[...])
    kc = _rope(kc_ref[...], cq_ref[:, :A_KV], sq_ref[:, :A_KV])
    kp = _rope(kp_ref[...], cp_ref[...], sp_ref[...])
    kw = jnp.concatenate([kp, kc], axis=0).astype(BF16)
    vw = jnp.concatenate([vp_ref[...], vc_ref[...]], axis=0).astype(BF16)
    qi = lax.broadcasted_iota(jnp.int32, (WINDOW, 2 * WINDOW), 0)
    kj = lax.broadcasted_iota(jnp.int32, (WINDOW, 2 * WINDOW), 1)
    diff = WINDOW + qi - kj
    lo = jnp.where(n > 0, 0, WINDOW)
    valid = (diff >= 0) & (diff <= WINDOW) & (kj >= lo)
    outs = []
    for h in range(A_HEADS):
        kv = h // A_GROUP
        qh = q[:, h * HEAD_DIM:(h + 1) * HEAD_DIM].astype(BF16)
        kh = kw[:, kv * HEAD_DIM:(kv + 1) * HEAD_DIM]
        vh = vw[:, kv * HEAD_DIM:(kv + 1) * HEAD_DIM]
        s = lax.dot_general(qh, kh, (((1,), (1,)), ((), ())), preferred_element_type=F32)
        s = jnp.where(valid, s * (HEAD_DIM ** -0.5), -jnp.inf)
        sink = sink_ref[li, h]
        mx = jnp.maximum(jnp.max(s, axis=-1, keepdims=True), sink)
        e = jnp.exp(s - mx)
        p = e / (jnp.sum(e, axis=-1, keepdims=True) + jnp.exp(sink - mx))
        outs.append(_dot(p.astype(BF16), vh))
    o_ref[...] = jnp.concatenate(outs, axis=-1)

    @pl.when(n == pl.num_programs(1) - 1)
    def _():
        nk_ref[...] = kc
        nv_ref[...] = vc_ref[...]


def _swa_seq(proj, sinks, li, cos, sin, batch, seq):
    nb = seq // WINDOW
    kcol = A_Q // A_KV
    vcol = kcol + 1
    cur = lambda b, n: b * nb + n
    prev = lambda b, n: b * nb + jnp.maximum(n - 1, 0)
    return pl.pallas_call(
        functools.partial(_swa_seq_kernel, li=li),
        grid=(batch, nb),
        in_specs=[
            pl.BlockSpec(memory_space=pltpu.SMEM),
            pl.BlockSpec((WINDOW, A_Q), lambda b, n: (cur(b, n), 0)),
            pl.BlockSpec((WINDOW, A_KV), lambda b, n: (cur(b, n), kcol)),
            pl.BlockSpec((WINDOW, A_KV), lambda b, n: (prev(b, n), kcol)),
            pl.BlockSpec((WINDOW, A_KV), lambda b, n: (cur(b, n), vcol)),
            pl.BlockSpec((WINDOW, A_KV), lambda b, n: (prev(b, n), vcol)),
            pl.BlockSpec((WINDOW, A_Q), lambda b, n: (n, 0)),
            pl.BlockSpec((WINDOW, A_Q), lambda b, n: (n, 0)),
            pl.BlockSpec((WINDOW, A_KV), lambda b, n: (jnp.maximum(n - 1, 0), 0)),
            pl.BlockSpec((WINDOW, A_KV), lambda b, n: (jnp.maximum(n - 1, 0), 0)),
        ],
        out_specs=[
            pl.BlockSpec((WINDOW, A_Q), lambda b, n: (cur(b, n), 0)),
            pl.BlockSpec((WINDOW, A_KV), lambda b, n: (b, 0)),
            pl.BlockSpec((WINDOW, A_KV), lambda b, n: (b, 0)),
        ],
        out_shape=[
            jax.ShapeDtypeStruct((batch * seq, A_Q), F32),
            jax.ShapeDtypeStruct((batch * WINDOW, A_KV), F32),
            jax.ShapeDtypeStruct((batch * WINDOW, A_KV), F32),
        ],
        compiler_params=_params(("parallel", "arbitrary")),
        name="swa_seq",
    )(sinks, proj, proj, proj, proj, proj, cos, sin, cos, sin)


def _swa_step_kernel(sink_ref, p_ref, ck_ref, cv_ref, cos_ref, sin_ref, o_ref, nk_ref, nv_ref, *, li):
    q = _rope(p_ref[0, :, 0:A_Q], cos_ref[...], sin_ref[...])
    kn = _rope(p_ref[0, :, A_Q:A_Q + A_KV], cos_ref[:, :A_KV], sin_ref[:, :A_KV])
    vn = p_ref[0, :, A_Q + A_KV:A_COLS]
    ck = ck_ref[0]
    cv = cv_ref[0]
    outs = []
    for h in range(A_HEADS):
        kv = h // A_GROUP
        hs = slice(h * HEAD_DIM, (h + 1) * HEAD_DIM)
        ks = slice(kv * HEAD_DIM, (kv + 1) * HEAD_DIM)
        qh = q[:, hs] * (HEAD_DIM ** -0.5)
        s = jnp.sum(ck[:, ks] * qh, axis=-1, keepdims=True)
        s_new = jnp.sum(kn[:, ks] * qh, axis=-1, keepdims=True)
        sink = sink_ref[li, h]
        mx = jnp.maximum(jnp.maximum(jnp.max(s, axis=0, keepdims=True), s_new), sink)
        e = jnp.exp(s - mx)
        e_new = jnp.exp(s_new - mx)
        den = jnp.sum(e, axis=0, keepdims=True) + e_new + jnp.exp(sink - mx)
        outs.append((jnp.sum(e * cv[:, ks], axis=0, keepdims=True) + e_new * vn[:, ks]) / den)
    o_ref[0] = jnp.concatenate(outs, axis=-1)
    nk_ref[0, 0:WINDOW - 1, :] = ck_ref[0, 1:WINDOW, :]
    nk_ref[0, WINDOW - 1:WINDOW, :] = kn
    nv_ref[0, 0:WINDOW - 1, :] = cv_ref[0, 1:WINDOW, :]
    nv_ref[0, WINDOW - 1:WINDOW, :] = vn


def _swa_step(proj, sinks, li, cache_k, cache_v, cos, sin):
    batch, width = proj.shape
    cache_in = pl.BlockSpec((None, 1, WINDOW, A_KV), lambda b: (li, b, 0, 0))
    cache_out = pl.BlockSpec((1, WINDOW, A_KV), lambda b: (b, 0, 0))
    return pl.pallas_call(
        functools.partial(_swa_step_kernel, li=li),
        grid=(batch,),
        in_specs=[
            pl.BlockSpec(memory_space=pltpu.SMEM),
            pl.BlockSpec((1, 1, width), lambda b: (b, 0, 0)),
            cache_in, cache_in,
            pl.BlockSpec((1, A_Q), lambda b: (0, 0)),
            pl.BlockSpec((1, A_Q), lambda b: (0, 0)),
        ],
        out_specs=[pl.BlockSpec((1, 1, A_Q), lambda b: (b, 0, 0)), cache_out, cache_out],
        out_shape=[
            jax.ShapeDtypeStruct((batch, 1, A_Q), F32),
            jax.ShapeDtypeStruct((batch, WINDOW, A_KV), F32),
            jax.ShapeDtypeStruct((batch, WINDOW, A_KV), F32),
        ],
        compiler_params=_params(("parallel",)),
        name="swa_step",
    )(sinks, proj.reshape(batch, 1, width), cache_k, cache_v, cos, sin)


def _softplus(z):
    return jnp.maximum(z, 0.0) + jnp.log(1.0 + jnp.exp(-jnp.abs(z)))


def _rwkv_in_math(pb, prev, mu, w0, wdec, a0, waaa, wgate, r_ref, k_ref, v_ref, lw_ref, a_ref, g_ref):
    xm = pb + (prev - pb) * mu
    r_ref[...] = xm[:, 0:B_WIDTH]
    k_ref[...] = xm[:, B_WIDTH:2 * B_WIDTH]
    v_ref[...] = xm[:, 2 * B_WIDTH:3 * B_WIDTH]
    lora = xm[:, 3 * B_WIDTH:3 * B_WIDTH + LORA_IN]
    dg = xm[:, 3 * B_WIDTH + LORA_IN:]
    w_log = -_softplus(-(w0 + _dot(jnp.tanh(lora).astype(BF16), wdec))) - 0.5
    lw_ref[...] = -jnp.exp(w_log)
    a_ref[...] = jax.nn.sigmoid(a0 + _dot(lora.astype(BF16), waaa))
    g_ref[...] = _dot(jax.nn.sigmoid(dg).astype(BF16), wgate)


def _rwkv_in_seq_kernel(p_ref, halo_ref, mu_ref, w0_ref, wdec_ref, a0_ref, waaa_ref, wgate_ref,
                        r_ref, k_ref, v_ref, lw_ref, a_ref, g_ref, *, tiles_per_seq):
    pb = p_ref[:, A_COLS:]
    first = (pl.program_id(0) % tiles_per_seq) == 0
    hrow = jnp.where(first, 0.0, halo_ref[7:8, A_COLS:])
    row = lax.broadcasted_iota(jnp.int32, pb.shape, 0)
    prev = jnp.where(row == 0, hrow, pltpu.roll(pb, 1, 0))
    _rwkv_in_math(pb, prev, mu_ref[...], w0_ref[...], wdec_ref[...], a0_ref[...], waaa_ref[...],
                  wgate_ref[...], r_ref, k_ref, v_ref, lw_ref, a_ref, g_ref)


def _rwkv_in_step_kernel(p_ref, prev_ref, mu_ref, w0_ref, wdec_ref, a0_ref, waaa_ref, wgate_ref,
                         r_ref, k_ref, v_ref, lw_ref, a_ref, g_ref):
    _rwkv_in_math(p_ref[:, A_COLS:], prev_ref[...], mu_ref[...], w0_ref[...], wdec_ref[...], a0_ref[...],
                  waaa_ref[...], wgate_ref[...], r_ref, k_ref, v_ref, lw_ref, a_ref, g_ref)


def _rwkv_in(proj, prev, wts, li, tt, seq):
    m, width = proj.shape
    out_spec = pl.BlockSpec((tt, B_WIDTH), lambda i: (i, 0))
    if prev is None:
        kern = functools.partial(_rwkv_in_seq_kernel, tiles_per_seq=seq // tt)
        second = proj
        second_spec = pl.BlockSpec((8, width), lambda i: (jnp.maximum(i * (tt // 8) - 1, 0), 0))
    else:
        kern = _rwkv_in_step_kernel
        second = prev
        second_spec = pl.BlockSpec((tt, B_COLS_PAD), lambda i: (i, 0))
    return pl.pallas_call(
        kern,
        grid=(m // tt,),
        in_specs=[pl.BlockSpec((tt, width), lambda i: (i, 0)), second_spec] + [_layer(a, (li,)) for a in wts],
        out_specs=[out_spec] * 6,
        out_shape=[jax.ShapeDtypeStruct((m, B_WIDTH), F32)] * 6,
        compiler_params=_params(("parallel",)),
        name="rwkv_in",
    )(proj, second, *wts)


def _bdg(a, b, dn):
    return lax.dot_general(a.astype(BF16), b.astype(BF16), dn, preferred_element_type=F32)


def _bmm(a, b):
    return _bdg(a, b, (((2,), (1,)), ((0,), (0,))))


def _bmm_nt(a, b):
    return _bdg(a, b, (((2,), (2,)), ((0,), (0,))))


def _bmm_tn(a, b):
    return _bdg(a, b, (((1,), (1,)), ((0,), (0,))))


def _unit_lower_inverse(a, eye, same_block):
    d = jnp.where(same_block, a, 0.0)
    e = a - d
    x = eye + d
    dp = d
    p = 1
    while 2 * p < RWKV_INV_BLOCK:
        dp = _bmm(dp, dp)
        x = _bmm(x, eye + dp)
        p *= 2
    nn = _bmm(x, e)
    return _bmm(_bmm(eye + nn, eye + _bmm(nn, nn)), x)


def _head_sums(x, first_half):
    outs = []
    for p in range(HEAD_PAIRS):
        xp = x[:, p * LANE:(p + 1) * LANE]
        s0 = jnp.sum(jnp.where(first_half, xp, 0.0), axis=-1, keepdims=True)
        s1 = jnp.sum(jnp.where(first_half, 0.0, xp), axis=-1, keepdims=True)
        outs.append(jnp.where(first_half, s0, s1))
    return jnp.concatenate(outs, axis=-1)


def _rwkv_scan_kernel(r_ref, k_ref, v_ref, lw_ref, a_ref, kk_ref, ka_ref, rk_ref, gw_ref, gb_ref,
                      y_ref, s_out_ref, st_ref):
    c = pl.program_id(1)

    @pl.when(c == 0)
    def _():
        st_ref[...] = jnp.zeros_like(st_ref)

    r = r_ref[...]
    k = k_ref[...]
    v = v_ref[...]
    lw = lw_ref[...]
    a = a_ref[...]
    cs = r.shape[0]
    c2 = 2 * cs
    iota = lambda shape, dim: lax.broadcasted_iota(jnp.int32, shape, dim)

    tril = jnp.where(iota((cs, cs), 1) <= iota((cs, cs), 0), 1.0, 0.0).astype(BF16)
    lw_hi = lw.astype(BF16)
    rem = lw - lw_hi.astype(F32)
    lw_mid = rem.astype(BF16)
    lw_lo = (rem - lw_mid.astype(F32)).astype(BF16)
    lcum = _dot(tril, lw_hi) + (_dot(tril, lw_mid) + _dot(tril, lw_lo))
    p_t = jnp.exp(lcum)
    p_inv = jnp.exp(-lcum)
    p_prev = jnp.exp(lcum - lw)
    p_end = jnp.exp(lcum[cs - 1:cs, :])

    first_half = iota((cs, LANE), 1) < HEAD_DIM
    kk = k * kk_ref[...]
    kk = kk / jnp.maximum(jnp.sqrt(_head_sums(kk * kk, first_half)), 1e-12)
    k2 = k * (1.0 + (a - 1.0) * ka_ref[...])
    al = -kk * p_prev
    be = kk * a * p_inv
    kt = k2 * p_inv
    rt = r * p_t
    be_end = be * p_end
    kt_end = kt * p_end
    bonus = _head_sums(r * k2 * rk_ref[...], first_half) * v

    np_, one = HEAD_PAIRS, (1,)
    own_half = (iota(one + (c2, LANE), 1) // cs) == (iota(one + (c2, LANE), 2) // HEAD_DIM)
    t2 = iota(one + (c2, c2), 1)
    s2 = iota(one + (c2, c2), 2)
    same_head = (t2 // cs) == (s2 // cs)
    strict = same_head & (s2 < t2)
    lower = same_head & (s2 <= t2)
    same_block = (t2 // RWKV_INV_BLOCK) == (s2 // RWKV_INV_BLOCK)
    eye = jnp.where(t2 == s2, 1.0, 0.0)
    kl = iota(one + (LANE, LANE), 1)
    vl = iota(one + (LANE, LANE), 2)
    st_diag = kl == vl
    st_blocks = (kl // HEAD_DIM) == (vl // HEAD_DIM)

    pairs = lambda x: jnp.stack([x[:, p * LANE:(p + 1) * LANE] for p in range(np_)], axis=0)
    dup = lambda x: jnp.concatenate([x, x], axis=1)
    own = lambda x: jnp.where(own_half, dup(pairs(x)), 0.0).astype(BF16)
    al2 = own(al)
    rt2 = own(rt)
    v2 = own(v)
    prod = _bmm_nt(jnp.concatenate([al2, rt2], axis=1),
                   jnp.concatenate([dup(pairs(be)), dup(pairs(kt))], axis=1))
    a_ab = jnp.where(strict, prod[:, :c2, :c2], 0.0)
    a_ak = jnp.where(strict, prod[:, :c2, c2:], 0.0)
    m_rb = jnp.where(lower, prod[:, c2:, :c2], 0.0)
    m_rk = jnp.where(lower, prod[:, c2:, c2:], 0.0)
    tinv = _unit_lower_inverse(a_ab, eye, same_block)
    st = st_ref[...]
    u2 = _bmm(tinv, _bmm(al2, st) + _bmm(a_ak, v2))
    y2 = _bmm(rt2, st) + _bmm(m_rb, u2) + _bmm(m_rk, v2)
    y3 = y2[:, :cs] + y2[:, cs:]
    u = u2[:, :cs] + u2[:, cs:]
    p_col = jnp.sum(jnp.where(st_diag, pairs(p_end), 0.0), axis=2, keepdims=True)
    inc = _bmm_tn(pairs(be_end), u) + _bmm_tn(pairs(kt_end), pairs(v))
    st_ref[...] = st * p_col + jnp.where(st_blocks, inc, 0.0)

    y = jnp.concatenate([y3[p] for p in range(np_)], axis=-1)
    mean = _head_sums(y, first_half) * (1.0 / HEAD_DIM)
    yc = y - mean
    var = _head_sums(yc * yc, first_half) * (1.0 / HEAD_DIM)
    y_ref[...] = yc * lax.rsqrt(var + GN_EPS) * gw_ref[...] + gb_ref[...] + bonus

    @pl.when(c == pl.num_programs(1) - 1)
    def _():
        for p in range(np_):
            sp = st_ref[p].T
            s_out_ref[0, 2 * p] = sp[:HEAD_DIM, :HEAD_DIM]
            s_out_ref[0, 2 * p + 1] = sp[HEAD_DIM:, HEAD_DIM:]


def _rwkv_scan(r, k, v, lw, a, head_wts, li, batch, cs):
    m = r.shape[0]
    nc = m // batch // cs
    seq_spec = pl.BlockSpec((cs, B_WIDTH), lambda b, c: (b * nc + c, 0))
    st_spec = pl.BlockSpec((1, B_HEADS, HEAD_DIM, HEAD_DIM), lambda b, c: (b, 0, 0, 0))
    return pl.pallas_call(
        _rwkv_scan_kernel,
        grid=(batch, nc),
        in_specs=[seq_spec] * 5 + [_layer(t, (li,)) for t in head_wts],
        out_specs=[seq_spec, st_spec],
        out_shape=[jax.ShapeDtypeStruct((m, B_WIDTH), F32),
                   jax.ShapeDtypeStruct((batch, B_HEADS, HEAD_DIM, HEAD_DIM), F32)],
        scratch_shapes=[pltpu.VMEM((HEAD_PAIRS, LANE, LANE), F32)],
        compiler_params=_params(("parallel", "arbitrary")),
        name="rwkv_scan",
    )(r, k, v, lw, a, *head_wts)


def _rwkv_step_kernel(r_ref, k_ref, v_ref, lw_ref, a_ref, s_ref, kk_ref, ka_ref, rk_ref, gw_ref, gb_ref,
                      y_ref, s_out_ref):
    bb = r_ref.shape[0]
    n = bb * B_HEADS

    def heads(ref):
        x = ref[...]
        return jnp.stack([x[b:b + 1, h * HEAD_DIM:(h + 1) * HEAD_DIM]
                          for b in range(bb) for h in range(B_HEADS)], axis=0)

    per_seq = lambda ref: jnp.concatenate([ref[...]] * bb, axis=0)
    r, k, v, lw, a = heads(r_ref), heads(k_ref), heads(v_ref), heads(lw_ref), heads(a_ref)
    s = s_ref[...].reshape(n, HEAD_DIM, HEAD_DIM)
    kk = k * per_seq(kk_ref)
    kk = kk / jnp.maximum(jnp.sqrt(jnp.sum(kk * kk, axis=-1, keepdims=True)), 1e-12)
    k2 = k * (1.0 + (a - 1.0) * per_seq(ka_ref))
    eye = (lax.broadcasted_iota(jnp.int32, (1, HEAD_DIM, HEAD_DIM), 1)
           == lax.broadcasted_iota(jnp.int32, (1, HEAD_DIM, HEAD_DIM), 2))
    sa = jnp.sum(s * (-kk), axis=-1, keepdims=True)
    v_col = jnp.sum(jnp.where(eye, v, 0.0), axis=-1, keepdims=True)
    s_new = s * jnp.exp(lw) + sa * (kk * a) + v_col * k2
    y_col = jnp.sum(s_new * r, axis=-1, keepdims=True)
    y = jnp.sum(jnp.where(eye, y_col, 0.0), axis=1, keepdims=True)
    mean = jnp.mean(y, axis=-1, keepdims=True)
    yc = y - mean
    var = jnp.mean(yc * yc, axis=-1, keepdims=True)
    out = (yc * lax.rsqrt(var + GN_EPS) * per_seq(gw_ref) + per_seq(gb_ref)
           + jnp.sum(r * k2 * per_seq(rk_ref), axis=-1, keepdims=True) * v)
    s_out_ref[...] = s_new.reshape(bb, B_HEADS, HEAD_DIM, HEAD_DIM)
    y_ref[...] = jnp.concatenate(
        [jnp.concatenate([out[b * B_HEADS + h] for h in range(B_HEADS)], axis=-1) for b in range(bb)], axis=0)


def _rwkv_step(r, k, v, lw, a, state, head_wts, li, bb):
    batch = r.shape[0]
    row_spec = pl.BlockSpec((bb, B_WIDTH), lambda b: (b, 0))
    st_shape = (bb, B_HEADS, HEAD_DIM, HEAD_DIM)
    return pl.pallas_call(
        _rwkv_step_kernel,
        grid=(batch // bb,),
        in_specs=[row_spec] * 5 + [pl.BlockSpec((None,) + st_shape, lambda b: (li, b, 0, 0, 0))]
        + [_layer(t, (li,)) for t in head_wts],
        out_specs=[row_spec, pl.BlockSpec(st_shape, lambda b: (b, 0, 0, 0))],
        out_shape=[jax.ShapeDtypeStruct((batch, B_WIDTH), F32),
                   jax.ShapeDtypeStruct((batch, B_HEADS, HEAD_DIM, HEAD_DIM), F32)],
        compiler_params=_params(("parallel",)),
        name="rwkv_step",
    )(r, k, v, lw, a, state, *head_wts)


def _ab_out_kernel(x_ref, ao_ref, yb_ref, gt_ref, wa_ref, wb_ref, g_ref, o_ref):
    mix = _dot(ao_ref[...].astype(BF16), wa_ref[...])
    mix += _dot((yb_ref[...] * gt_ref[...]).astype(BF16), wb_ref[...])
    o_ref[...] = x_ref[...] + _rms(mix, g_ref[...])


def _ab_out(x, ao, yb, gt, w_out, norm_g, g_row, li, tm):
    m, d = x.shape
    return pl.pallas_call(
        _ab_out_kernel,
        grid=(m // tm,),
        in_specs=[
            pl.BlockSpec((tm, d), lambda i: (i, 0)),
            pl.BlockSpec((tm, A_Q), lambda i: (i, 0)),
            pl.BlockSpec((tm, B_WIDTH), lambda i: (i, 0)),
            pl.BlockSpec((tm, B_WIDTH), lambda i: (i, 0)),
            pl.BlockSpec((None, A_Q, d), lambda i: (li, 0, 0)),
            pl.BlockSpec((None, B_WIDTH, d), lambda i: (li, 1, 0)),
            _layer(norm_g, (g_row,)),
        ],
        out_specs=pl.BlockSpec((tm, d), lambda i: (i, 0)),
        out_shape=jax.ShapeDtypeStruct((m, d), F32),
        compiler_params=_params(("parallel",)),
        name="ab_out",
    )(x, ao, yb, gt, w_out, w_out, norm_g)


def _rope_tables(pos):
    half = HEAD_DIM // 2
    freqs = ROPE_THETA ** (-jnp.arange(half, dtype=F32) / half)
    ang = pos.astype(F32)[:, None] * freqs[None, :]
    cos = jnp.cos(ang)
    sin = jnp.sin(ang)
    return (jnp.tile(jnp.concatenate([cos, cos], axis=-1), (1, A_HEADS)),
            jnp.tile(jnp.concatenate([-sin, sin], axis=-1), (1, A_HEADS)))


def _trunk(xp3, xs3, past, w):
    bp, seq, d = xp3.shape
    bs = xs3.shape[0]
    mp = bp * seq
    xp = xp3.reshape(mp, d)
    xs = xs3.reshape(bs, d)
    tp = _tiles(mp)
    ts = _tiles(bs)
    assert all(seq % n == 0 for n in (tp["mix_m"], tp["rwkv_m"], RWKV_CHUNK, WINDOW)) and mp % tp["ffn_m"] == 0
    cos_p, sin_p = _rope_tables(jnp.arange(seq, dtype=jnp.int32))
    cos_s, sin_s = _rope_tables(PAST_LEN + jnp.arange(1, dtype=jnp.int32))
    cache_k, cache_v, state, shift, conv_buf = past
    ng = w["norm_g"]
    outs = {key: [] for key in ("pk", "pv", "ps", "psh", "pc", "sk", "sv", "ss", "ssh", "sc")}
    kv_heads = lambda t, b: t.reshape(b, WINDOW, A_KV_HEADS, HEAD_DIM)
    for l in range(w["depth"]):
        i = l // 2
        grow = lambda j: l * 6 + j
        xp, xs = _ffn(xp, xs, ng, (grow(0), grow(1)), w["ffn_w_gu"], w["ffn_w_down"], (l, 0),
                      tp["ffn_m"], tp["ffn_f"])
        if l % 2 == 0:
            proj = _norm_mm(xp, ng, grow(2), w["ab_w_in"], i, tp["mix_m"], 1280)
            ao, nk, nv = _swa_seq(proj, w["attn_sinks"], i, cos_p, sin_p, bp, seq)
            r, k, v, lw, a, gt = _rwkv_in(proj, None, w["rwkv_in"], i, tp["rwkv_m"], seq)
            yb, st = _rwkv_scan(r, k, v, lw, a, w["rwkv_rows"], i, bp, RWKV_CHUNK)
            xp = _ab_out(xp, ao, yb, gt, w["ab_w_out"], ng, grow(3), i, tp["mix_m"])
            outs["pk"].append(kv_heads(nk, bp))
            outs["pv"].append(kv_heads(nv, bp))
            outs["ps"].append(st)
            outs["psh"].append(proj.reshape(bp, seq, AB_COLS_PAD)[:, -1, A_COLS:AB_COLS])

            proj = _norm_mm(xs, ng, grow(2), w["ab_w_in"], i, ts["mix_m"], 1280)
            ao, nk, nv = _swa_step(proj, w["attn_sinks"], i, cache_k, cache_v, cos_s, sin_s)
            prev = jnp.pad(shift[i], ((0, 0), (0, B_COLS_PAD - B_COLS)))
            r, k, v, lw, a, gt = _rwkv_in(proj, prev, w["rwkv_in"], i, ts["rwkv_m"], 1)
            yb, st = _rwkv_step(r, k, v, lw, a, state, w["rwkv_heads"], i, min(8, bs))
            xs = _ab_out(xs, ao.reshape(bs, A_Q), yb, gt, w["ab_w_out"], ng, grow(3), i, ts["mix_m"])
            outs["sk"].append(kv_heads(nk, bs))
            outs["sv"].append(kv_heads(nv, bs))
            outs["ss"].append(st)
            outs["ssh"].append(proj[:, A_COLS:AB_COLS])
        else:
            bg, u = _conv_in(xp, ng, grow(2), w["conv_w_in"], i, tp["mix_m"], 512)
            xp = _conv_out_seq(xp, bg, u, w["conv_w"], w["conv_w_out"], ng, grow(3), i, tp["mix_m"], seq)
            outs["pc"].append(u.reshape(bp, seq, -1)[:, -(CONV_W - 1):])

            bg, u = _conv_in(xs, ng, grow(2), w["conv_w_in"], i, ts["mix_m"], 512)
            buf = conv_buf[i]
            xs = _conv_out_step(xs, bg, u, buf[:, 1], buf[:, 0], w["conv_w"], w["conv_w_out"], ng, grow(3), i)
            outs["sc"].append(jnp.stack([buf[:, 1], u], axis=1))
        xp, xs = _ffn(xp, xs, ng, (grow(4), grow(5)), w["ffn_w_gu"], w["ffn_w_down"], (l, 1),
                      tp["ffn_m"], tp["ffn_f"])
    stacked = {key: jnp.stack(val) for key, val in outs.items()}
    return xp.reshape(bp, seq, d), xs.reshape(bs, 1, d), stacked


def _prep_weights(norm_g, ffn_w_gu, ffn_w_down, ab_w_in, ab_w_out, attn_sinks, rwkv_mu, rwkv_w0, rwkv_w_decay,
                  rwkv_a0, rwkv_w_aaa, rwkv_w_gate, rwkv_k_k, rwkv_k_a, rwkv_r_k, rwkv_gn_w, rwkv_gn_b,
                  conv_w_in, conv_w, conv_w_out):
    depth, n_norm, d = norm_g.shape
    padc = lambda t, n: jnp.pad(t, [(0, 0)] * (t.ndim - 1) + [(0, n - t.shape[-1])])
    row = lambda t: t[:, None, :]
    wdec = jnp.pad(rwkv_w_decay, ((0, 0), (0, D_AAA), (0, 0))).astype(BF16)
    waaa = jnp.pad(rwkv_w_aaa, ((0, 0), (D_DECAY, 0), (0, 0))).astype(BF16)
    wgate = jnp.pad(rwkv_w_gate, ((0, 0), (0, GATE_PAD - D_GATE), (0, 0))).astype(BF16)
    return {
        "depth": depth,
        "norm_g": norm_g.reshape(depth * n_norm, 1, d),
        "ffn_w_gu": ffn_w_gu,
        "ffn_w_down": ffn_w_down,
        "ab_w_in": padc(ab_w_in, AB_COLS_PAD).astype(BF16),
        "ab_w_out": ab_w_out.astype(BF16),
        "attn_sinks": attn_sinks,
        "rwkv_in": (row(padc(rwkv_mu, B_COLS_PAD)), row(rwkv_w0), wdec, row(rwkv_a0), waaa, wgate),
        "rwkv_rows": tuple(row(t) for t in (rwkv_k_k, rwkv_k_a, rwkv_r_k, rwkv_gn_w, rwkv_gn_b)),
        "rwkv_heads": tuple(t.reshape(-1, B_HEADS, 1, HEAD_DIM)
                            for t in (rwkv_k_k, rwkv_k_a, rwkv_r_k, rwkv_gn_w, rwkv_gn_b)),
        "conv_w_in": conv_w_in.astype(BF16),
        "conv_w": conv_w,
        "conv_w_out": conv_w_out.astype(BF16),
    }


def kernel(x_prompt, x_sample, cache_swa_k, cache_swa_v, state_rwkv, state_rwkv_shift, state_conv, norm_g, ffn_w_gu, ffn_w_down, ab_w_in, ab_w_out, attn_sinks, rwkv_mu, rwkv_w0, rwkv_w_decay, rwkv_a0, rwkv_w_aaa, rwkv_w_gate, rwkv_k_k, rwkv_k_a, rwkv_r_k, rwkv_gn_w, rwkv_gn_b, conv_w_in, conv_w, conv_w_out):
    w = _prep_weights(norm_g, ffn_w_gu, ffn_w_down, ab_w_in, ab_w_out, attn_sinks, rwkv_mu, rwkv_w0, rwkv_w_decay,
                      rwkv_a0, rwkv_w_aaa, rwkv_w_gate, rwkv_k_k, rwkv_k_a, rwkv_r_k, rwkv_gn_w, rwkv_gn_b,
                      conv_w_in, conv_w, conv_w_out)
    n_ab, dec_batch = cache_swa_k.shape[:2]
    past = (cache_swa_k.reshape(n_ab, dec_batch, WINDOW, A_KV), cache_swa_v.reshape(n_ab, dec_batch, WINDOW, A_KV),
            state_rwkv, state_rwkv_shift, state_conv)
    y_prompt, y_sample, o = _trunk(x_prompt, x_sample, past, w)
    return (y_prompt, y_sample, o["pk"], o["pv"], o["ps"], o["psh"], o["pc"],
            o["sk"], o["sv"], o["ss"], o["ssh"], o["sc"])
```

```python
import functools

import jax
import jax.numpy as jnp
from jax import lax
from jax.experimental import pallas as pl
from jax.experimental.pallas import tpu as pltpu

F32 = jnp.float32
BF16 = jnp.bfloat16

HEAD_DIM = 64
A_HEADS = 16
A_KV_HEADS = 4
A_GROUP = A_HEADS // A_KV_HEADS
WINDOW = 128
ROPE_THETA = 10000.0
PAST_LEN = 16384
A_Q = A_HEADS * HEAD_DIM
A_KV = A_KV_HEADS * HEAD_DIM
A_COLS = A_Q + 2 * A_KV
B_HEADS = 16
B_WIDTH = B_HEADS * HEAD_DIM
D_DECAY = 64
D_AAA = 64
D_GATE = 160
B_COLS = 3 * B_WIDTH + D_DECAY + D_AAA + D_GATE
GN_EPS = 64e-5
AB_COLS = A_COLS + B_COLS
CONV_W = 3
NORM_EPS = 1e-6

LANE = 128
HEAD_PAIRS = B_WIDTH // LANE
AB_COLS_PAD = 5120
B_COLS_PAD = AB_COLS_PAD - A_COLS
LORA_IN = D_DECAY + D_AAA
GATE_PAD = AB_COLS_PAD - A_COLS - 3 * B_WIDTH - LORA_IN
NORM_ROWS = 16
NORM_UNROLL = 64
RWKV_CHUNK = 64
RWKV_INV_BLOCK = 16
VMEM_LIMIT = 56 * 1024 * 1024
FFN_VMEM_LIMIT = 60 * 1024 * 1024


def _tiles(m):
    return dict(ffn_m=min(1024, m), ffn_f=256 if m >= 1024 else 512, proj_m=min(1024, m), mix_m=min(512, m))


def _params(sem, vmem_limit=VMEM_LIMIT):
    return pltpu.CompilerParams(dimension_semantics=sem, vmem_limit_bytes=vmem_limit)


def _layer(arr, idx):
    idx = tuple(idx)
    rest = arr.shape[len(idx):]
    return pl.BlockSpec((None,) * len(idx) + rest, lambda *_: idx + (0,) * len(rest))


def _rms(x, g):
    return x * lax.rsqrt(jnp.mean(x * x, axis=-1, keepdims=True) + NORM_EPS) * g


def _dot(a, b):
    return jnp.dot(a, b, preferred_element_type=F32)


def _for_row_chunks(n_rows, body):
    size = min(NORM_ROWS, n_rows)
    assert n_rows % size == 0

    def step(c, carry):
        body(pl.ds(pl.multiple_of(c * size, size), size))
        return carry

    n = n_rows // size
    lax.fori_loop(0, n, step, 0, unroll=min(NORM_UNROLL, n))


def _norm_rows_into(h_ref, x_ref, g):
    def chunk(rows):
        h_ref[rows, :] = _rms(x_ref[rows, :], g).astype(BF16)

    _for_row_chunks(x_ref.shape[0], chunk)


def _add_normed_rows(o_ref, x_ref, g, scale=1.0):
    def chunk(rows):
        normed = _rms(o_ref[rows, :], g)
        o_ref[rows, :] = x_ref[rows, :] + (normed if scale == 1.0 else scale * normed)

    _for_row_chunks(x_ref.shape[0], chunk)


def _ffn_kernel(x_ref, xs_ref, g0_ref, g1_ref, wg_ref, wu_ref, wd_ref, o_ref, os_ref, h_ref, hs_ref):
    i = pl.program_id(0)
    f = pl.program_id(1)
    g0 = g0_ref[...]
    g1 = g1_ref[...]
    cast = lambda w_ref: w_ref[...].astype(BF16)

    def group(x_ref, o_ref, h_ref):
        @pl.when(f == 0)
        def _():
            def prologue(rows):
                h_ref[rows, :] = _rms(x_ref[rows, :], g0).astype(BF16)
                o_ref[rows, :] = jnp.zeros_like(o_ref[rows, :])

            _for_row_chunks(x_ref.shape[0], prologue)

        h = h_ref[...]
        gate = _dot(h, cast(wg_ref))
        act = (gate * jax.nn.sigmoid(gate) * _dot(h, cast(wu_ref))).astype(BF16)
        o_ref[...] += _dot(act, cast(wd_ref))

        @pl.when(f == pl.num_programs(1) - 1)
        def _():
            _add_normed_rows(o_ref, x_ref, g1, 0.5)

    group(x_ref, o_ref, h_ref)

    @pl.when(i == 0)
    def _():
        group(xs_ref, os_ref, hs_ref)


def _ffn(x, xs, norm_g, g_rows, w_gu, w_down, lj, tm, tf):
    m, d = x.shape
    ms = xs.shape[0]
    l, j = lj
    nf = w_down.shape[2] // tf
    return pl.pallas_call(
        _ffn_kernel,
        grid=(m // tm, nf),
        in_specs=[
            pl.BlockSpec((tm, d), lambda i, f: (i, 0)),
            pl.BlockSpec((ms, d), lambda i, f: (0, 0)),
            _layer(norm_g, (g_rows[0],)),
            _layer(norm_g, (g_rows[1],)),
            pl.BlockSpec((None, None, d, tf), lambda i, f: (l, j, 0, f)),
            pl.BlockSpec((None, None, d, tf), lambda i, f: (l, j, 0, nf + f)),
            pl.BlockSpec((None, None, tf, d), lambda i, f: (l, j, f, 0)),
        ],
        out_specs=[pl.BlockSpec((tm, d), lambda i, f: (i, 0)),
                   pl.BlockSpec((ms, d), lambda i, f: (0, 0))],
        out_shape=[jax.ShapeDtypeStruct((m, d), F32), jax.ShapeDtypeStruct((ms, d), F32)],
        scratch_shapes=[pltpu.VMEM((tm, d), BF16), pltpu.VMEM((ms, d), BF16)],
        compiler_params=_params(("arbitrary", "arbitrary"), FFN_VMEM_LIMIT),
        name="ffn",
    )(x, xs, norm_g, norm_g, w_gu, w_gu, w_down)


def _norm_mm_kernel(x_ref, g_ref, w_ref, o_ref, h_ref):
    @pl.when(pl.program_id(1) == 0)
    def _():
        _norm_rows_into(h_ref, x_ref, g_ref[...])

    o_ref[...] = _dot(h_ref[...], w_ref[...])


def _norm_mm(x, norm_g, g_row, w, li, tm, tn):
    m, d = x.shape
    n = w.shape[-1]
    return pl.pallas_call(
        _norm_mm_kernel,
        grid=(m // tm, n // tn),
        in_specs=[
            pl.BlockSpec((tm, d), lambda i, j: (i, 0)),
            _layer(norm_g, (g_row,)),
            pl.BlockSpec((None, d, tn), lambda i, j: (li, 0, j)),
        ],
        out_specs=pl.BlockSpec((tm, tn), lambda i, j: (i, j)),
        out_shape=jax.ShapeDtypeStruct((m, n), F32),
        scratch_shapes=[pltpu.VMEM((tm, d), BF16)],
        compiler_params=_params(("parallel", "arbitrary")),
        name="ab_in",
    )(x, norm_g, w)


def _conv_in_kernel(x_ref, g_ref, wb_ref, wc_ref, wh_ref, bg_ref, u_ref, h_ref):
    @pl.when(pl.program_id(1) == 0)
    def _():
        _norm_rows_into(h_ref, x_ref, g_ref[...])

    h = h_ref[...]
    bg_ref[...] = _dot(h, wb_ref[...])
    u_ref[...] = _dot(h, wc_ref[...]) * _dot(h, wh_ref[...])


def _conv_in(x, norm_g, g_row, w_in, li, tm, tn):
    m, d = x.shape
    dc = w_in.shape[-1] // 3
    nj = dc // tn
    return pl.pallas_call(
        _conv_in_kernel,
        grid=(m // tm, nj),
        in_specs=[
            pl.BlockSpec((tm, d), lambda i, j: (i, 0)),
            _layer(norm_g, (g_row,)),
            pl.BlockSpec((None, d, tn), lambda i, j: (li, 0, j)),
            pl.BlockSpec((None, d, tn), lambda i, j: (li, 0, nj + j)),
            pl.BlockSpec((None, d, tn), lambda i, j: (li, 0, 2 * nj + j)),
        ],
        out_specs=[pl.BlockSpec((tm, tn), lambda i, j: (i, j)),
                   pl.BlockSpec((tm, tn), lambda i, j: (i, j))],
        out_shape=[jax.ShapeDtypeStruct((m, dc), F32), jax.ShapeDtypeStruct((m, dc), F32)],
        scratch_shapes=[pltpu.VMEM((tm, d), BF16)],
        compiler_params=_params(("parallel", "arbitrary")),
        name="conv_in",
    )(x, norm_g, w_in, w_in, w_in)


def _conv_taps(u, p1, p2, cw):
    return p2 * cw[0:1, :] + p1 * cw[1:2, :] + u * cw[2:3, :]


def _conv_out_seq_kernel(x_ref, bg_ref, u_ref, halo_ref, cw_ref, w_ref, g_ref, o_ref, *, tiles_per_seq):
    u = u_ref[...]
    first = (pl.program_id(0) % tiles_per_seq) == 0
    h1 = jnp.where(first, 0.0, halo_ref[7:8, :])
    h2 = jnp.where(first, 0.0, halo_ref[6:7, :])
    row = lax.broadcasted_iota(jnp.int32, u.shape, 0)
    p1 = jnp.where(row == 0, h1, pltpu.roll(u, 1, 0))
    p2 = jnp.where(row == 0, h2, jnp.where(row == 1, h1, pltpu.roll(u, 2, 0)))
    z = (bg_ref[...] * _conv_taps(u, p1, p2, cw_ref[...])).astype(BF16)
    o_ref[...] = _dot(z, w_ref[...])
    _add_normed_rows(o_ref, x_ref, g_ref[...])


def _conv_out_step_kernel(x_ref, bg_ref, u_ref, p1_ref, p2_ref, cw_ref, w_ref, g_ref, o_ref):
    z = (bg_ref[...] * _conv_taps(u_ref[...], p1_ref[...], p2_ref[...], cw_ref[...])).astype(BF16)
    o_ref[...] = _dot(z, w_ref[...])
    _add_normed_rows(o_ref, x_ref, g_ref[...])


def _conv_out_seq(x, bg, u, conv_w, w_out, norm_g, g_row, li, tm, seq):
    m, d = x.shape
    dc = u.shape[1]
    hb = tm // 8
    row_spec = pl.BlockSpec((tm, dc), lambda i: (i, 0))
    return pl.pallas_call(
        functools.partial(_conv_out_seq_kernel, tiles_per_seq=seq // tm),
        grid=(m // tm,),
        in_specs=[
            pl.BlockSpec((tm, d), lambda i: (i, 0)),
            row_spec, row_spec,
            pl.BlockSpec((8, dc), lambda i: (jnp.maximum(i * hb - 1, 0), 0)),
            _layer(conv_w, (li,)), _layer(w_out, (li,)), _layer(norm_g, (g_row,)),
        ],
        out_specs=pl.BlockSpec((tm, d), lambda i: (i, 0)),
        out_shape=jax.ShapeDtypeStruct((m, d), F32),
        compiler_params=_params(("parallel",)),
        name="conv_out",
    )(x, bg, u, u, conv_w, w_out, norm_g)


def _conv_out_step(x, bg, u, p1, p2, conv_w, w_out, norm_g, g_row, li):
    m, d = x.shape
    dc = u.shape[1]
    row_spec = pl.BlockSpec((m, dc), lambda i: (0, 0))
    return pl.pallas_call(
        _conv_out_step_kernel,
        grid=(1,),
        in_specs=[
            pl.BlockSpec((m, d), lambda i: (0, 0)),
            row_spec, row_spec, row_spec, row_spec,
            _layer(conv_w, (li,)), _layer(w_out, (li,)), _layer(norm_g, (g_row,)),
        ],
        out_specs=pl.BlockSpec((m, d), lambda i: (0, 0)),
        out_shape=jax.ShapeDtypeStruct((m, d), F32),
        compiler_params=_params(("arbitrary",)),
        name="conv_out_step",
    )(x, bg, u, p1, p2, conv_w, w_out, norm_g)


def _rope(x, cos, sin):
    width = x.shape[-1]
    half = HEAD_DIM // 2
    lane = lax.broadcasted_iota(jnp.int32, x.shape, x.ndim - 1)
    swapped = jnp.where((lane % HEAD_DIM) < half,
                        pltpu.roll(x, width - half, x.ndim - 1),
                        pltpu.roll(x, half, x.ndim - 1))
    return x * cos + swapped * sin


def _swa_seq_kernel(sink_ref, q_ref, kc_ref, kp_ref, vc_ref, vp_ref, cq_ref, sq_ref, cp_ref, sp_ref,
                    o_ref, nk_ref, nv_ref, *, li):
    n = pl.program_id(1)
    q = _rope(q_ref[...], cq_ref[...], sq_ref[...])
    kc = _rope(kc_ref[...], cq_ref[:, :A_KV], sq_ref[:, :A_KV])
    kp = _rope(kp_ref[...], cp_ref[...], sp_ref[...])
    kw = jnp.concatenate([kp, kc], axis=0).astype(BF16)
    vw = jnp.concatenate([vp_ref[...], vc_ref[...]], axis=0).astype(BF16)
    rows = A_GROUP * WINDOW
    head = lambda x, h: x[:, h * HEAD_DIM:(h + 1) * HEAD_DIM]
    groups = lambda f: jnp.stack([f(g) for g in range(A_KV_HEADS)], axis=0)
    qg = groups(lambda g: jnp.concatenate([head(q, g * A_GROUP + j) for j in range(A_GROUP)], axis=0))
    sink = groups(lambda g: jnp.concatenate(
        [jnp.full((WINDOW, 1), sink_ref[li, g * A_GROUP + j], F32) for j in range(A_GROUP)], axis=0))
    qi = lax.broadcasted_iota(jnp.int32, (1, rows, 2 * WINDOW), 1) % WINDOW
    kj = lax.broadcasted_iota(jnp.int32, (1, rows, 2 * WINDOW), 2)
    diff = WINDOW + qi - kj
    lo = jnp.where(n > 0, 0, WINDOW)
    valid = (diff >= 0) & (diff <= WINDOW) & (kj >= lo)
    s = _bmm_nt(qg, groups(lambda g: head(kw, g)))
    s = jnp.where(valid, s * (HEAD_DIM ** -0.5), -jnp.inf)
    mx = jnp.maximum(jnp.max(s, axis=-1, keepdims=True), sink)
    e = jnp.exp(s - mx)
    p = e / (jnp.sum(e, axis=-1, keepdims=True) + jnp.exp(sink - mx))
    og = _bmm(p, groups(lambda g: head(vw, g)))
    o_ref[...] = jnp.concatenate(
        [og[h // A_GROUP][(h % A_GROUP) * WINDOW:(h % A_GROUP + 1) * WINDOW] for h in range(A_HEADS)], axis=-1)

    @pl.when(n == pl.num_programs(1) - 1)
    def _():
        nk_ref[...] = kc
        nv_ref[...] = vc_ref[...]


def _swa_seq(proj, sinks, li, cos, sin, batch, seq):
    nb = seq // WINDOW
    kcol = A_Q // A_KV
    vcol = kcol + 1
    cur = lambda b, n: b * nb + n
    prev = lambda b, n: b * nb + jnp.maximum(n - 1, 0)
    return pl.pallas_call(
        functools.partial(_swa_seq_kernel, li=li),
        grid=(batch, nb),
        in_specs=[
            pl.BlockSpec(memory_space=pltpu.SMEM),
            pl.BlockSpec((WINDOW, A_Q), lambda b, n: (cur(b, n), 0)),
            pl.BlockSpec((WINDOW, A_KV), lambda b, n: (cur(b, n), kcol)),
            pl.BlockSpec((WINDOW, A_KV), lambda b, n: (prev(b, n), kcol)),
            pl.BlockSpec((WINDOW, A_KV), lambda b, n: (cur(b, n), vcol)),
            pl.BlockSpec((WINDOW, A_KV), lambda b, n: (prev(b, n), vcol)),
            pl.BlockSpec((WINDOW, A_Q), lambda b, n: (n, 0)),
            pl.BlockSpec((WINDOW, A_Q), lambda b, n: (n, 0)),
            pl.BlockSpec((WINDOW, A_KV), lambda b, n: (jnp.maximum(n - 1, 0), 0)),
            pl.BlockSpec((WINDOW, A_KV), lambda b, n: (jnp.maximum(n - 1, 0), 0)),
        ],
        out_specs=[
            pl.BlockSpec((WINDOW, A_Q), lambda b, n: (cur(b, n), 0)),
            pl.BlockSpec((WINDOW, A_KV), lambda b, n: (b, 0)),
            pl.BlockSpec((WINDOW, A_KV), lambda b, n: (b, 0)),
        ],
        out_shape=[
            jax.ShapeDtypeStruct((batch * seq, A_Q), F32),
            jax.ShapeDtypeStruct((batch * WINDOW, A_KV), F32),
            jax.ShapeDtypeStruct((batch * WINDOW, A_KV), F32),
        ],
        compiler_params=_params(("parallel", "arbitrary")),
        name="swa_seq",
    )(sinks, proj, proj, proj, proj, proj, cos, sin, cos, sin)


def _swa_step_kernel(sink_ref, p_ref, ck_ref, cv_ref, cos_ref, sin_ref, o_ref, nk_ref, nv_ref, *, li):
    q = _rope(p_ref[0, :, 0:A_Q], cos_ref[...], sin_ref[...])
    kn = _rope(p_ref[0, :, A_Q:A_Q + A_KV], cos_ref[:, :A_KV], sin_ref[:, :A_KV])
    vn = p_ref[0, :, A_Q + A_KV:A_COLS]
    ck = ck_ref[0]
    cv = cv_ref[0]
    outs = []
    for h in range(A_HEADS):
        kv = h // A_GROUP
        hs = slice(h * HEAD_DIM, (h + 1) * HEAD_DIM)
        ks = slice(kv * HEAD_DIM, (kv + 1) * HEAD_DIM)
        qh = q[:, hs] * (HEAD_DIM ** -0.5)
        s = jnp.sum(ck[:, ks] * qh, axis=-1, keepdims=True)
        s_new = jnp.sum(kn[:, ks] * qh, axis=-1, keepdims=True)
        sink = sink_ref[li, h]
        mx = jnp.maximum(jnp.maximum(jnp.max(s, axis=0, keepdims=True), s_new), sink)
        e = jnp.exp(s - mx)
        e_new = jnp.exp(s_new - mx)
        den = jnp.sum(e, axis=0, keepdims=True) + e_new + jnp.exp(sink - mx)
        outs.append((jnp.sum(e * cv[:, ks], axis=0, keepdims=True) + e_new * vn[:, ks]) / den)
    o_ref[0] = jnp.concatenate(outs, axis=-1)
    nk_ref[0, 0:WINDOW - 1, :] = ck_ref[0, 1:WINDOW, :]
    nk_ref[0, WINDOW - 1:WINDOW, :] = kn
    nv_ref[0, 0:WINDOW - 1, :] = cv_ref[0, 1:WINDOW, :]
    nv_ref[0, WINDOW - 1:WINDOW, :] = vn


def _swa_step(proj, sinks, li, cache_k, cache_v, cos, sin):
    batch, width = proj.shape
    cache_in = pl.BlockSpec((None, 1, WINDOW, A_KV), lambda b: (li, b, 0, 0))
    cache_out = pl.BlockSpec((1, WINDOW, A_KV), lambda b: (b, 0, 0))
    return pl.pallas_call(
        functools.partial(_swa_step_kernel, li=li),
        grid=(batch,),
        in_specs=[
            pl.BlockSpec(memory_space=pltpu.SMEM),
            pl.BlockSpec((1, 1, width), lambda b: (b, 0, 0)),
            cache_in, cache_in,
            pl.BlockSpec((1, A_Q), lambda b: (0, 0)),
            pl.BlockSpec((1, A_Q), lambda b: (0, 0)),
        ],
        out_specs=[pl.BlockSpec((1, 1, A_Q), lambda b: (b, 0, 0)), cache_out, cache_out],
        out_shape=[
            jax.ShapeDtypeStruct((batch, 1, A_Q), F32),
            jax.ShapeDtypeStruct((batch, WINDOW, A_KV), F32),
            jax.ShapeDtypeStruct((batch, WINDOW, A_KV), F32),
        ],
        compiler_params=_params(("parallel",)),
        name="swa_step",
    )(sinks, proj.reshape(batch, 1, width), cache_k, cache_v, cos, sin)


def _softplus(z):
    return jnp.maximum(z, 0.0) + jnp.log(1.0 + jnp.exp(-jnp.abs(z)))


def _rwkv_in_math(pb, prev, mu, w0, wdec, a0, waaa, wgate):
    xm = pb + (prev - pb) * mu
    r = xm[:, 0:B_WIDTH]
    k = xm[:, B_WIDTH:2 * B_WIDTH]
    v = xm[:, 2 * B_WIDTH:3 * B_WIDTH]
    lora = xm[:, 3 * B_WIDTH:3 * B_WIDTH + LORA_IN]
    dg = xm[:, 3 * B_WIDTH + LORA_IN:]
    w_log = -_softplus(-(w0 + _dot(jnp.tanh(lora).astype(BF16), wdec))) - 0.5
    lw = -jnp.exp(w_log)
    a = jax.nn.sigmoid(a0 + _dot(lora.astype(BF16), waaa))
    g = _dot(jax.nn.sigmoid(dg).astype(BF16), wgate)
    return r, k, v, lw, a, g


def _rwkv_in_step_kernel(p_ref, prev_ref, mu_ref, w0_ref, wdec_ref, a0_ref, waaa_ref, wgate_ref, *out_refs):
    vals = _rwkv_in_math(p_ref[:, A_COLS:], prev_ref[...], mu_ref[...], w0_ref[...], wdec_ref[...], a0_ref[...],
                         waaa_ref[...], wgate_ref[...])
    for ref, val in zip(out_refs, vals):
        ref[...] = val


def _rwkv_in_step(proj, prev, wts, li):
    m, width = proj.shape
    out_spec = pl.BlockSpec((m, B_WIDTH), lambda i: (0, 0))
    return pl.pallas_call(
        _rwkv_in_step_kernel,
        grid=(1,),
        in_specs=[pl.BlockSpec((m, width), lambda i: (0, 0)), pl.BlockSpec((m, B_COLS_PAD), lambda i: (0, 0))]
        + [_layer(a, (li,)) for a in wts],
        out_specs=[out_spec] * 6,
        out_shape=[jax.ShapeDtypeStruct((m, B_WIDTH), F32)] * 6,
        compiler_params=_params(("arbitrary",)),
        name="rwkv_in_step",
    )(proj, prev, *wts)


def _bdg(a, b, dn):
    return lax.dot_general(a.astype(BF16), b.astype(BF16), dn, preferred_element_type=F32)


def _bmm(a, b):
    return _bdg(a, b, (((2,), (1,)), ((0,), (0,))))


def _bmm_nt(a, b):
    return _bdg(a, b, (((2,), (2,)), ((0,), (0,))))


def _bmm_tn(a, b):
    return _bdg(a, b, (((1,), (1,)), ((0,), (0,))))


def _unit_lower_inverse(a, eye, same_block):
    d = jnp.where(same_block, a, 0.0)
    e = a - d
    x = eye + d
    dp = d
    p = 1
    while 2 * p < RWKV_INV_BLOCK:
        dp = _bmm(dp, dp)
        x = _bmm(x, eye + dp)
        p *= 2
    nn = _bmm(x, e)
    return _bmm(_bmm(eye + nn, eye + _bmm(nn, nn)), x)


def _head_sums(x, first_half):
    outs = []
    for p in range(HEAD_PAIRS):
        xp = x[:, p * LANE:(p + 1) * LANE]
        s0 = jnp.sum(jnp.where(first_half, xp, 0.0), axis=-1, keepdims=True)
        s1 = jnp.sum(jnp.where(first_half, 0.0, xp), axis=-1, keepdims=True)
        outs.append(jnp.where(first_half, s0, s1))
    return jnp.concatenate(outs, axis=-1)


def _rwkv_scan_kernel(p_ref, halo_ref, mu_ref, w0_ref, wdec_ref, a0_ref, waaa_ref, wgate_ref,
                      kk_ref, ka_ref, rk_ref, gw_ref, gb_ref, y_ref, s_out_ref, st_ref):
    c = pl.program_id(1)

    @pl.when(c == 0)
    def _():
        st_ref[...] = jnp.zeros_like(st_ref)

    pb = p_ref[:, A_COLS:]
    hrow = jnp.where(c == 0, 0.0, halo_ref[7:8, A_COLS:])
    prev = jnp.where(lax.broadcasted_iota(jnp.int32, pb.shape, 0) == 0, hrow, pltpu.roll(pb, 1, 0))
    r, k, v, lw, a, gate = _rwkv_in_math(pb, prev, mu_ref[...], w0_ref[...], wdec_ref[...], a0_ref[...],
                                         waaa_ref[...], wgate_ref[...])
    cs = r.shape[0]
    c2 = 2 * cs
    iota = lambda shape, dim: lax.broadcasted_iota(jnp.int32, shape, dim)

    tril = jnp.where(iota((cs, cs), 1) <= iota((cs, cs), 0), 1.0, 0.0).astype(BF16)
    lw_hi = lw.astype(BF16)
    rem = lw - lw_hi.astype(F32)
    lw_mid = rem.astype(BF16)
    lw_lo = (rem - lw_mid.astype(F32)).astype(BF16)
    lcum = _dot(tril, lw_hi) + (_dot(tril, lw_mid) + _dot(tril, lw_lo))
    p_t = jnp.exp(lcum)
    p_inv = jnp.exp(-lcum)
    p_prev = jnp.exp(lcum - lw)
    p_end = jnp.exp(lcum[cs - 1:cs, :])

    first_half = iota((cs, LANE), 1) < HEAD_DIM
    kk = k * kk_ref[...]
    kk = kk / jnp.maximum(jnp.sqrt(_head_sums(kk * kk, first_half)), 1e-12)
    k2 = k * (1.0 + (a - 1.0) * ka_ref[...])
    al = -kk * p_prev
    be = kk * a * p_inv
    kt = k2 * p_inv
    rt = r * p_t
    be_end = be * p_end
    kt_end = kt * p_end
    bonus = _head_sums(r * k2 * rk_ref[...], first_half) * v

    np_, one = HEAD_PAIRS, (1,)
    own_half = (iota(one + (c2, LANE), 1) // cs) == (iota(one + (c2, LANE), 2) // HEAD_DIM)
    t2 = iota(one + (c2, c2), 1)
    s2 = iota(one + (c2, c2), 2)
    same_head = (t2 // cs) == (s2 // cs)
    strict = same_head & (s2 < t2)
    lower = same_head & (s2 <= t2)
    same_block = (t2 // RWKV_INV_BLOCK) == (s2 // RWKV_INV_BLOCK)
    eye = jnp.where(t2 == s2, 1.0, 0.0)
    kl = iota(one + (LANE, LANE), 1)
    vl = iota(one + (LANE, LANE), 2)
    st_diag = kl == vl
    st_blocks = (kl // HEAD_DIM) == (vl // HEAD_DIM)

    pairs = lambda x: jnp.stack([x[:, p * LANE:(p + 1) * LANE] for p in range(np_)], axis=0)
    dup = lambda x: jnp.concatenate([x, x], axis=1)
    own = lambda x: jnp.where(own_half, dup(pairs(x)), 0.0).astype(BF16)
    al2 = own(al)
    rt2 = own(rt)
    v2 = own(v)
    prod = _bmm_nt(jnp.concatenate([al2, rt2], axis=1),
                   jnp.concatenate([dup(pairs(be)), dup(pairs(kt))], axis=1))
    a_ab = jnp.where(strict, prod[:, :c2, :c2], 0.0)
    a_ak = jnp.where(strict, prod[:, :c2, c2:], 0.0)
    m_rb = jnp.where(lower, prod[:, c2:, :c2], 0.0)
    m_rk = jnp.where(lower, prod[:, c2:, c2:], 0.0)
    tinv = _unit_lower_inverse(a_ab, eye, same_block)
    st = st_ref[...]
    u2 = _bmm(tinv, _bmm(al2, st) + _bmm(a_ak, v2))
    y2 = _bmm(rt2, st) + _bmm(m_rb, u2) + _bmm(m_rk, v2)
    y3 = y2[:, :cs] + y2[:, cs:]
    u = u2[:, :cs] + u2[:, cs:]
    p_col = jnp.sum(jnp.where(st_diag, pairs(p_end), 0.0), axis=2, keepdims=True)
    inc = _bmm_tn(pairs(be_end), u) + _bmm_tn(pairs(kt_end), pairs(v))
    st_ref[...] = st * p_col + jnp.where(st_blocks, inc, 0.0)

    y = jnp.concatenate([y3[p] for p in range(np_)], axis=-1)
    mean = _head_sums(y, first_half) * (1.0 / HEAD_DIM)
    yc = y - mean
    var = _head_sums(yc * yc, first_half) * (1.0 / HEAD_DIM)
    y_ref[...] = (yc * lax.rsqrt(var + GN_EPS) * gw_ref[...] + gb_ref[...] + bonus) * gate

    @pl.when(c == pl.num_programs(1) - 1)
    def _():
        for p in range(np_):
            sp = st_ref[p].T
            s_out_ref[0, 2 * p] = sp[:HEAD_DIM, :HEAD_DIM]
            s_out_ref[0, 2 * p + 1] = sp[HEAD_DIM:, HEAD_DIM:]


def _rwkv_scan(proj, in_wts, head_wts, li, batch, cs):
    m, width = proj.shape
    nc = m // batch // cs
    st_spec = pl.BlockSpec((1, B_HEADS, HEAD_DIM, HEAD_DIM), lambda b, c: (b, 0, 0, 0))
    return pl.pallas_call(
        _rwkv_scan_kernel,
        grid=(batch, nc),
        in_specs=[pl.BlockSpec((cs, width), lambda b, c: (b * nc + c, 0)),
                  pl.BlockSpec((8, width), lambda b, c: (jnp.maximum((b * nc + c) * (cs // 8) - 1, 0), 0))]
        + [_layer(t, (li,)) for t in in_wts + head_wts],
        out_specs=[pl.BlockSpec((cs, B_WIDTH), lambda b, c: (b * nc + c, 0)), st_spec],
        out_shape=[jax.ShapeDtypeStruct((m, B_WIDTH), F32),
                   jax.ShapeDtypeStruct((batch, B_HEADS, HEAD_DIM, HEAD_DIM), F32)],
        scratch_shapes=[pltpu.VMEM((HEAD_PAIRS, LANE, LANE), F32)],
        compiler_params=_params(("parallel", "arbitrary")),
        name="rwkv_scan",
    )(proj, proj, *in_wts, *head_wts)


def _rwkv_step_kernel(r_ref, k_ref, v_ref, lw_ref, a_ref, g_ref, s_ref, kk_ref, ka_ref, rk_ref, gw_ref, gb_ref,
                      y_ref, s_out_ref):
    bb = r_ref.shape[0]
    n = bb * B_HEADS

    def heads(ref):
        x = ref[...]
        return jnp.stack([x[b:b + 1, h * HEAD_DIM:(h + 1) * HEAD_DIM]
                          for b in range(bb) for h in range(B_HEADS)], axis=0)

    per_seq = lambda ref: jnp.concatenate([ref[...]] * bb, axis=0)
    r, k, v, lw, a = heads(r_ref), heads(k_ref), heads(v_ref), heads(lw_ref), heads(a_ref)
    s = s_ref[...].reshape(n, HEAD_DIM, HEAD_DIM)
    kk = k * per_seq(kk_ref)
    kk = kk / jnp.maximum(jnp.sqrt(jnp.sum(kk * kk, axis=-1, keepdims=True)), 1e-12)
    k2 = k * (1.0 + (a - 1.0) * per_seq(ka_ref))
    eye = (lax.broadcasted_iota(jnp.int32, (1, HEAD_DIM, HEAD_DIM), 1)
           == lax.broadcasted_iota(jnp.int32, (1, HEAD_DIM, HEAD_DIM), 2))
    sa = jnp.sum(s * (-kk), axis=-1, keepdims=True)
    v_col = jnp.sum(jnp.where(eye, v, 0.0), axis=-1, keepdims=True)
    s_new = s * jnp.exp(lw) + sa * (kk * a) + v_col * k2
    y_col = jnp.sum(s_new * r, axis=-1, keepdims=True)
    y = jnp.sum(jnp.where(eye, y_col, 0.0), axis=1, keepdims=True)
    mean = jnp.mean(y, axis=-1, keepdims=True)
    yc = y - mean
    var = jnp.mean(yc * yc, axis=-1, keepdims=True)
    out = (yc * lax.rsqrt(var + GN_EPS) * per_seq(gw_ref) + per_seq(gb_ref)
           + jnp.sum(r * k2 * per_seq(rk_ref), axis=-1, keepdims=True) * v)
    s_out_ref[...] = s_new.reshape(bb, B_HEADS, HEAD_DIM, HEAD_DIM)
    y_ref[...] = g_ref[...] * jnp.concatenate(
        [jnp.concatenate([out[b * B_HEADS + h] for h in range(B_HEADS)], axis=-1) for b in range(bb)], axis=0)


def _rwkv_step(r, k, v, lw, a, gate, state, head_wts, li, bb):
    batch = r.shape[0]
    row_spec = pl.BlockSpec((bb, B_WIDTH), lambda b: (b, 0))
    st_shape = (bb, B_HEADS, HEAD_DIM, HEAD_DIM)
    return pl.pallas_call(
        _rwkv_step_kernel,
        grid=(batch // bb,),
        in_specs=[row_spec] * 6 + [pl.BlockSpec((None,) + st_shape, lambda b: (li, b, 0, 0, 0))]
        + [_layer(t, (li,)) for t in head_wts],
        out_specs=[row_spec, pl.BlockSpec(st_shape, lambda b: (b, 0, 0, 0))],
        out_shape=[jax.ShapeDtypeStruct((batch, B_WIDTH), F32),
                   jax.ShapeDtypeStruct((batch, B_HEADS, HEAD_DIM, HEAD_DIM), F32)],
        compiler_params=_params(("parallel",)),
        name="rwkv_step",
    )(r, k, v, lw, a, gate, state, *head_wts)


def _ab_out_kernel(x_ref, ao_ref, yb_ref, wa_ref, wb_ref, g_ref, o_ref):
    mix = _dot(ao_ref[...].astype(BF16), wa_ref[...])
    mix += _dot(yb_ref[...].astype(BF16), wb_ref[...])
    o_ref[...] = mix
    _add_normed_rows(o_ref, x_ref, g_ref[...])


def _ab_out(x, ao, yb, w_out, norm_g, g_row, li, tm):
    m, d = x.shape
    return pl.pallas_call(
        _ab_out_kernel,
        grid=(m // tm,),
        in_specs=[
            pl.BlockSpec((tm, d), lambda i: (i, 0)),
            pl.BlockSpec((tm, A_Q), lambda i: (i, 0)),
            pl.BlockSpec((tm, B_WIDTH), lambda i: (i, 0)),
            pl.BlockSpec((None, A_Q, d), lambda i: (li, 0, 0)),
            pl.BlockSpec((None, B_WIDTH, d), lambda i: (li, 1, 0)),
            _layer(norm_g, (g_row,)),
        ],
        out_specs=pl.BlockSpec((tm, d), lambda i: (i, 0)),
        out_shape=jax.ShapeDtypeStruct((m, d), F32),
        compiler_params=_params(("parallel",)),
        name="ab_out",
    )(x, ao, yb, w_out, w_out, norm_g)


def _rope_tables(pos):
    half = HEAD_DIM // 2
    freqs = ROPE_THETA ** (-jnp.arange(half, dtype=F32) / half)
    ang = pos.astype(F32)[:, None] * freqs[None, :]
    cos = jnp.cos(ang)
    sin = jnp.sin(ang)
    return (jnp.tile(jnp.concatenate([cos, cos], axis=-1), (1, A_HEADS)),
            jnp.tile(jnp.concatenate([-sin, sin], axis=-1), (1, A_HEADS)))


def _trunk(xp3, xs3, past, w):
    bp, seq, d = xp3.shape
    bs = xs3.shape[0]
    mp = bp * seq
    xp = xp3.reshape(mp, d)
    xs = xs3.reshape(bs, d)
    tp = _tiles(mp)
    ts = _tiles(bs)
    assert all(seq % n == 0 for n in (tp["mix_m"], RWKV_CHUNK, WINDOW)) and mp % tp["ffn_m"] == 0
    cos_p, sin_p = _rope_tables(jnp.arange(seq, dtype=jnp.int32))
    cos_s, sin_s = _rope_tables(PAST_LEN + jnp.arange(1, dtype=jnp.int32))
    cache_k, cache_v, state, shift, conv_buf = past
    ng = w["norm_g"]
    outs = {key: [] for key in ("pk", "pv", "ps", "psh", "pc", "sk", "sv", "ss", "ssh", "sc")}
    kv_heads = lambda t, b: t.reshape(b, WINDOW, A_KV_HEADS, HEAD_DIM)
    for l in range(w["depth"]):
        i = l // 2
        grow = lambda j: l * 6 + j
        xp, xs = _ffn(xp, xs, ng, (grow(0), grow(1)), w["ffn_w_gu"], w["ffn_w_down"], (l, 0),
                      tp["ffn_m"], tp["ffn_f"])
        if l % 2 == 0:
            proj = _norm_mm(xp, ng, grow(2), w["ab_w_in"], i, tp["proj_m"], 1280)
            ao, nk, nv = _swa_seq(proj, w["attn_sinks"], i, cos_p, sin_p, bp, seq)
            yb, st = _rwkv_scan(proj, w["rwkv_in"], w["rwkv_rows"], i, bp, RWKV_CHUNK)
            xp = _ab_out(xp, ao, yb, w["ab_w_out"], ng, grow(3), i, tp["mix_m"])
            outs["pk"].append(kv_heads(nk, bp))
            outs["pv"].append(kv_heads(nv, bp))
            outs["ps"].append(st)
            outs["psh"].append(proj.reshape(bp, seq, AB_COLS_PAD)[:, -1, A_COLS:AB_COLS])

            proj = _norm_mm(xs, ng, grow(2), w["ab_w_in"], i, ts["proj_m"], 1280)
            ao, nk, nv = _swa_step(proj, w["attn_sinks"], i, cache_k, cache_v, cos_s, sin_s)
            prev = jnp.pad(shift[i], ((0, 0), (0, B_COLS_PAD - B_COLS)))
            r, k, v, lw, a, gt = _rwkv_in_step(proj, prev, w["rwkv_in"], i)
            yb, st = _rwkv_step(r, k, v, lw, a, gt, state, w["rwkv_heads"], i, min(8, bs))
            xs = _ab_out(xs, ao.reshape(bs, A_Q), yb, w["ab_w_out"], ng, grow(3), i, ts["mix_m"])
            outs["sk"].append(kv_heads(nk, bs))
            outs["sv"].append(kv_heads(nv, bs))
            outs["ss"].append(st)
            outs["ssh"].append(proj[:, A_COLS:AB_COLS])
        else:
            bg, u = _conv_in(xp, ng, grow(2), w["conv_w_in"], i, tp["proj_m"], 512)
            xp = _conv_out_seq(xp, bg, u, w["conv_w"], w["conv_w_out"], ng, grow(3), i, tp["mix_m"], seq)
            outs["pc"].append(u.reshape(bp, seq, -1)[:, -(CONV_W - 1):])

            bg, u = _conv_in(xs, ng, grow(2), w["conv_w_in"], i, ts["proj_m"], 512)
            buf = conv_buf[i]
            xs = _conv_out_step(xs, bg, u, buf[:, 1], buf[:, 0], w["conv_w"], w["conv_w_out"], ng, grow(3), i)
            outs["sc"].append(jnp.stack([buf[:, 1], u], axis=1))
        xp, xs = _ffn(xp, xs, ng, (grow(4), grow(5)), w["ffn_w_gu"], w["ffn_w_down"], (l, 1),
                      tp["ffn_m"], tp["ffn_f"])
    stacked = {key: jnp.stack(val) for key, val in outs.items()}
    return xp.reshape(bp, seq, d), xs.reshape(bs, 1, d), stacked


def _prep_weights(norm_g, ffn_w_gu, ffn_w_down, ab_w_in, ab_w_out, attn_sinks, rwkv_mu, rwkv_w0, rwkv_w_decay,
                  rwkv_a0, rwkv_w_aaa, rwkv_w_gate, rwkv_k_k, rwkv_k_a, rwkv_r_k, rwkv_gn_w, rwkv_gn_b,
                  conv_w_in, conv_w, conv_w_out):
    depth, n_norm, d = norm_g.shape
    padc = lambda t, n: jnp.pad(t, [(0, 0)] * (t.ndim - 1) + [(0, n - t.shape[-1])])
    row = lambda t: t[:, None, :]
    wdec = jnp.pad(rwkv_w_decay, ((0, 0), (0, D_AAA), (0, 0))).astype(BF16)
    waaa = jnp.pad(rwkv_w_aaa, ((0, 0), (D_DECAY, 0), (0, 0))).astype(BF16)
    wgate = jnp.pad(rwkv_w_gate, ((0, 0), (0, GATE_PAD - D_GATE), (0, 0))).astype(BF16)
    return {
        "depth": depth,
        "norm_g": norm_g.reshape(depth * n_norm, 1, d),
        "ffn_w_gu": ffn_w_gu,
        "ffn_w_down": ffn_w_down,
        "ab_w_in": padc(ab_w_in, AB_COLS_PAD).astype(BF16),
        "ab_w_out": ab_w_out.astype(BF16),
        "attn_sinks": attn_sinks,
        "rwkv_in": (row(padc(rwkv_mu, B_COLS_PAD)), row(rwkv_w0), wdec, row(rwkv_a0), waaa, wgate),
        "rwkv_rows": tuple(row(t) for t in (rwkv_k_k, rwkv_k_a, rwkv_r_k, rwkv_gn_w, rwkv_gn_b)),
        "rwkv_heads": tuple(t.reshape(-1, B_HEADS, 1, HEAD_DIM)
                            for t in (rwkv_k_k, rwkv_k_a, rwkv_r_k, rwkv_gn_w, rwkv_gn_b)),
        "conv_w_in": conv_w_in.astype(BF16),
        "conv_w": conv_w,
        "conv_w_out": conv_w_out.astype(BF16),
    }


def kernel(x_prompt, x_sample, cache_swa_k, cache_swa_v, state_rwkv, state_rwkv_shift, state_conv, norm_g, ffn_w_gu, ffn_w_down, ab_w_in, ab_w_out, attn_sinks, rwkv_mu, rwkv_w0, rwkv_w_decay, rwkv_a0, rwkv_w_aaa, rwkv_w_gate, rwkv_k_k, rwkv_k_a, rwkv_r_k, rwkv_gn_w, rwkv_gn_b, conv_w_in, conv_w, conv_w_out):
    w = _prep_weights(norm_g, ffn_w_gu, ffn_w_down, ab_w_in, ab_w_out, attn_sinks, rwkv_mu, rwkv_w0, rwkv_w_decay,
                      rwkv_a0, rwkv_w_aaa, rwkv_w_gate, rwkv_k_k, rwkv_k_a, rwkv_r_k, rwkv_gn_w, rwkv_gn_b,
                      conv_w_in, conv_w, conv_w_out)
    n_ab, dec_batch = cache_swa_k.shape[:2]
    past = (cache_swa_k.reshape(n_ab, dec_batch, WINDOW, A_KV), cache_swa_v.reshape(n_ab, dec_batch, WINDOW, A_KV),
            state_rwkv, state_rwkv_shift, state_conv)
    y_prompt, y_sample, o = _trunk(x_prompt, x_sample, past, w)
    return (y_prompt, y_sample, o["pk"], o["pv"], o["ps"], o["psh"], o["pc"],
            o["sk"], o["sv"], o["ss"], o["ssh"], o["sc"])
```

```python
import functools

import jax
import jax.numpy as jnp
from jax import lax
from jax.experimental import pallas as pl
from jax.experimental.pallas import tpu as pltpu

F32 = jnp.float32
BF16 = jnp.bfloat16

HEAD_DIM = 64
A_HEADS = 16
A_KV_HEADS = 4
A_GROUP = A_HEADS // A_KV_HEADS
WINDOW = 128
ROPE_THETA = 10000.0
PAST_LEN = 16384
A_Q = A_HEADS * HEAD_DIM
A_KV = A_KV_HEADS * HEAD_DIM
A_COLS = A_Q + 2 * A_KV
B_HEADS = 16
B_WIDTH = B_HEADS * HEAD_DIM
D_DECAY = 64
D_AAA = 64
D_GATE = 160
B_COLS = 3 * B_WIDTH + D_DECAY + D_AAA + D_GATE
GN_EPS = 64e-5
AB_COLS = A_COLS + B_COLS
CONV_W = 3
NORM_EPS = 1e-6

LANE = 128
HEAD_PAIRS = B_WIDTH // LANE
AB_COLS_PAD = 5120
B_COLS_PAD = AB_COLS_PAD - A_COLS
LORA_IN = D_DECAY + D_AAA
GATE_PAD = AB_COLS_PAD - A_COLS - 3 * B_WIDTH - LORA_IN
NORM_ROWS = 16
NORM_UNROLL = 64
RWKV_CHUNK = 64
RWKV_INV_BLOCK = 16
VMEM_LIMIT = 56 * 1024 * 1024
WIDE_VMEM_LIMIT = 60 * 1024 * 1024


def _tiles(m):
    return dict(ffn_m=min(1024, m), ffn_f=256 if m >= 1024 else 512, proj_m=min(1024, m), mix_m=min(512, m))


def _params(sem, vmem_limit=VMEM_LIMIT):
    return pltpu.CompilerParams(dimension_semantics=sem, vmem_limit_bytes=vmem_limit)


def _layer(arr, idx):
    idx = tuple(idx)
    rest = arr.shape[len(idx):]
    return pl.BlockSpec((None,) * len(idx) + rest, lambda *_: idx + (0,) * len(rest))


def _rms(x, g):
    return x * lax.rsqrt(jnp.mean(x * x, axis=-1, keepdims=True) + NORM_EPS) * g


def _dot(a, b):
    return jnp.dot(a, b, preferred_element_type=F32)


def _for_row_chunks(n_rows, body):
    size = min(NORM_ROWS, n_rows)
    assert n_rows % size == 0

    def step(c, carry):
        body(pl.ds(pl.multiple_of(c * size, size), size))
        return carry

    n = n_rows // size
    lax.fori_loop(0, n, step, 0, unroll=min(NORM_UNROLL, n))


def _norm_rows_into(h_ref, x_ref, g):
    def chunk(rows):
        h_ref[rows, :] = _rms(x_ref[rows, :], g).astype(BF16)

    _for_row_chunks(x_ref.shape[0], chunk)


def _add_normed_rows(o_ref, x_ref, g, scale=1.0):
    def chunk(rows):
        normed = _rms(o_ref[rows, :], g)
        o_ref[rows, :] = x_ref[rows, :] + (normed if scale == 1.0 else scale * normed)

    _for_row_chunks(x_ref.shape[0], chunk)


def _ffn_kernel(x_ref, xs_ref, g0_ref, g1_ref, wg_ref, wu_ref, wd_ref, o_ref, os_ref, h_ref, hs_ref):
    i = pl.program_id(0)
    f = pl.program_id(1)
    g0 = g0_ref[...]
    g1 = g1_ref[...]
    cast = lambda w_ref: w_ref[...].astype(BF16)

    def group(x_ref, o_ref, h_ref):
        def hidden_tile(accumulate):
            h = h_ref[...]
            gate = _dot(h, cast(wg_ref))
            act = (gate * jax.nn.sigmoid(gate) * _dot(h, cast(wu_ref))).astype(BF16)
            if accumulate:
                o_ref[...] += _dot(act, cast(wd_ref))
            else:
                o_ref[...] = _dot(act, cast(wd_ref))

        @pl.when(f == 0)
        def _():
            _norm_rows_into(h_ref, x_ref, g0)
            hidden_tile(False)

        @pl.when(f > 0)
        def _():
            hidden_tile(True)

        @pl.when(f == pl.num_programs(1) - 1)
        def _():
            _add_normed_rows(o_ref, x_ref, g1, 0.5)

    group(x_ref, o_ref, h_ref)

    @pl.when(i == 0)
    def _():
        group(xs_ref, os_ref, hs_ref)


def _ffn(x, xs, norm_g, g_rows, w_gu, w_down, lj, tm, tf):
    m, d = x.shape
    ms = xs.shape[0]
    l, j = lj
    nf = w_down.shape[2] // tf
    return pl.pallas_call(
        _ffn_kernel,
        grid=(m // tm, nf),
        in_specs=[
            pl.BlockSpec((tm, d), lambda i, f: (i, 0)),
            pl.BlockSpec((ms, d), lambda i, f: (0, 0)),
            _layer(norm_g, (g_rows[0],)),
            _layer(norm_g, (g_rows[1],)),
            pl.BlockSpec((None, None, d, tf), lambda i, f: (l, j, 0, f)),
            pl.BlockSpec((None, None, d, tf), lambda i, f: (l, j, 0, nf + f)),
            pl.BlockSpec((None, None, tf, d), lambda i, f: (l, j, f, 0)),
        ],
        out_specs=[pl.BlockSpec((tm, d), lambda i, f: (i, 0)),
                   pl.BlockSpec((ms, d), lambda i, f: (0, 0))],
        out_shape=[jax.ShapeDtypeStruct((m, d), F32), jax.ShapeDtypeStruct((ms, d), F32)],
        scratch_shapes=[pltpu.VMEM((tm, d), BF16), pltpu.VMEM((ms, d), BF16)],
        compiler_params=_params(("arbitrary", "arbitrary"), WIDE_VMEM_LIMIT),
        name="ffn",
    )(x, xs, norm_g, norm_g, w_gu, w_gu, w_down)


def _norm_mm_kernel(x_ref, g_ref, w_ref, o_ref, h_ref):
    @pl.when(pl.program_id(1) == 0)
    def _():
        _norm_rows_into(h_ref, x_ref, g_ref[...])

    o_ref[...] = _dot(h_ref[...], w_ref[...])


def _norm_mm(x, norm_g, g_row, w, li, tm, tn):
    m, d = x.shape
    n = w.shape[-1]
    return pl.pallas_call(
        _norm_mm_kernel,
        grid=(m // tm, n // tn),
        in_specs=[
            pl.BlockSpec((tm, d), lambda i, j: (i, 0)),
            _layer(norm_g, (g_row,)),
            pl.BlockSpec((None, d, tn), lambda i, j: (li, 0, j)),
        ],
        out_specs=pl.BlockSpec((tm, tn), lambda i, j: (i, j)),
        out_shape=jax.ShapeDtypeStruct((m, n), F32),
        scratch_shapes=[pltpu.VMEM((tm, d), BF16)],
        compiler_params=_params(("parallel", "arbitrary")),
        name="ab_in",
    )(x, norm_g, w)


def _conv_in_kernel(x_ref, g_ref, wb_ref, wc_ref, wh_ref, bg_ref, u_ref, h_ref):
    @pl.when(pl.program_id(1) == 0)
    def _():
        _norm_rows_into(h_ref, x_ref, g_ref[...])

    h = h_ref[...]
    bg_ref[...] = _dot(h, wb_ref[...].astype(BF16))
    u_ref[...] = _dot(h, wc_ref[...].astype(BF16)) * _dot(h, wh_ref[...].astype(BF16))


def _conv_in(x, norm_g, g_row, w_in, li, tm, tn):
    m, d = x.shape
    dc = w_in.shape[-1] // 3
    nj = dc // tn
    return pl.pallas_call(
        _conv_in_kernel,
        grid=(m // tm, nj),
        in_specs=[
            pl.BlockSpec((tm, d), lambda i, j: (i, 0)),
            _layer(norm_g, (g_row,)),
            pl.BlockSpec((None, d, tn), lambda i, j: (li, 0, j)),
            pl.BlockSpec((None, d, tn), lambda i, j: (li, 0, nj + j)),
            pl.BlockSpec((None, d, tn), lambda i, j: (li, 0, 2 * nj + j)),
        ],
        out_specs=[pl.BlockSpec((tm, tn), lambda i, j: (i, j)),
                   pl.BlockSpec((tm, tn), lambda i, j: (i, j))],
        out_shape=[jax.ShapeDtypeStruct((m, dc), F32), jax.ShapeDtypeStruct((m, dc), F32)],
        scratch_shapes=[pltpu.VMEM((tm, d), BF16)],
        compiler_params=_params(("parallel", "arbitrary"), WIDE_VMEM_LIMIT),
        name="conv_in",
    )(x, norm_g, w_in, w_in, w_in)


def _conv_taps(u, p1, p2, cw):
    return p2 * cw[0:1, :] + p1 * cw[1:2, :] + u * cw[2:3, :]


def _conv_out_seq_kernel(x_ref, bg_ref, u_ref, halo_ref, cw_ref, w_ref, g_ref, o_ref, *, tiles_per_seq):
    u = u_ref[...]
    first = (pl.program_id(0) % tiles_per_seq) == 0
    h1 = jnp.where(first, 0.0, halo_ref[7:8, :])
    h2 = jnp.where(first, 0.0, halo_ref[6:7, :])
    row = lax.broadcasted_iota(jnp.int32, u.shape, 0)
    p1 = jnp.where(row == 0, h1, pltpu.roll(u, 1, 0))
    p2 = jnp.where(row == 0, h2, jnp.where(row == 1, h1, pltpu.roll(u, 2, 0)))
    z = (bg_ref[...] * _conv_taps(u, p1, p2, cw_ref[...])).astype(BF16)
    o_ref[...] = _dot(z, w_ref[...])
    _add_normed_rows(o_ref, x_ref, g_ref[...])


def _conv_out_step_kernel(x_ref, bg_ref, u_ref, p1_ref, p2_ref, cw_ref, w_ref, g_ref, o_ref):
    z = (bg_ref[...] * _conv_taps(u_ref[...], p1_ref[...], p2_ref[...], cw_ref[...])).astype(BF16)
    o_ref[...] = _dot(z, w_ref[...])
    _add_normed_rows(o_ref, x_ref, g_ref[...])


def _conv_out_seq(x, bg, u, conv_w, w_out, norm_g, g_row, li, tm, seq):
    m, d = x.shape
    dc = u.shape[1]
    hb = tm // 8
    row_spec = pl.BlockSpec((tm, dc), lambda i: (i, 0))
    return pl.pallas_call(
        functools.partial(_conv_out_seq_kernel, tiles_per_seq=seq // tm),
        grid=(m // tm,),
        in_specs=[
            pl.BlockSpec((tm, d), lambda i: (i, 0)),
            row_spec, row_spec,
            pl.BlockSpec((8, dc), lambda i: (jnp.maximum(i * hb - 1, 0), 0)),
            _layer(conv_w, (li,)), _layer(w_out, (li,)), _layer(norm_g, (g_row,)),
        ],
        out_specs=pl.BlockSpec((tm, d), lambda i: (i, 0)),
        out_shape=jax.ShapeDtypeStruct((m, d), F32),
        compiler_params=_params(("parallel",)),
        name="conv_out",
    )(x, bg, u, u, conv_w, w_out, norm_g)


def _conv_out_step(x, bg, u, p1, p2, conv_w, w_out, norm_g, g_row, li):
    m, d = x.shape
    dc = u.shape[1]
    row_spec = pl.BlockSpec((m, dc), lambda i: (0, 0))
    return pl.pallas_call(
        _conv_out_step_kernel,
        grid=(1,),
        in_specs=[
            pl.BlockSpec((m, d), lambda i: (0, 0)),
            row_spec, row_spec, row_spec, row_spec,
            _layer(conv_w, (li,)), _layer(w_out, (li,)), _layer(norm_g, (g_row,)),
        ],
        out_specs=pl.BlockSpec((m, d), lambda i: (0, 0)),
        out_shape=jax.ShapeDtypeStruct((m, d), F32),
        compiler_params=_params(("arbitrary",)),
        name="conv_out_step",
    )(x, bg, u, p1, p2, conv_w, w_out, norm_g)


def _rope(x, cos, sin):
    width = x.shape[-1]
    half = HEAD_DIM // 2
    lane = lax.broadcasted_iota(jnp.int32, x.shape, x.ndim - 1)
    swapped = jnp.where((lane % HEAD_DIM) < half,
                        pltpu.roll(x, width - half, x.ndim - 1),
                        pltpu.roll(x, half, x.ndim - 1))
    return x * cos + swapped * sin


def _swa_seq_kernel(sink_ref, q_ref, kc_ref, kp_ref, vc_ref, vp_ref, cq_ref, sq_ref, cp_ref, sp_ref,
                    o_ref, nk_ref, nv_ref, *, li):
    n = pl.program_id(1)
    q = _rope(q_ref[...], cq_ref[...], sq_ref[...])
    kc = _rope(kc_ref[...], cq_ref[:, :A_KV], sq_ref[:, :A_KV])
    kp = _rope(kp_ref[...], cp_ref[...], sp_ref[...])
    kw = jnp.concatenate([kp, kc], axis=0).astype(BF16)
    vw = jnp.concatenate([vp_ref[...], vc_ref[...]], axis=0).astype(BF16)
    rows = A_GROUP * WINDOW
    head = lambda x, h: x[:, h * HEAD_DIM:(h + 1) * HEAD_DIM]
    groups = lambda f: jnp.stack([f(g) for g in range(A_KV_HEADS)], axis=0)
    qg = groups(lambda g: jnp.concatenate([head(q, g * A_GROUP + j) for j in range(A_GROUP)], axis=0))
    sink = groups(lambda g: jnp.concatenate(
        [jnp.full((WINDOW, 1), sink_ref[li, g * A_GROUP + j], F32) for j in range(A_GROUP)], axis=0))
    qi = lax.broadcasted_iota(jnp.int32, (1, rows, 2 * WINDOW), 1) % WINDOW
    kj = lax.broadcasted_iota(jnp.int32, (1, rows, 2 * WINDOW), 2)
    diff = WINDOW + qi - kj
    lo = jnp.where(n > 0, 0, WINDOW)
    valid = (diff >= 0) & (diff <= WINDOW) & (kj >= lo)
    s = _bmm_nt(qg, groups(lambda g: head(kw, g)))
    s = jnp.where(valid, s * (HEAD_DIM ** -0.5), -jnp.inf)
    mx = jnp.maximum(jnp.max(s, axis=-1, keepdims=True), sink)
    e = jnp.exp(s - mx)
    p = e / (jnp.sum(e, axis=-1, keepdims=True) + jnp.exp(sink - mx))
    og = _bmm(p, groups(lambda g: head(vw, g)))
    o_ref[...] = jnp.concatenate(
        [og[h // A_GROUP][(h % A_GROUP) * WINDOW:(h % A_GROUP + 1) * WINDOW] for h in range(A_HEADS)], axis=-1)

    @pl.when(n == pl.num_programs(1) - 1)
    def _():
        nk_ref[...] = kc
        nv_ref[...] = vc_ref[...]


def _swa_seq(proj, sinks, li, cos, sin, batch, seq):
    nb = seq // WINDOW
    kcol = A_Q // A_KV
    vcol = kcol + 1
    cur = lambda b, n: b * nb + n
    prev = lambda b, n: b * nb + jnp.maximum(n - 1, 0)
    return pl.pallas_call(
        functools.partial(_swa_seq_kernel, li=li),
        grid=(batch, nb),
        in_specs=[
            pl.BlockSpec(memory_space=pltpu.SMEM),
            pl.BlockSpec((WINDOW, A_Q), lambda b, n: (cur(b, n), 0)),
            pl.BlockSpec((WINDOW, A_KV), lambda b, n: (cur(b, n), kcol)),
            pl.BlockSpec((WINDOW, A_KV), lambda b, n: (prev(b, n), kcol)),
            pl.BlockSpec((WINDOW, A_KV), lambda b, n: (cur(b, n), vcol)),
            pl.BlockSpec((WINDOW, A_KV), lambda b, n: (prev(b, n), vcol)),
            pl.BlockSpec((WINDOW, A_Q), lambda b, n: (n, 0)),
            pl.BlockSpec((WINDOW, A_Q), lambda b, n: (n, 0)),
            pl.BlockSpec((WINDOW, A_KV), lambda b, n: (jnp.maximum(n - 1, 0), 0)),
            pl.BlockSpec((WINDOW, A_KV), lambda b, n: (jnp.maximum(n - 1, 0), 0)),
        ],
        out_specs=[
            pl.BlockSpec((WINDOW, A_Q), lambda b, n: (cur(b, n), 0)),
            pl.BlockSpec((WINDOW, A_KV), lambda b, n: (b, 0)),
            pl.BlockSpec((WINDOW, A_KV), lambda b, n: (b, 0)),
        ],
        out_shape=[
            jax.ShapeDtypeStruct((batch * seq, A_Q), F32),
            jax.ShapeDtypeStruct((batch * WINDOW, A_KV), F32),
            jax.ShapeDtypeStruct((batch * WINDOW, A_KV), F32),
        ],
        compiler_params=_params(("parallel", "arbitrary")),
        name="swa_seq",
    )(sinks, proj, proj, proj, proj, proj, cos, sin, cos, sin)


def _swa_step_kernel(sink_ref, p_ref, ck_ref, cv_ref, cos_ref, sin_ref, o_ref, nk_ref, nv_ref, *, li):
    sb = p_ref.shape[0]
    pairs = A_KV // LANE
    per_pair = A_HEADS // pairs
    q = _rope(p_ref[:, 0:A_Q], cos_ref[...], sin_ref[...]) * (HEAD_DIM ** -0.5)
    kn = _rope(p_ref[:, A_Q:A_Q + A_KV], cos_ref[:, :A_KV], sin_ref[:, :A_KV])
    vn = p_ref[:, A_Q + A_KV:A_COLS]
    zero = jnp.zeros((1, HEAD_DIM), F32)

    def q_row(s, h):
        piece = q[s:s + 1, h * HEAD_DIM:(h + 1) * HEAD_DIM]
        return jnp.concatenate([piece, zero] if (h // A_GROUP) % 2 == 0 else [zero, piece], axis=-1)

    batch = [(s, p) for s in range(sb) for p in range(pairs)]
    qb = jnp.stack([jnp.concatenate([q_row(s, p * per_pair + r) for r in range(per_pair)], axis=0)
                    for s, p in batch], axis=0)
    lanes = lambda x, s, p: x[s:s + 1, p * LANE:(p + 1) * LANE]
    knb = jnp.stack([lanes(kn, s, p) for s, p in batch], axis=0)
    vnb = jnp.stack([lanes(vn, s, p) for s, p in batch], axis=0)
    kb = jnp.stack([ck_ref[s, :, p * LANE:(p + 1) * LANE] for s, p in batch], axis=0)
    vb = jnp.stack([cv_ref[s, :, p * LANE:(p + 1) * LANE] for s, p in batch], axis=0)
    sink = jnp.stack([jnp.concatenate([jnp.full((1, 1), sink_ref[li, p * per_pair + r], F32)
                                       for r in range(per_pair)], axis=0) for s, p in batch], axis=0)
    s_old = _bmm_nt(qb, kb)
    s_new = jnp.sum(qb * knb, axis=-1, keepdims=True)
    mx = jnp.maximum(jnp.maximum(jnp.max(s_old, axis=-1, keepdims=True), s_new), sink)
    e = jnp.exp(s_old - mx)
    e_new = jnp.exp(s_new - mx)
    den = jnp.sum(e, axis=-1, keepdims=True) + e_new + jnp.exp(sink - mx)
    ob = (_bmm(e, vb) + e_new * vnb) / den

    def o_piece(s, h):
        row = ob[s * pairs + h // per_pair][h % per_pair:h % per_pair + 1]
        half = (h // A_GROUP) % 2
        return row[:, half * HEAD_DIM:(half + 1) * HEAD_DIM]

    o_ref[...] = jnp.concatenate(
        [jnp.concatenate([o_piece(s, h) for h in range(A_HEADS)], axis=-1) for s in range(sb)], axis=0)
    nk_ref[:, 0:WINDOW - 1, :] = ck_ref[:, 1:WINDOW, :]
    nv_ref[:, 0:WINDOW - 1, :] = cv_ref[:, 1:WINDOW, :]
    for s in range(sb):
        nk_ref[s, WINDOW - 1:WINDOW, :] = kn[s:s + 1, :]
        nv_ref[s, WINDOW - 1:WINDOW, :] = vn[s:s + 1, :]


def _swa_step(proj, sinks, li, cache_k, cache_v, cos, sin, sb):
    batch, width = proj.shape
    cache_in = pl.BlockSpec((None, sb, WINDOW, A_KV), lambda b: (li, b, 0, 0))
    cache_out = pl.BlockSpec((sb, WINDOW, A_KV), lambda b: (b, 0, 0))
    return pl.pallas_call(
        functools.partial(_swa_step_kernel, li=li),
        grid=(batch // sb,),
        in_specs=[
            pl.BlockSpec(memory_space=pltpu.SMEM),
            pl.BlockSpec((sb, width), lambda b: (b, 0)),
            cache_in, cache_in,
            pl.BlockSpec((1, A_Q), lambda b: (0, 0)),
            pl.BlockSpec((1, A_Q), lambda b: (0, 0)),
        ],
        out_specs=[pl.BlockSpec((sb, A_Q), lambda b: (b, 0)), cache_out, cache_out],
        out_shape=[
            jax.ShapeDtypeStruct((batch, A_Q), F32),
            jax.ShapeDtypeStruct((batch, WINDOW, A_KV), F32),
            jax.ShapeDtypeStruct((batch, WINDOW, A_KV), F32),
        ],
        compiler_params=_params(("parallel",)),
        name="swa_step",
    )(sinks, proj, cache_k, cache_v, cos, sin)


def _softplus(z):
    return jnp.maximum(z, 0.0) + jnp.log(1.0 + jnp.exp(-jnp.abs(z)))


def _rwkv_in_math(pb, prev, mu, w0, wdec, a0, waaa, wgate):
    xm = pb + (prev - pb) * mu
    r = xm[:, 0:B_WIDTH]
    k = xm[:, B_WIDTH:2 * B_WIDTH]
    v = xm[:, 2 * B_WIDTH:3 * B_WIDTH]
    lora = xm[:, 3 * B_WIDTH:3 * B_WIDTH + LORA_IN]
    dg = xm[:, 3 * B_WIDTH + LORA_IN:]
    w_log = -_softplus(-(w0 + _dot(jnp.tanh(lora).astype(BF16), wdec))) - 0.5
    lw = -jnp.exp(w_log)
    a = jax.nn.sigmoid(a0 + _dot(lora.astype(BF16), waaa))
    g = _dot(jax.nn.sigmoid(dg).astype(BF16), wgate)
    return r, k, v, lw, a, g


def _rwkv_in_step_kernel(p_ref, prev_ref, mu_ref, w0_ref, wdec_ref, a0_ref, waaa_ref, wgate_ref, *out_refs):
    vals = _rwkv_in_math(p_ref[:, A_COLS:], prev_ref[...], mu_ref[...], w0_ref[...], wdec_ref[...], a0_ref[...],
                         waaa_ref[...], wgate_ref[...])
    for ref, val in zip(out_refs, vals):
        ref[...] = val


def _rwkv_in_step(proj, prev, wts, li):
    m, width = proj.shape
    out_spec = pl.BlockSpec((m, B_WIDTH), lambda i: (0, 0))
    return pl.pallas_call(
        _rwkv_in_step_kernel,
        grid=(1,),
        in_specs=[pl.BlockSpec((m, width), lambda i: (0, 0)), pl.BlockSpec((m, B_COLS_PAD), lambda i: (0, 0))]
        + [_layer(a, (li,)) for a in wts],
        out_specs=[out_spec] * 6,
        out_shape=[jax.ShapeDtypeStruct((m, B_WIDTH), F32)] * 6,
        compiler_params=_params(("arbitrary",)),
        name="rwkv_in_step",
    )(proj, prev, *wts)


def _bdg(a, b, dn):
    return lax.dot_general(a.astype(BF16), b.astype(BF16), dn, preferred_element_type=F32)


def _bmm(a, b):
    return _bdg(a, b, (((2,), (1,)), ((0,), (0,))))


def _bmm_nt(a, b):
    return _bdg(a, b, (((2,), (2,)), ((0,), (0,))))


def _bmm_tn(a, b):
    return _bdg(a, b, (((1,), (1,)), ((0,), (0,))))


def _unit_lower_inverse(a, eye, same_block):
    d = jnp.where(same_block, a, 0.0)
    e = a - d
    x = eye + d
    dp = d
    p = 1
    while 2 * p < RWKV_INV_BLOCK:
        dp = _bmm(dp, dp)
        x = _bmm(x, eye + dp)
        p *= 2
    nn = _bmm(x, e)
    return _bmm(_bmm(eye + nn, eye + _bmm(nn, nn)), x)


def _head_sums(x, first_half):
    outs = []
    for p in range(HEAD_PAIRS):
        xp = x[:, p * LANE:(p + 1) * LANE]
        s0 = jnp.sum(jnp.where(first_half, xp, 0.0), axis=-1, keepdims=True)
        s1 = jnp.sum(jnp.where(first_half, 0.0, xp), axis=-1, keepdims=True)
        outs.append(jnp.where(first_half, s0, s1))
    return jnp.concatenate(outs, axis=-1)


def _rwkv_scan_kernel(p_ref, halo_ref, mu_ref, w0_ref, wdec_ref, a0_ref, waaa_ref, wgate_ref,
                      kk_ref, ka_ref, rk_ref, gw_ref, gb_ref, y_ref, s_out_ref, st_ref):
    c = pl.program_id(1)

    @pl.when(c == 0)
    def _():
        st_ref[...] = jnp.zeros_like(st_ref)

    pb = p_ref[:, A_COLS:]
    hrow = jnp.where(c == 0, 0.0, halo_ref[7:8, A_COLS:])
    prev = jnp.where(lax.broadcasted_iota(jnp.int32, pb.shape, 0) == 0, hrow, pltpu.roll(pb, 1, 0))
    r, k, v, lw, a, gate = _rwkv_in_math(pb, prev, mu_ref[...], w0_ref[...], wdec_ref[...], a0_ref[...],
                                         waaa_ref[...], wgate_ref[...])
    cs = r.shape[0]
    c2 = 2 * cs
    iota = lambda shape, dim: lax.broadcasted_iota(jnp.int32, shape, dim)

    tril = jnp.where(iota((cs, cs), 1) <= iota((cs, cs), 0), 1.0, 0.0).astype(BF16)
    lw_hi = lw.astype(BF16)
    rem = lw - lw_hi.astype(F32)
    lw_mid = rem.astype(BF16)
    lw_lo = (rem - lw_mid.astype(F32)).astype(BF16)
    lcum = _dot(tril, lw_hi) + (_dot(tril, lw_mid) + _dot(tril, lw_lo))
    p_t = jnp.exp(lcum)
    p_inv = jnp.exp(-lcum)
    p_prev = jnp.exp(lcum - lw)
    p_end = jnp.exp(lcum[cs - 1:cs, :])

    first_half = iota((cs, LANE), 1) < HEAD_DIM
    kk = k * kk_ref[...]
    kk = kk / jnp.maximum(jnp.sqrt(_head_sums(kk * kk, first_half)), 1e-12)
    k2 = k * (1.0 + (a - 1.0) * ka_ref[...])
    al = -kk * p_prev
    be = kk * a * p_inv
    kt = k2 * p_inv
    rt = r * p_t
    be_end = be * p_end
    kt_end = kt * p_end
    bonus = _head_sums(r * k2 * rk_ref[...], first_half) * v

    np_, one = HEAD_PAIRS, (1,)
    own_half = (iota(one + (c2, LANE), 1) // cs) == (iota(one + (c2, LANE), 2) // HEAD_DIM)
    t2 = iota(one + (c2, c2), 1)
    s2 = iota(one + (c2, c2), 2)
    same_head = (t2 // cs) == (s2 // cs)
    strict = same_head & (s2 < t2)
    lower = same_head & (s2 <= t2)
    same_block = (t2 // RWKV_INV_BLOCK) == (s2 // RWKV_INV_BLOCK)
    eye = jnp.where(t2 == s2, 1.0, 0.0)
    kl = iota(one + (LANE, LANE), 1)
    vl = iota(one + (LANE, LANE), 2)
    st_diag = kl == vl
    st_blocks = (kl // HEAD_DIM) == (vl // HEAD_DIM)

    pairs = lambda x: jnp.stack([x[:, p * LANE:(p + 1) * LANE] for p in range(np_)], axis=0)
    dup = lambda x: jnp.concatenate([x, x], axis=1)
    own = lambda x: jnp.where(own_half, dup(pairs(x)), 0.0).astype(BF16)
    al2 = own(al)
    rt2 = own(rt)
    v2 = own(v)
    prod = _bmm_nt(jnp.concatenate([al2, rt2], axis=1),
                   jnp.concatenate([dup(pairs(be)), dup(pairs(kt))], axis=1))
    a_ab = jnp.where(strict, prod[:, :c2, :c2], 0.0)
    a_ak = jnp.where(strict, prod[:, :c2, c2:], 0.0)
    m_rb = jnp.where(lower, prod[:, c2:, :c2], 0.0)
    m_rk = jnp.where(lower, prod[:, c2:, c2:], 0.0)
    tinv = _unit_lower_inverse(a_ab, eye, same_block)
    st = st_ref[...]
    u2 = _bmm(tinv, _bmm(al2, st) + _bmm(a_ak, v2))
    y2 = _bmm(rt2, st) + _bmm(m_rb, u2) + _bmm(m_rk, v2)
    y3 = y2[:, :cs] + y2[:, cs:]
    u = u2[:, :cs] + u2[:, cs:]
    p_col = jnp.sum(jnp.where(st_diag, pairs(p_end), 0.0), axis=2, keepdims=True)
    inc = _bmm_tn(pairs(be_end), u) + _bmm_tn(pairs(kt_end), pairs(v))
    st_ref[...] = st * p_col + jnp.where(st_blocks, inc, 0.0)

    y = jnp.concatenate([y3[p] for p in range(np_)], axis=-1)
    mean = _head_sums(y, first_half) * (1.0 / HEAD_DIM)
    yc = y - mean
    var = _head_sums(yc * yc, first_half) * (1.0 / HEAD_DIM)
    y_ref[...] = (yc * lax.rsqrt(var + GN_EPS) * gw_ref[...] + gb_ref[...] + bonus) * gate

    @pl.when(c == pl.num_programs(1) - 1)
    def _():
        for p in range(np_):
            sp = st_ref[p].T
            s_out_ref[0, 2 * p] = sp[:HEAD_DIM, :HEAD_DIM]
            s_out_ref[0, 2 * p + 1] = sp[HEAD_DIM:, HEAD_DIM:]


def _rwkv_scan(proj, in_wts, head_wts, li, batch, cs):
    m, width = proj.shape
    nc = m // batch // cs
    st_spec = pl.BlockSpec((1, B_HEADS, HEAD_DIM, HEAD_DIM), lambda b, c: (b, 0, 0, 0))
    return pl.pallas_call(
        _rwkv_scan_kernel,
        grid=(batch, nc),
        in_specs=[pl.BlockSpec((cs, width), lambda b, c: (b * nc + c, 0)),
                  pl.BlockSpec((8, width), lambda b, c: (jnp.maximum((b * nc + c) * (cs // 8) - 1, 0), 0))]
        + [_layer(t, (li,)) for t in in_wts + head_wts],
        out_specs=[pl.BlockSpec((cs, B_WIDTH), lambda b, c: (b * nc + c, 0)), st_spec],
        out_shape=[jax.ShapeDtypeStruct((m, B_WIDTH), F32),
                   jax.ShapeDtypeStruct((batch, B_HEADS, HEAD_DIM, HEAD_DIM), F32)],
        scratch_shapes=[pltpu.VMEM((HEAD_PAIRS, LANE, LANE), F32)],
        compiler_params=_params(("parallel", "arbitrary")),
        name="rwkv_scan",
    )(proj, proj, *in_wts, *head_wts)


def _rwkv_step_kernel(r_ref, k_ref, v_ref, lw_ref, a_ref, g_ref, s_ref, kk_ref, ka_ref, rk_ref, gw_ref, gb_ref,
                      y_ref, s_out_ref):
    bb = r_ref.shape[0]
    n = bb * B_HEADS

    def heads(ref):
        x = ref[...]
        return jnp.stack([x[b:b + 1, h * HEAD_DIM:(h + 1) * HEAD_DIM]
                          for b in range(bb) for h in range(B_HEADS)], axis=0)

    per_seq = lambda ref: jnp.concatenate([ref[...]] * bb, axis=0)
    r, k, v, lw, a = heads(r_ref), heads(k_ref), heads(v_ref), heads(lw_ref), heads(a_ref)
    s = s_ref[...].reshape(n, HEAD_DIM, HEAD_DIM)
    kk = k * per_seq(kk_ref)
    kk = kk / jnp.maximum(jnp.sqrt(jnp.sum(kk * kk, axis=-1, keepdims=True)), 1e-12)
    k2 = k * (1.0 + (a - 1.0) * per_seq(ka_ref))
    eye = (lax.broadcasted_iota(jnp.int32, (1, HEAD_DIM, HEAD_DIM), 1)
           == lax.broadcasted_iota(jnp.int32, (1, HEAD_DIM, HEAD_DIM), 2))
    sa = jnp.sum(s * (-kk), axis=-1, keepdims=True)
    v_col = jnp.sum(jnp.where(eye, v, 0.0), axis=-1, keepdims=True)
    s_new = s * jnp.exp(lw) + sa * (kk * a) + v_col * k2
    y_col = jnp.sum(s_new * r, axis=-1, keepdims=True)
    y = jnp.sum(jnp.where(eye, y_col, 0.0), axis=1, keepdims=True)
    mean = jnp.mean(y, axis=-1, keepdims=True)
    yc = y - mean
    var = jnp.mean(yc * yc, axis=-1, keepdims=True)
    out = (yc * lax.rsqrt(var + GN_EPS) * per_seq(gw_ref) + per_seq(gb_ref)
           + jnp.sum(r * k2 * per_seq(rk_ref), axis=-1, keepdims=True) * v)
    s_out_ref[...] = s_new.reshape(bb, B_HEADS, HEAD_DIM, HEAD_DIM)
    y_ref[...] = g_ref[...] * jnp.concatenate(
        [jnp.concatenate([out[b * B_HEADS + h] for h in range(B_HEADS)], axis=-1) for b in range(bb)], axis=0)


def _rwkv_step(r, k, v, lw, a, gate, state, head_wts, li, bb):
    batch = r.shape[0]
    row_spec = pl.BlockSpec((bb, B_WIDTH), lambda b: (b, 0))
    st_shape = (bb, B_HEADS, HEAD_DIM, HEAD_DIM)
    return pl.pallas_call(
        _rwkv_step_kernel,
        grid=(batch // bb,),
        in_specs=[row_spec] * 6 + [pl.BlockSpec((None,) + st_shape, lambda b: (li, b, 0, 0, 0))]
        + [_layer(t, (li,)) for t in head_wts],
        out_specs=[row_spec, pl.BlockSpec(st_shape, lambda b: (b, 0, 0, 0))],
        out_shape=[jax.ShapeDtypeStruct((batch, B_WIDTH), F32),
                   jax.ShapeDtypeStruct((batch, B_HEADS, HEAD_DIM, HEAD_DIM), F32)],
        compiler_params=_params(("parallel",)),
        name="rwkv_step",
    )(r, k, v, lw, a, gate, state, *head_wts)


def _ab_out_kernel(x_ref, ao_ref, yb_ref, wa_ref, wb_ref, g_ref, o_ref):
    mix = _dot(ao_ref[...].astype(BF16), wa_ref[...])
    mix += _dot(yb_ref[...].astype(BF16), wb_ref[...])
    o_ref[...] = mix
    _add_normed_rows(o_ref, x_ref, g_ref[...])


def _ab_out(x, ao, yb, w_out, norm_g, g_row, li, tm):
    m, d = x.shape
    return pl.pallas_call(
        _ab_out_kernel,
        grid=(m // tm,),
        in_specs=[
            pl.BlockSpec((tm, d), lambda i: (i, 0)),
            pl.BlockSpec((tm, A_Q), lambda i: (i, 0)),
            pl.BlockSpec((tm, B_WIDTH), lambda i: (i, 0)),
            pl.BlockSpec((None, A_Q, d), lambda i: (li, 0, 0)),
            pl.BlockSpec((None, B_WIDTH, d), lambda i: (li, 1, 0)),
            _layer(norm_g, (g_row,)),
        ],
        out_specs=pl.BlockSpec((tm, d), lambda i: (i, 0)),
        out_shape=jax.ShapeDtypeStruct((m, d), F32),
        compiler_params=_params(("parallel",)),
        name="ab_out",
    )(x, ao, yb, w_out, w_out, norm_g)


def _rope_tables(pos):
    half = HEAD_DIM // 2
    freqs = ROPE_THETA ** (-jnp.arange(half, dtype=F32) / half)
    ang = pos.astype(F32)[:, None] * freqs[None, :]
    cos = jnp.cos(ang)
    sin = jnp.sin(ang)
    return (jnp.tile(jnp.concatenate([cos, cos], axis=-1), (1, A_HEADS)),
            jnp.tile(jnp.concatenate([-sin, sin], axis=-1), (1, A_HEADS)))


def _trunk(xp3, xs3, past, w):
    bp, seq, d = xp3.shape
    bs = xs3.shape[0]
    mp = bp * seq
    xp = xp3.reshape(mp, d)
    xs = xs3.reshape(bs, d)
    tp = _tiles(mp)
    ts = _tiles(bs)
    assert all(seq % n == 0 for n in (tp["mix_m"], RWKV_CHUNK, WINDOW)) and mp % tp["ffn_m"] == 0
    cos_p, sin_p = _rope_tables(jnp.arange(seq, dtype=jnp.int32))
    cos_s, sin_s = _rope_tables(PAST_LEN + jnp.arange(1, dtype=jnp.int32))
    cache_k, cache_v, state, shift, conv_buf = past
    ng = w["norm_g"]
    outs = {key: [] for key in ("pk", "pv", "ps", "psh", "pc", "sk", "sv", "ss", "ssh", "sc")}
    kv_heads = lambda t, b: t.reshape(b, WINDOW, A_KV_HEADS, HEAD_DIM)
    for l in range(w["depth"]):
        i = l // 2
        grow = lambda j: l * 6 + j
        xp, xs = _ffn(xp, xs, ng, (grow(0), grow(1)), w["ffn_w_gu"], w["ffn_w_down"], (l, 0),
                      tp["ffn_m"], tp["ffn_f"])
        if l % 2 == 0:
            proj = _norm_mm(xp, ng, grow(2), w["ab_w_in"], i, tp["proj_m"], 1280)
            ao, nk, nv = _swa_seq(proj, w["attn_sinks"], i, cos_p, sin_p, bp, seq)
            yb, st = _rwkv_scan(proj, w["rwkv_in"], w["rwkv_rows"], i, bp, RWKV_CHUNK)
            xp = _ab_out(xp, ao, yb, w["ab_w_out"], ng, grow(3), i, tp["mix_m"])
            outs["pk"].append(kv_heads(nk, bp))
            outs["pv"].append(kv_heads(nv, bp))
            outs["ps"].append(st)
            outs["psh"].append(proj.reshape(bp, seq, AB_COLS_PAD)[:, -1, A_COLS:AB_COLS])

            proj = _norm_mm(xs, ng, grow(2), w["ab_w_in"], i, ts["proj_m"], 1280)
            ao, nk, nv = _swa_step(proj, w["attn_sinks"], i, cache_k, cache_v, cos_s, sin_s, min(8, bs))
            prev = jnp.pad(shift[i], ((0, 0), (0, B_COLS_PAD - B_COLS)))
            r, k, v, lw, a, gt = _rwkv_in_step(proj, prev, w["rwkv_in"], i)
            yb, st = _rwkv_step(r, k, v, lw, a, gt, state, w["rwkv_heads"], i, min(8, bs))
            xs = _ab_out(xs, ao, yb, w["ab_w_out"], ng, grow(3), i, ts["mix_m"])
            outs["sk"].append(kv_heads(nk, bs))
            outs["sv"].append(kv_heads(nv, bs))
            outs["ss"].append(st)
            outs["ssh"].append(proj[:, A_COLS:AB_COLS])
        else:
            bg, u = _conv_in(xp, ng, grow(2), w["conv_w_in"], i, tp["proj_m"], 512)
            xp = _conv_out_seq(xp, bg, u, w["conv_w"], w["conv_w_out"], ng, grow(3), i, tp["mix_m"], seq)
            outs["pc"].append(u.reshape(bp, seq, -1)[:, -(CONV_W - 1):])

            bg, u = _conv_in(xs, ng, grow(2), w["conv_w_in"], i, ts["proj_m"], 512)
            buf = conv_buf[i]
            xs = _conv_out_step(xs, bg, u, buf[:, 1], buf[:, 0], w["conv_w"], w["conv_w_out"], ng, grow(3), i)
            outs["sc"].append(jnp.stack([buf[:, 1], u], axis=1))
        xp, xs = _ffn(xp, xs, ng, (grow(4), grow(5)), w["ffn_w_gu"], w["ffn_w_down"], (l, 1),
                      tp["ffn_m"], tp["ffn_f"])
    stacked = {key: jnp.stack(val) for key, val in outs.items()}
    return xp.reshape(bp, seq, d), xs.reshape(bs, 1, d), stacked


def _prep_weights(norm_g, ffn_w_gu, ffn_w_down, ab_w_in, ab_w_out, attn_sinks, rwkv_mu, rwkv_w0, rwkv_w_decay,
                  rwkv_a0, rwkv_w_aaa, rwkv_w_gate, rwkv_k_k, rwkv_k_a, rwkv_r_k, rwkv_gn_w, rwkv_gn_b,
                  conv_w_in, conv_w, conv_w_out):
    depth, n_norm, d = norm_g.shape
    padc = lambda t, n: jnp.pad(t, [(0, 0)] * (t.ndim - 1) + [(0, n - t.shape[-1])])
    row = lambda t: t[:, None, :]
    wdec = jnp.pad(rwkv_w_decay, ((0, 0), (0, D_AAA), (0, 0))).astype(BF16)
    waaa = jnp.pad(rwkv_w_aaa, ((0, 0), (D_DECAY, 0), (0, 0))).astype(BF16)
    wgate = jnp.pad(rwkv_w_gate, ((0, 0), (0, GATE_PAD - D_GATE), (0, 0))).astype(BF16)
    return {
        "depth": depth,
        "norm_g": norm_g.reshape(depth * n_norm, 1, d),
        "ffn_w_gu": ffn_w_gu,
        "ffn_w_down": ffn_w_down,
        "ab_w_in": padc(ab_w_in, AB_COLS_PAD).astype(BF16),
        "ab_w_out": ab_w_out.astype(BF16),
        "attn_sinks": attn_sinks,
        "rwkv_in": (row(padc(rwkv_mu, B_COLS_PAD)), row(rwkv_w0), wdec, row(rwkv_a0), waaa, wgate),
        "rwkv_rows": tuple(row(t) for t in (rwkv_k_k, rwkv_k_a, rwkv_r_k, rwkv_gn_w, rwkv_gn_b)),
        "rwkv_heads": tuple(t.reshape(-1, B_HEADS, 1, HEAD_DIM)
                            for t in (rwkv_k_k, rwkv_k_a, rwkv_r_k, rwkv_gn_w, rwkv_gn_b)),
        "conv_w_in": conv_w_in,
        "conv_w": conv_w,
        "conv_w_out": conv_w_out.astype(BF16),
    }


def kernel(x_prompt, x_sample, cache_swa_k, cache_swa_v, state_rwkv, state_rwkv_shift, state_conv, norm_g, ffn_w_gu, ffn_w_down, ab_w_in, ab_w_out, attn_sinks, rwkv_mu, rwkv_w0, rwkv_w_decay, rwkv_a0, rwkv_w_aaa, rwkv_w_gate, rwkv_k_k, rwkv_k_a, rwkv_r_k, rwkv_gn_w, rwkv_gn_b, conv_w_in, conv_w, conv_w_out):
    w = _prep_weights(norm_g, ffn_w_gu, ffn_w_down, ab_w_in, ab_w_out, attn_sinks, rwkv_mu, rwkv_w0, rwkv_w_decay,
                      rwkv_a0, rwkv_w_aaa, rwkv_w_gate, rwkv_k_k, rwkv_k_a, rwkv_r_k, rwkv_gn_w, rwkv_gn_b,
                      conv_w_in, conv_w, conv_w_out)
    n_ab, dec_batch = cache_swa_k.shape[:2]
    past = (cache_swa_k.reshape(n_ab, dec_batch, WINDOW, A_KV), cache_swa_v.reshape(n_ab, dec_batch, WINDOW, A_KV),
            state_rwkv, state_rwkv_shift, state_conv)
    y_prompt, y_sample, o = _trunk(x_prompt, x_sample, past, w)
    return (y_prompt, y_sample, o["pk"], o["pv"], o["ps"], o["psh"], o["pc"],
            o["sk"], o["sv"], o["ss"], o["ssh"], o["sc"])
```

```python
import functools

import jax
import jax.numpy as jnp
from jax import lax
from jax.experimental import pallas as pl
from jax.experimental.pallas import tpu as pltpu

F32 = jnp.float32
BF16 = jnp.bfloat16

HEAD_DIM = 64
A_HEADS = 16
A_KV_HEADS = 4
A_GROUP = A_HEADS // A_KV_HEADS
WINDOW = 128
ROPE_THETA = 10000.0
PAST_LEN = 16384
A_Q = A_HEADS * HEAD_DIM
A_KV = A_KV_HEADS * HEAD_DIM
A_COLS = A_Q + 2 * A_KV
B_HEADS = 16
B_WIDTH = B_HEADS * HEAD_DIM
D_DECAY = 64
D_AAA = 64
D_GATE = 160
B_COLS = 3 * B_WIDTH + D_DECAY + D_AAA + D_GATE
GN_EPS = 64e-5
AB_COLS = A_COLS + B_COLS
CONV_W = 3
NORM_EPS = 1e-6

LANE = 128
HEAD_PAIRS = B_WIDTH // LANE
AB_COLS_PAD = 5120
B_COLS_PAD = AB_COLS_PAD - A_COLS
LORA_IN = D_DECAY + D_AAA
GATE_PAD = AB_COLS_PAD - A_COLS - 3 * B_WIDTH - LORA_IN
NORM_ROWS = 16
NORM_UNROLL = 64
RWKV_CHUNK = 64
RWKV_INV_BLOCK = 16
VMEM_LIMIT = 56 * 1024 * 1024
WIDE_VMEM_LIMIT = 60 * 1024 * 1024


def _tiles(m):
    return dict(ffn_m=min(1024, m), ffn_f=512, proj_m=min(1024, m), mix_m=min(512, m))


def _params(sem, vmem_limit=VMEM_LIMIT):
    return pltpu.CompilerParams(dimension_semantics=sem, vmem_limit_bytes=vmem_limit)


def _layer(arr, idx):
    idx = tuple(idx)
    rest = arr.shape[len(idx):]
    return pl.BlockSpec((None,) * len(idx) + rest, lambda *_: idx + (0,) * len(rest))


def _rms(x, g):
    return x * lax.rsqrt(jnp.mean(x * x, axis=-1, keepdims=True) + NORM_EPS) * g


def _dot(a, b):
    return jnp.dot(a, b, preferred_element_type=F32)


def _for_row_chunks(n_rows, body):
    size = min(NORM_ROWS, n_rows)
    assert n_rows % size == 0

    def step(c, carry):
        body(pl.ds(pl.multiple_of(c * size, size), size))
        return carry

    n = n_rows // size
    lax.fori_loop(0, n, step, 0, unroll=min(NORM_UNROLL, n))


def _norm_rows_into(h_ref, x_ref, g):
    def chunk(rows):
        h_ref[rows, :] = _rms(x_ref[rows, :], g).astype(BF16)

    _for_row_chunks(x_ref.shape[0], chunk)


def _add_normed_rows(o_ref, x_ref, g, scale=1.0):
    def chunk(rows):
        normed = _rms(o_ref[rows, :], g)
        o_ref[rows, :] = x_ref[rows, :] + (normed if scale == 1.0 else scale * normed)

    _for_row_chunks(x_ref.shape[0], chunk)


def _ffn_kernel(x_ref, xs_ref, g0_ref, g1_ref, wg_ref, wu_ref, wd_ref, o_ref, os_ref, h_ref, hs_ref):
    i = pl.program_id(0)
    f = pl.program_id(1)
    g0 = g0_ref[...]
    g1 = g1_ref[...]
    cast = lambda w_ref: w_ref[...].astype(BF16)

    def group(x_ref, o_ref, h_ref):
        def hidden_tile(accumulate):
            h = h_ref[...]
            gate = _dot(h, cast(wg_ref))
            act = (gate * jax.nn.sigmoid(gate) * _dot(h, cast(wu_ref))).astype(BF16)
            if accumulate:
                o_ref[...] += _dot(act, cast(wd_ref))
            else:
                o_ref[...] = _dot(act, cast(wd_ref))

        @pl.when(f == 0)
        def _():
            _norm_rows_into(h_ref, x_ref, g0)
            hidden_tile(False)

        @pl.when(f > 0)
        def _():
            hidden_tile(True)

        @pl.when(f == pl.num_programs(1) - 1)
        def _():
            _add_normed_rows(o_ref, x_ref, g1, 0.5)

    group(x_ref, o_ref, h_ref)

    @pl.when(i == 0)
    def _():
        group(xs_ref, os_ref, hs_ref)


def _ffn(x, xs, norm_g, g_rows, w_gu, w_down, lj, tm, tf):
    m, d = x.shape
    ms = xs.shape[0]
    l, j = lj
    nf = w_down.shape[2] // tf
    once = pl.Buffered(1)
    return pl.pallas_call(
        _ffn_kernel,
        grid=(m // tm, nf),
        in_specs=[
            pl.BlockSpec((tm, d), lambda i, f: (i, 0), pipeline_mode=once),
            pl.BlockSpec((ms, d), lambda i, f: (0, 0), pipeline_mode=once),
            _layer(norm_g, (g_rows[0],)),
            _layer(norm_g, (g_rows[1],)),
            pl.BlockSpec((None, None, d, tf), lambda i, f: (l, j, 0, f)),
            pl.BlockSpec((None, None, d, tf), lambda i, f: (l, j, 0, nf + f)),
            pl.BlockSpec((None, None, tf, d), lambda i, f: (l, j, f, 0)),
        ],
        out_specs=[pl.BlockSpec((tm, d), lambda i, f: (i, 0), pipeline_mode=once),
                   pl.BlockSpec((ms, d), lambda i, f: (0, 0), pipeline_mode=once)],
        out_shape=[jax.ShapeDtypeStruct((m, d), F32), jax.ShapeDtypeStruct((ms, d), F32)],
        scratch_shapes=[pltpu.VMEM((tm, d), BF16), pltpu.VMEM((ms, d), BF16)],
        compiler_params=_params(("arbitrary", "arbitrary"), WIDE_VMEM_LIMIT),
        name="ffn",
    )(x, xs, norm_g, norm_g, w_gu, w_gu, w_down)


def _norm_mm_kernel(x_ref, g_ref, w_ref, o_ref, h_ref):
    @pl.when(pl.program_id(1) == 0)
    def _():
        _norm_rows_into(h_ref, x_ref, g_ref[...])

    o_ref[...] = _dot(h_ref[...], w_ref[...])


def _norm_mm(x, norm_g, g_row, w, li, tm, tn):
    m, d = x.shape
    n = w.shape[-1]
    return pl.pallas_call(
        _norm_mm_kernel,
        grid=(m // tm, n // tn),
        in_specs=[
            pl.BlockSpec((tm, d), lambda i, j: (i, 0)),
            _layer(norm_g, (g_row,)),
            pl.BlockSpec((None, d, tn), lambda i, j: (li, 0, j)),
        ],
        out_specs=pl.BlockSpec((tm, tn), lambda i, j: (i, j)),
        out_shape=jax.ShapeDtypeStruct((m, n), F32),
        scratch_shapes=[pltpu.VMEM((tm, d), BF16)],
        compiler_params=_params(("parallel", "arbitrary")),
        name="ab_in",
    )(x, norm_g, w)


def _conv_in_kernel(x_ref, g_ref, wb_ref, wc_ref, wh_ref, bg_ref, u_ref, h_ref):
    @pl.when(pl.program_id(1) == 0)
    def _():
        _norm_rows_into(h_ref, x_ref, g_ref[...])

    h = h_ref[...]
    bg_ref[...] = _dot(h, wb_ref[...].astype(BF16))
    u_ref[...] = _dot(h, wc_ref[...].astype(BF16)) * _dot(h, wh_ref[...].astype(BF16))


def _conv_in(x, norm_g, g_row, w_in, li, tm, tn):
    m, d = x.shape
    dc = w_in.shape[-1] // 3
    nj = dc // tn
    return pl.pallas_call(
        _conv_in_kernel,
        grid=(m // tm, nj),
        in_specs=[
            pl.BlockSpec((tm, d), lambda i, j: (i, 0)),
            _layer(norm_g, (g_row,)),
            pl.BlockSpec((None, d, tn), lambda i, j: (li, 0, j)),
            pl.BlockSpec((None, d, tn), lambda i, j: (li, 0, nj + j)),
            pl.BlockSpec((None, d, tn), lambda i, j: (li, 0, 2 * nj + j)),
        ],
        out_specs=[pl.BlockSpec((tm, tn), lambda i, j: (i, j)),
                   pl.BlockSpec((tm, tn), lambda i, j: (i, j))],
        out_shape=[jax.ShapeDtypeStruct((m, dc), F32), jax.ShapeDtypeStruct((m, dc), F32)],
        scratch_shapes=[pltpu.VMEM((tm, d), BF16)],
        compiler_params=_params(("parallel", "arbitrary"), WIDE_VMEM_LIMIT),
        name="conv_in",
    )(x, norm_g, w_in, w_in, w_in)


def _conv_taps(u, p1, p2, cw):
    return p2 * cw[0:1, :] + p1 * cw[1:2, :] + u * cw[2:3, :]


def _conv_out_seq_kernel(x_ref, bg_ref, u_ref, halo_ref, cw_ref, w_ref, g_ref, o_ref, *, tiles_per_seq):
    u = u_ref[...]
    first = (pl.program_id(0) % tiles_per_seq) == 0
    h1 = jnp.where(first, 0.0, halo_ref[7:8, :])
    h2 = jnp.where(first, 0.0, halo_ref[6:7, :])
    row = lax.broadcasted_iota(jnp.int32, u.shape, 0)
    p1 = jnp.where(row == 0, h1, pltpu.roll(u, 1, 0))
    p2 = jnp.where(row == 0, h2, jnp.where(row == 1, h1, pltpu.roll(u, 2, 0)))
    z = (bg_ref[...] * _conv_taps(u, p1, p2, cw_ref[...])).astype(BF16)
    o_ref[...] = _dot(z, w_ref[...])
    _add_normed_rows(o_ref, x_ref, g_ref[...])


def _conv_out_step_kernel(x_ref, bg_ref, u_ref, p1_ref, p2_ref, cw_ref, w_ref, g_ref, o_ref):
    z = (bg_ref[...] * _conv_taps(u_ref[...], p1_ref[...], p2_ref[...], cw_ref[...])).astype(BF16)
    o_ref[...] = _dot(z, w_ref[...])
    _add_normed_rows(o_ref, x_ref, g_ref[...])


def _conv_out_seq(x, bg, u, conv_w, w_out, norm_g, g_row, li, tm, seq):
    m, d = x.shape
    dc = u.shape[1]
    hb = tm // 8
    row_spec = pl.BlockSpec((tm, dc), lambda i: (i, 0))
    return pl.pallas_call(
        functools.partial(_conv_out_seq_kernel, tiles_per_seq=seq // tm),
        grid=(m // tm,),
        in_specs=[
            pl.BlockSpec((tm, d), lambda i: (i, 0)),
            row_spec, row_spec,
            pl.BlockSpec((8, dc), lambda i: (jnp.maximum(i * hb - 1, 0), 0)),
            _layer(conv_w, (li,)), _layer(w_out, (li,)), _layer(norm_g, (g_row,)),
        ],
        out_specs=pl.BlockSpec((tm, d), lambda i: (i, 0)),
        out_shape=jax.ShapeDtypeStruct((m, d), F32),
        compiler_params=_params(("parallel",)),
        name="conv_out",
    )(x, bg, u, u, conv_w, w_out, norm_g)


def _conv_out_step(x, bg, u, p1, p2, conv_w, w_out, norm_g, g_row, li):
    m, d = x.shape
    dc = u.shape[1]
    row_spec = pl.BlockSpec((m, dc), lambda i: (0, 0))
    return pl.pallas_call(
        _conv_out_step_kernel,
        grid=(1,),
        in_specs=[
            pl.BlockSpec((m, d), lambda i: (0, 0)),
            row_spec, row_spec, row_spec, row_spec,
            _layer(conv_w, (li,)), _layer(w_out, (li,)), _layer(norm_g, (g_row,)),
        ],
        out_specs=pl.BlockSpec((m, d), lambda i: (0, 0)),
        out_shape=jax.ShapeDtypeStruct((m, d), F32),
        compiler_params=_params(("arbitrary",)),
        name="conv_out_step",
    )(x, bg, u, p1, p2, conv_w, w_out, norm_g)


def _rope(x, cos, sin):
    width = x.shape[-1]
    half = HEAD_DIM // 2
    lane = lax.broadcasted_iota(jnp.int32, x.shape, x.ndim - 1)
    swapped = jnp.where((lane % HEAD_DIM) < half,
                        pltpu.roll(x, width - half, x.ndim - 1),
                        pltpu.roll(x, half, x.ndim - 1))
    return x * cos + swapped * sin


def _swa_seq_kernel(sink_ref, q_ref, kc_ref, kp_ref, vc_ref, vp_ref, cq_ref, sq_ref, cp_ref, sp_ref,
                    o_ref, nk_ref, nv_ref, *, li):
    n = pl.program_id(1)
    q = _rope(q_ref[...], cq_ref[...], sq_ref[...])
    kc = _rope(kc_ref[...], cq_ref[:, :A_KV], sq_ref[:, :A_KV])
    kp = _rope(kp_ref[...], cp_ref[...], sp_ref[...])
    kw = jnp.concatenate([kp, kc], axis=0).astype(BF16)
    vw = jnp.concatenate([vp_ref[...], vc_ref[...]], axis=0).astype(BF16)
    rows = A_GROUP * WINDOW
    head = lambda x, h: x[:, h * HEAD_DIM:(h + 1) * HEAD_DIM]
    groups = lambda f: jnp.stack([f(g) for g in range(A_KV_HEADS)], axis=0)
    qg = groups(lambda g: jnp.concatenate([head(q, g * A_GROUP + j) for j in range(A_GROUP)], axis=0))
    sink = groups(lambda g: jnp.concatenate(
        [jnp.full((WINDOW, 1), sink_ref[li, g * A_GROUP + j], F32) for j in range(A_GROUP)], axis=0))
    qi = lax.broadcasted_iota(jnp.int32, (1, rows, 2 * WINDOW), 1) % WINDOW
    kj = lax.broadcasted_iota(jnp.int32, (1, rows, 2 * WINDOW), 2)
    diff = WINDOW + qi - kj
    lo = jnp.where(n > 0, 0, WINDOW)
    valid = (diff >= 0) & (diff <= WINDOW) & (kj >= lo)
    s = _bmm_nt(qg, groups(lambda g: head(kw, g)))
    s = jnp.where(valid, s * (HEAD_DIM ** -0.5), -jnp.inf)
    mx = jnp.maximum(jnp.max(s, axis=-1, keepdims=True), sink)
    e = jnp.exp(s - mx)
    p = e / (jnp.sum(e, axis=-1, keepdims=True) + jnp.exp(sink - mx))
    og = _bmm(p, groups(lambda g: head(vw, g)))
    o_ref[...] = jnp.concatenate(
        [og[h // A_GROUP][(h % A_GROUP) * WINDOW:(h % A_GROUP + 1) * WINDOW] for h in range(A_HEADS)], axis=-1)

    @pl.when(n == pl.num_programs(1) - 1)
    def _():
        nk_ref[...] = kc
        nv_ref[...] = vc_ref[...]


def _swa_seq(proj, sinks, li, cos, sin, batch, seq):
    nb = seq // WINDOW
    kcol = A_Q // A_KV
    vcol = kcol + 1
    cur = lambda b, n: b * nb + n
    prev = lambda b, n: b * nb + jnp.maximum(n - 1, 0)
    return pl.pallas_call(
        functools.partial(_swa_seq_kernel, li=li),
        grid=(batch, nb),
        in_specs=[
            pl.BlockSpec(memory_space=pltpu.SMEM),
            pl.BlockSpec((WINDOW, A_Q), lambda b, n: (cur(b, n), 0)),
            pl.BlockSpec((WINDOW, A_KV), lambda b, n: (cur(b, n), kcol)),
            pl.BlockSpec((WINDOW, A_KV), lambda b, n: (prev(b, n), kcol)),
            pl.BlockSpec((WINDOW, A_KV), lambda b, n: (cur(b, n), vcol)),
            pl.BlockSpec((WINDOW, A_KV), lambda b, n: (prev(b, n), vcol)),
            pl.BlockSpec((WINDOW, A_Q), lambda b, n: (n, 0)),
            pl.BlockSpec((WINDOW, A_Q), lambda b, n: (n, 0)),
            pl.BlockSpec((WINDOW, A_KV), lambda b, n: (jnp.maximum(n - 1, 0), 0)),
            pl.BlockSpec((WINDOW, A_KV), lambda b, n: (jnp.maximum(n - 1, 0), 0)),
        ],
        out_specs=[
            pl.BlockSpec((WINDOW, A_Q), lambda b, n: (cur(b, n), 0)),
            pl.BlockSpec((WINDOW, A_KV), lambda b, n: (b, 0)),
            pl.BlockSpec((WINDOW, A_KV), lambda b, n: (b, 0)),
        ],
        out_shape=[
            jax.ShapeDtypeStruct((batch * seq, A_Q), F32),
            jax.ShapeDtypeStruct((batch * WINDOW, A_KV), F32),
            jax.ShapeDtypeStruct((batch * WINDOW, A_KV), F32),
        ],
        compiler_params=_params(("parallel", "arbitrary")),
        name="swa_seq",
    )(sinks, proj, proj, proj, proj, proj, cos, sin, cos, sin)


def _swa_step_kernel(sink_ref, p_ref, ck_ref, cv_ref, cos_ref, sin_ref, o_ref, nk_ref, nv_ref, *, li):
    sb = p_ref.shape[0]
    pairs = A_KV // LANE
    per_pair = A_HEADS // pairs
    q = _rope(p_ref[:, 0:A_Q], cos_ref[...], sin_ref[...]) * (HEAD_DIM ** -0.5)
    kn = _rope(p_ref[:, A_Q:A_Q + A_KV], cos_ref[:, :A_KV], sin_ref[:, :A_KV])
    vn = p_ref[:, A_Q + A_KV:A_COLS]
    zero = jnp.zeros((1, HEAD_DIM), F32)

    def q_row(s, h):
        piece = q[s:s + 1, h * HEAD_DIM:(h + 1) * HEAD_DIM]
        return jnp.concatenate([piece, zero] if (h // A_GROUP) % 2 == 0 else [zero, piece], axis=-1)

    batch = [(s, p) for s in range(sb) for p in range(pairs)]
    qb = jnp.stack([jnp.concatenate([q_row(s, p * per_pair + r) for r in range(per_pair)], axis=0)
                    for s, p in batch], axis=0)
    lanes = lambda x, s, p: x[s:s + 1, p * LANE:(p + 1) * LANE]
    knb = jnp.stack([lanes(kn, s, p) for s, p in batch], axis=0)
    vnb = jnp.stack([lanes(vn, s, p) for s, p in batch], axis=0)
    kb = jnp.stack([ck_ref[s, :, p * LANE:(p + 1) * LANE] for s, p in batch], axis=0)
    vb = jnp.stack([cv_ref[s, :, p * LANE:(p + 1) * LANE] for s, p in batch], axis=0)
    sink = jnp.stack([jnp.concatenate([jnp.full((1, 1), sink_ref[li, p * per_pair + r], F32)
                                       for r in range(per_pair)], axis=0) for s, p in batch], axis=0)
    s_old = _bmm_nt(qb, kb)
    s_new = jnp.sum(qb * knb, axis=-1, keepdims=True)
    mx = jnp.maximum(jnp.maximum(jnp.max(s_old, axis=-1, keepdims=True), s_new), sink)
    e = jnp.exp(s_old - mx)
    e_new = jnp.exp(s_new - mx)
    den = jnp.sum(e, axis=-1, keepdims=True) + e_new + jnp.exp(sink - mx)
    ob = (_bmm(e, vb) + e_new * vnb) / den

    def o_piece(s, h):
        row = ob[s * pairs + h // per_pair][h % per_pair:h % per_pair + 1]
        half = (h // A_GROUP) % 2
        return row[:, half * HEAD_DIM:(half + 1) * HEAD_DIM]

    o_ref[...] = jnp.concatenate(
        [jnp.concatenate([o_piece(s, h) for h in range(A_HEADS)], axis=-1) for s in range(sb)], axis=0)
    nk_ref[:, 0:WINDOW - 1, :] = ck_ref[:, 1:WINDOW, :]
    nv_ref[:, 0:WINDOW - 1, :] = cv_ref[:, 1:WINDOW, :]
    for s in range(sb):
        nk_ref[s, WINDOW - 1:WINDOW, :] = kn[s:s + 1, :]
        nv_ref[s, WINDOW - 1:WINDOW, :] = vn[s:s + 1, :]


def _swa_step(proj, sinks, li, cache_k, cache_v, cos, sin, sb):
    batch, width = proj.shape
    cache_in = pl.BlockSpec((None, sb, WINDOW, A_KV), lambda b: (li, b, 0, 0))
    cache_out = pl.BlockSpec((sb, WINDOW, A_KV), lambda b: (b, 0, 0))
    return pl.pallas_call(
        functools.partial(_swa_step_kernel, li=li),
        grid=(batch // sb,),
        in_specs=[
            pl.BlockSpec(memory_space=pltpu.SMEM),
            pl.BlockSpec((sb, width), lambda b: (b, 0)),
            cache_in, cache_in,
            pl.BlockSpec((1, A_Q), lambda b: (0, 0)),
            pl.BlockSpec((1, A_Q), lambda b: (0, 0)),
        ],
        out_specs=[pl.BlockSpec((sb, A_Q), lambda b: (b, 0)), cache_out, cache_out],
        out_shape=[
            jax.ShapeDtypeStruct((batch, A_Q), F32),
            jax.ShapeDtypeStruct((batch, WINDOW, A_KV), F32),
            jax.ShapeDtypeStruct((batch, WINDOW, A_KV), F32),
        ],
        compiler_params=_params(("parallel",)),
        name="swa_step",
    )(sinks, proj, cache_k, cache_v, cos, sin)


def _softplus(z):
    return jnp.maximum(z, 0.0) + jnp.log(1.0 + jnp.exp(-jnp.abs(z)))


def _rwkv_in_math(pb, prev, mu, w0, wdec, a0, waaa, wgate):
    xm = pb + (prev - pb) * mu
    r = xm[:, 0:B_WIDTH]
    k = xm[:, B_WIDTH:2 * B_WIDTH]
    v = xm[:, 2 * B_WIDTH:3 * B_WIDTH]
    lora = xm[:, 3 * B_WIDTH:3 * B_WIDTH + LORA_IN]
    dg = xm[:, 3 * B_WIDTH + LORA_IN:]
    w_log = -_softplus(-(w0 + _dot(jnp.tanh(lora).astype(BF16), wdec))) - 0.5
    lw = -jnp.exp(w_log)
    a = jax.nn.sigmoid(a0 + _dot(lora.astype(BF16), waaa))
    g = _dot(jax.nn.sigmoid(dg).astype(BF16), wgate)
    return r, k, v, lw, a, g


def _rwkv_in_step_kernel(p_ref, prev_ref, mu_ref, w0_ref, wdec_ref, a0_ref, waaa_ref, wgate_ref, *out_refs):
    vals = _rwkv_in_math(p_ref[:, A_COLS:], prev_ref[...], mu_ref[...], w0_ref[...], wdec_ref[...], a0_ref[...],
                         waaa_ref[...], wgate_ref[...])
    for ref, val in zip(out_refs, vals):
        ref[...] = val


def _rwkv_in_step(proj, prev, wts, li):
    m, width = proj.shape
    out_spec = pl.BlockSpec((m, B_WIDTH), lambda i: (0, 0))
    return pl.pallas_call(
        _rwkv_in_step_kernel,
        grid=(1,),
        in_specs=[pl.BlockSpec((m, width), lambda i: (0, 0)), pl.BlockSpec((m, B_COLS_PAD), lambda i: (0, 0))]
        + [_layer(a, (li,)) for a in wts],
        out_specs=[out_spec] * 6,
        out_shape=[jax.ShapeDtypeStruct((m, B_WIDTH), F32)] * 6,
        compiler_params=_params(("arbitrary",)),
        name="rwkv_in_step",
    )(proj, prev, *wts)


def _bdg(a, b, dn):
    return lax.dot_general(a.astype(BF16), b.astype(BF16), dn, preferred_element_type=F32)


def _bmm(a, b):
    return _bdg(a, b, (((2,), (1,)), ((0,), (0,))))


def _bmm_nt(a, b):
    return _bdg(a, b, (((2,), (2,)), ((0,), (0,))))


def _bmm_tn(a, b):
    return _bdg(a, b, (((1,), (1,)), ((0,), (0,))))


def _unit_lower_inverse(a, eye, same_block):
    d = jnp.where(same_block, a, 0.0)
    e = a - d
    x = eye + d
    dp = d
    p = 1
    while 2 * p < RWKV_INV_BLOCK:
        dp = _bmm(dp, dp)
        x = _bmm(x, eye + dp)
        p *= 2
    nn = _bmm(x, e)
    return _bmm(_bmm(eye + nn, eye + _bmm(nn, nn)), x)


def _head_sums(x, first_half):
    outs = []
    for p in range(HEAD_PAIRS):
        xp = x[:, p * LANE:(p + 1) * LANE]
        s0 = jnp.sum(jnp.where(first_half, xp, 0.0), axis=-1, keepdims=True)
        s1 = jnp.sum(jnp.where(first_half, 0.0, xp), axis=-1, keepdims=True)
        outs.append(jnp.where(first_half, s0, s1))
    return jnp.concatenate(outs, axis=-1)


def _rwkv_scan_kernel(p_ref, halo_ref, mu_ref, w0_ref, wdec_ref, a0_ref, waaa_ref, wgate_ref,
                      kk_ref, ka_ref, rk_ref, gw_ref, gb_ref, y_ref, s_out_ref, st_ref):
    c = pl.program_id(1)

    @pl.when(c == 0)
    def _():
        st_ref[...] = jnp.zeros_like(st_ref)

    pb = p_ref[:, A_COLS:]
    hrow = jnp.where(c == 0, 0.0, halo_ref[7:8, A_COLS:])
    prev = jnp.where(lax.broadcasted_iota(jnp.int32, pb.shape, 0) == 0, hrow, pltpu.roll(pb, 1, 0))
    r, k, v, lw, a, gate = _rwkv_in_math(pb, prev, mu_ref[...], w0_ref[...], wdec_ref[...], a0_ref[...],
                                         waaa_ref[...], wgate_ref[...])
    cs = r.shape[0]
    c2 = 2 * cs
    iota = lambda shape, dim: lax.broadcasted_iota(jnp.int32, shape, dim)

    tril = jnp.where(iota((cs, cs), 1) <= iota((cs, cs), 0), 1.0, 0.0).astype(BF16)
    lw_hi = lw.astype(BF16)
    rem = lw - lw_hi.astype(F32)
    lw_mid = rem.astype(BF16)
    lw_lo = (rem - lw_mid.astype(F32)).astype(BF16)
    lcum = _dot(tril, lw_hi) + (_dot(tril, lw_mid) + _dot(tril, lw_lo))
    p_t = jnp.exp(lcum)
    p_inv = jnp.exp(-lcum)
    p_prev = jnp.exp(lcum - lw)
    p_end = jnp.exp(lcum[cs - 1:cs, :])

    first_half = iota((cs, LANE), 1) < HEAD_DIM
    kk = k * kk_ref[...]
    kk = kk / jnp.maximum(jnp.sqrt(_head_sums(kk * kk, first_half)), 1e-12)
    k2 = k * (1.0 + (a - 1.0) * ka_ref[...])
    al = -kk * p_prev
    be = kk * a * p_inv
    kt = k2 * p_inv
    rt = r * p_t
    be_end = be * p_end
    kt_end = kt * p_end
    bonus = _head_sums(r * k2 * rk_ref[...], first_half) * v

    np_, one = HEAD_PAIRS, (1,)
    own_half = (iota(one + (c2, LANE), 1) // cs) == (iota(one + (c2, LANE), 2) // HEAD_DIM)
    t2 = iota(one + (c2, c2), 1)
    s2 = iota(one + (c2, c2), 2)
    same_head = (t2 // cs) == (s2 // cs)
    strict = same_head & (s2 < t2)
    lower = same_head & (s2 <= t2)
    same_block = (t2 // RWKV_INV_BLOCK) == (s2 // RWKV_INV_BLOCK)
    eye = jnp.where(t2 == s2, 1.0, 0.0)
    kl = iota(one + (LANE, LANE), 1)
    vl = iota(one + (LANE, LANE), 2)
    st_diag = kl == vl
    st_blocks = (kl // HEAD_DIM) == (vl // HEAD_DIM)

    pairs = lambda x: jnp.stack([x[:, p * LANE:(p + 1) * LANE] for p in range(np_)], axis=0)
    dup = lambda x: jnp.concatenate([x, x], axis=1)
    own = lambda x: jnp.where(own_half, dup(pairs(x)), 0.0).astype(BF16)
    al2 = own(al)
    rt2 = own(rt)
    v2 = own(v)
    prod = _bmm_nt(jnp.concatenate([al2, rt2], axis=1),
                   jnp.concatenate([dup(pairs(be)), dup(pairs(kt))], axis=1))
    a_ab = jnp.where(strict, prod[:, :c2, :c2], 0.0)
    a_ak = jnp.where(strict, prod[:, :c2, c2:], 0.0)
    m_rb = jnp.where(lower, prod[:, c2:, :c2], 0.0)
    m_rk = jnp.where(lower, prod[:, c2:, c2:], 0.0)
    tinv = _unit_lower_inverse(a_ab, eye, same_block)
    st = st_ref[...]
    u2 = _bmm(tinv, _bmm(al2, st) + _bmm(a_ak, v2))
    y2 = _bmm(rt2, st) + _bmm(m_rb, u2) + _bmm(m_rk, v2)
    y3 = y2[:, :cs] + y2[:, cs:]
    u = u2[:, :cs] + u2[:, cs:]
    p_col = jnp.sum(jnp.where(st_diag, pairs(p_end), 0.0), axis=2, keepdims=True)
    inc = _bmm_tn(pairs(be_end), u) + _bmm_tn(pairs(kt_end), pairs(v))
    st_ref[...] = st * p_col + jnp.where(st_blocks, inc, 0.0)

    y = jnp.concatenate([y3[p] for p in range(np_)], axis=-1)
    mean = _head_sums(y, first_half) * (1.0 / HEAD_DIM)
    yc = y - mean
    var = _head_sums(yc * yc, first_half) * (1.0 / HEAD_DIM)
    y_ref[...] = (yc * lax.rsqrt(var + GN_EPS) * gw_ref[...] + gb_ref[...] + bonus) * gate

    @pl.when(c == pl.num_programs(1) - 1)
    def _():
        for p in range(np_):
            sp = st_ref[p].T
            s_out_ref[0, 2 * p] = sp[:HEAD_DIM, :HEAD_DIM]
            s_out_ref[0, 2 * p + 1] = sp[HEAD_DIM:, HEAD_DIM:]


def _rwkv_scan(proj, in_wts, head_wts, li, batch, cs):
    m, width = proj.shape
    nc = m // batch // cs
    st_spec = pl.BlockSpec((1, B_HEADS, HEAD_DIM, HEAD_DIM), lambda b, c: (b, 0, 0, 0))
    return pl.pallas_call(
        _rwkv_scan_kernel,
        grid=(batch, nc),
        in_specs=[pl.BlockSpec((cs, width), lambda b, c: (b * nc + c, 0)),
                  pl.BlockSpec((8, width), lambda b, c: (jnp.maximum((b * nc + c) * (cs // 8) - 1, 0), 0))]
        + [_layer(t, (li,)) for t in in_wts + head_wts],
        out_specs=[pl.BlockSpec((cs, B_WIDTH), lambda b, c: (b * nc + c, 0)), st_spec],
        out_shape=[jax.ShapeDtypeStruct((m, B_WIDTH), F32),
                   jax.ShapeDtypeStruct((batch, B_HEADS, HEAD_DIM, HEAD_DIM), F32)],
        scratch_shapes=[pltpu.VMEM((HEAD_PAIRS, LANE, LANE), F32)],
        compiler_params=_params(("parallel", "arbitrary")),
        name="rwkv_scan",
    )(proj, proj, *in_wts, *head_wts)


def _rwkv_step_kernel(r_ref, k_ref, v_ref, lw_ref, a_ref, g_ref, s_ref, kk_ref, ka_ref, rk_ref, gw_ref, gb_ref,
                      y_ref, s_out_ref):
    bb = r_ref.shape[0]
    n = bb * B_HEADS

    def heads(ref):
        x = ref[...]
        return jnp.stack([x[b:b + 1, h * HEAD_DIM:(h + 1) * HEAD_DIM]
                          for b in range(bb) for h in range(B_HEADS)], axis=0)

    per_seq = lambda ref: jnp.concatenate([ref[...]] * bb, axis=0)
    r, k, v, lw, a = heads(r_ref), heads(k_ref), heads(v_ref), heads(lw_ref), heads(a_ref)
    s = s_ref[...].reshape(n, HEAD_DIM, HEAD_DIM)
    kk = k * per_seq(kk_ref)
    kk = kk / jnp.maximum(jnp.sqrt(jnp.sum(kk * kk, axis=-1, keepdims=True)), 1e-12)
    k2 = k * (1.0 + (a - 1.0) * per_seq(ka_ref))
    eye = (lax.broadcasted_iota(jnp.int32, (1, HEAD_DIM, HEAD_DIM), 1)
           == lax.broadcasted_iota(jnp.int32, (1, HEAD_DIM, HEAD_DIM), 2))
    sa = jnp.sum(s * (-kk), axis=-1, keepdims=True)
    v_col = jnp.sum(jnp.where(eye, v, 0.0), axis=-1, keepdims=True)
    s_new = s * jnp.exp(lw) + sa * (kk * a) + v_col * k2
    y_col = jnp.sum(s_new * r, axis=-1, keepdims=True)
    y = jnp.sum(jnp.where(eye, y_col, 0.0), axis=1, keepdims=True)
    mean = jnp.mean(y, axis=-1, keepdims=True)
    yc = y - mean
    var = jnp.mean(yc * yc, axis=-1, keepdims=True)
    out = (yc * lax.rsqrt(var + GN_EPS) * per_seq(gw_ref) + per_seq(gb_ref)
           + jnp.sum(r * k2 * per_seq(rk_ref), axis=-1, keepdims=True) * v)
    s_out_ref[...] = s_new.reshape(bb, B_HEADS, HEAD_DIM, HEAD_DIM)
    y_ref[...] = g_ref[...] * jnp.concatenate(
        [jnp.concatenate([out[b * B_HEADS + h] for h in range(B_HEADS)], axis=-1) for b in range(bb)], axis=0)


def _rwkv_step(r, k, v, lw, a, gate, state, head_wts, li, bb):
    batch = r.shape[0]
    row_spec = pl.BlockSpec((bb, B_WIDTH), lambda b: (b, 0))
    st_shape = (bb, B_HEADS, HEAD_DIM, HEAD_DIM)
    return pl.pallas_call(
        _rwkv_step_kernel,
        grid=(batch // bb,),
        in_specs=[row_spec] * 6 + [pl.BlockSpec((None,) + st_shape, lambda b: (li, b, 0, 0, 0))]
        + [_layer(t, (li,)) for t in head_wts],
        out_specs=[row_spec, pl.BlockSpec(st_shape, lambda b: (b, 0, 0, 0))],
        out_shape=[jax.ShapeDtypeStruct((batch, B_WIDTH), F32),
                   jax.ShapeDtypeStruct((batch, B_HEADS, HEAD_DIM, HEAD_DIM), F32)],
        compiler_params=_params(("parallel",)),
        name="rwkv_step",
    )(r, k, v, lw, a, gate, state, *head_wts)


def _ab_out_kernel(x_ref, ao_ref, yb_ref, wa_ref, wb_ref, g_ref, o_ref):
    mix = _dot(ao_ref[...].astype(BF16), wa_ref[...])
    mix += _dot(yb_ref[...].astype(BF16), wb_ref[...])
    o_ref[...] = mix
    _add_normed_rows(o_ref, x_ref, g_ref[...])


def _ab_out(x, ao, yb, w_out, norm_g, g_row, li, tm):
    m, d = x.shape
    return pl.pallas_call(
        _ab_out_kernel,
        grid=(m // tm,),
        in_specs=[
            pl.BlockSpec((tm, d), lambda i: (i, 0)),
            pl.BlockSpec((tm, A_Q), lambda i: (i, 0)),
            pl.BlockSpec((tm, B_WIDTH), lambda i: (i, 0)),
            pl.BlockSpec((None, A_Q, d), lambda i: (li, 0, 0)),
            pl.BlockSpec((None, B_WIDTH, d), lambda i: (li, 1, 0)),
            _layer(norm_g, (g_row,)),
        ],
        out_specs=pl.BlockSpec((tm, d), lambda i: (i, 0)),
        out_shape=jax.ShapeDtypeStruct((m, d), F32),
        compiler_params=_params(("parallel",)),
        name="ab_out",
    )(x, ao, yb, w_out, w_out, norm_g)


def _rope_tables(pos):
    half = HEAD_DIM // 2
    freqs = ROPE_THETA ** (-jnp.arange(half, dtype=F32) / half)
    ang = pos.astype(F32)[:, None] * freqs[None, :]
    cos = jnp.cos(ang)
    sin = jnp.sin(ang)
    return (jnp.tile(jnp.concatenate([cos, cos], axis=-1), (1, A_HEADS)),
            jnp.tile(jnp.concatenate([-sin, sin], axis=-1), (1, A_HEADS)))


def _trunk(xp3, xs3, past, w):
    bp, seq, d = xp3.shape
    bs = xs3.shape[0]
    mp = bp * seq
    xp = xp3.reshape(mp, d)
    xs = xs3.reshape(bs, d)
    tp = _tiles(mp)
    ts = _tiles(bs)
    assert all(seq % n == 0 for n in (tp["mix_m"], RWKV_CHUNK, WINDOW)) and mp % tp["ffn_m"] == 0
    cos_p, sin_p = _rope_tables(jnp.arange(seq, dtype=jnp.int32))
    cos_s, sin_s = _rope_tables(PAST_LEN + jnp.arange(1, dtype=jnp.int32))
    cache_k, cache_v, state, shift, conv_buf = past
    ng = w["norm_g"]
    outs = {key: [] for key in ("pk", "pv", "ps", "psh", "pc", "sk", "sv", "ss", "ssh", "sc")}
    kv_heads = lambda t, b: t.reshape(b, WINDOW, A_KV_HEADS, HEAD_DIM)
    for l in range(w["depth"]):
        i = l // 2
        grow = lambda j: l * 6 + j
        xp, xs = _ffn(xp, xs, ng, (grow(0), grow(1)), w["ffn_w_gu"], w["ffn_w_down"], (l, 0),
                      tp["ffn_m"], tp["ffn_f"])
        if l % 2 == 0:
            proj = _norm_mm(xp, ng, grow(2), w["ab_w_in"], i, tp["proj_m"], 1280)
            ao, nk, nv = _swa_seq(proj, w["attn_sinks"], i, cos_p, sin_p, bp, seq)
            yb, st = _rwkv_scan(proj, w["rwkv_in"], w["rwkv_rows"], i, bp, RWKV_CHUNK)
            xp = _ab_out(xp, ao, yb, w["ab_w_out"], ng, grow(3), i, tp["mix_m"])
            outs["pk"].append(kv_heads(nk, bp))
            outs["pv"].append(kv_heads(nv, bp))
            outs["ps"].append(st)
            outs["psh"].append(proj.reshape(bp, seq, AB_COLS_PAD)[:, -1, A_COLS:AB_COLS])

            proj = _norm_mm(xs, ng, grow(2), w["ab_w_in"], i, ts["proj_m"], 1280)
            ao, nk, nv = _swa_step(proj, w["attn_sinks"], i, cache_k, cache_v, cos_s, sin_s, min(8, bs))
            prev = jnp.pad(shift[i], ((0, 0), (0, B_COLS_PAD - B_COLS)))
            r, k, v, lw, a, gt = _rwkv_in_step(proj, prev, w["rwkv_in"], i)
            yb, st = _rwkv_step(r, k, v, lw, a, gt, state, w["rwkv_heads"], i, min(8, bs))
            xs = _ab_out(xs, ao, yb, w["ab_w_out"], ng, grow(3), i, ts["mix_m"])
            outs["sk"].append(kv_heads(nk, bs))
            outs["sv"].append(kv_heads(nv, bs))
            outs["ss"].append(st)
            outs["ssh"].append(proj[:, A_COLS:AB_COLS])
        else:
            bg, u = _conv_in(xp, ng, grow(2), w["conv_w_in"], i, tp["proj_m"], 512)
            xp = _conv_out_seq(xp, bg, u, w["conv_w"], w["conv_w_out"], ng, grow(3), i, tp["mix_m"], seq)
            outs["pc"].append(u.reshape(bp, seq, -1)[:, -(CONV_W - 1):])

            bg, u = _conv_in(xs, ng, grow(2), w["conv_w_in"], i, ts["proj_m"], 512)
            buf = conv_buf[i]
            xs = _conv_out_step(xs, bg, u, buf[:, 1], buf[:, 0], w["conv_w"], w["conv_w_out"], ng, grow(3), i)
            outs["sc"].append(jnp.stack([buf[:, 1], u], axis=1))
        xp, xs = _ffn(xp, xs, ng, (grow(4), grow(5)), w["ffn_w_gu"], w["ffn_w_down"], (l, 1),
                      tp["ffn_m"], tp["ffn_f"])
    stacked = {key: jnp.stack(val) for key, val in outs.items()}
    return xp.reshape(bp, seq, d), xs.reshape(bs, 1, d), stacked


def _prep_weights(norm_g, ffn_w_gu, ffn_w_down, ab_w_in, ab_w_out, attn_sinks, rwkv_mu, rwkv_w0, rwkv_w_decay,
                  rwkv_a0, rwkv_w_aaa, rwkv_w_gate, rwkv_k_k, rwkv_k_a, rwkv_r_k, rwkv_gn_w, rwkv_gn_b,
                  conv_w_in, conv_w, conv_w_out):
    depth, n_norm, d = norm_g.shape
    padc = lambda t, n: jnp.pad(t, [(0, 0)] * (t.ndim - 1) + [(0, n - t.shape[-1])])
    row = lambda t: t[:, None, :]
    wdec = jnp.pad(rwkv_w_decay, ((0, 0), (0, D_AAA), (0, 0))).astype(BF16)
    waaa = jnp.pad(rwkv_w_aaa, ((0, 0), (D_DECAY, 0), (0, 0))).astype(BF16)
    wgate = jnp.pad(rwkv_w_gate, ((0, 0), (0, GATE_PAD - D_GATE), (0, 0))).astype(BF16)
    return {
        "depth": depth,
        "norm_g": norm_g.reshape(depth * n_norm, 1, d),
        "ffn_w_gu": ffn_w_gu,
        "ffn_w_down": ffn_w_down,
        "ab_w_in": padc(ab_w_in, AB_COLS_PAD).astype(BF16),
        "ab_w_out": ab_w_out.astype(BF16),
        "attn_sinks": attn_sinks,
        "rwkv_in": (row(padc(rwkv_mu, B_COLS_PAD)), row(rwkv_w0), wdec, row(rwkv_a0), waaa, wgate),
        "rwkv_rows": tuple(row(t) for t in (rwkv_k_k, rwkv_k_a, rwkv_r_k, rwkv_gn_w, rwkv_gn_b)),
        "rwkv_heads": tuple(t.reshape(-1, B_HEADS, 1, HEAD_DIM)
                            for t in (rwkv_k_k, rwkv_k_a, rwkv_r_k, rwkv_gn_w, rwkv_gn_b)),
        "conv_w_in": conv_w_in,
        "conv_w": conv_w,
        "conv_w_out": conv_w_out.astype(BF16),
    }


def kernel(x_prompt, x_sample, cache_swa_k, cache_swa_v, state_rwkv, state_rwkv_shift, state_conv, norm_g, ffn_w_gu, ffn_w_down, ab_w_in, ab_w_out, attn_sinks, rwkv_mu, rwkv_w0, rwkv_w_decay, rwkv_a0, rwkv_w_aaa, rwkv_w_gate, rwkv_k_k, rwkv_k_a, rwkv_r_k, rwkv_gn_w, rwkv_gn_b, conv_w_in, conv_w, conv_w_out):
    w = _prep_weights(norm_g, ffn_w_gu, ffn_w_down, ab_w_in, ab_w_out, attn_sinks, rwkv_mu, rwkv_w0, rwkv_w_decay,
                      rwkv_a0, rwkv_w_aaa, rwkv_w_gate, rwkv_k_k, rwkv_k_a, rwkv_r_k, rwkv_gn_w, rwkv_gn_b,
                      conv_w_in, conv_w, conv_w_out)
    n_ab, dec_batch = cache_swa_k.shape[:2]
    past = (cache_swa_k.reshape(n_ab, dec_batch, WINDOW, A_KV), cache_swa_v.reshape(n_ab, dec_batch, WINDOW, A_KV),
            state_rwkv, state_rwkv_shift, state_conv)
    y_prompt, y_sample, o = _trunk(x_prompt, x_sample, past, w)
    return (y_prompt, y_sample, o["pk"], o["pv"], o["ps"], o["psh"], o["pc"],
            o["sk"], o["sv"], o["ss"], o["ssh"], o["sc"])
```

```python
import functools

import jax
import jax.numpy as jnp
from jax import lax
from jax.experimental import pallas as pl
from jax.experimental.pallas import tpu as pltpu

F32 = jnp.float32
BF16 = jnp.bfloat16

HEAD_DIM = 64
A_HEADS = 16
A_KV_HEADS = 4
A_GROUP = A_HEADS // A_KV_HEADS
WINDOW = 128
ROPE_THETA = 10000.0
PAST_LEN = 16384
A_Q = A_HEADS * HEAD_DIM
A_KV = A_KV_HEADS * HEAD_DIM
A_COLS = A_Q + 2 * A_KV
B_HEADS = 16
B_WIDTH = B_HEADS * HEAD_DIM
D_DECAY = 64
D_AAA = 64
D_GATE = 160
B_COLS = 3 * B_WIDTH + D_DECAY + D_AAA + D_GATE
GN_EPS = 64e-5
AB_COLS = A_COLS + B_COLS
CONV_W = 3
NORM_EPS = 1e-6

LANE = 128
HEAD_PAIRS = B_WIDTH // LANE
AB_COLS_PAD = 5120
B_COLS_PAD = AB_COLS_PAD - A_COLS
LORA_IN = D_DECAY + D_AAA
GATE_PAD = AB_COLS_PAD - A_COLS - 3 * B_WIDTH - LORA_IN
NORM_ROWS = 16
NORM_UNROLL = 64
RWKV_CHUNK = 64
RWKV_STEP_CHUNKS = 4
RWKV_INV_BLOCK = 16
VMEM_LIMIT = 56 * 1024 * 1024
WIDE_VMEM_LIMIT = 60 * 1024 * 1024


def _tiles(m):
    return dict(ffn_m=min(1024, m), ffn_f=256 if m >= 1024 else 512, proj_m=min(1024, m), mix_m=min(512, m))


def _params(sem, vmem_limit=VMEM_LIMIT):
    return pltpu.CompilerParams(dimension_semantics=sem, vmem_limit_bytes=vmem_limit)


def _layer(arr, idx):
    idx = tuple(idx)
    rest = arr.shape[len(idx):]
    return pl.BlockSpec((None,) * len(idx) + rest, lambda *_: idx + (0,) * len(rest))


def _rms(x, g):
    return x * lax.rsqrt(jnp.mean(x * x, axis=-1, keepdims=True) + NORM_EPS) * g


def _dot(a, b):
    return jnp.dot(a, b, preferred_element_type=F32)


def _for_row_chunks(n_rows, body):
    size = min(NORM_ROWS, n_rows)
    assert n_rows % size == 0

    def step(c, carry):
        body(pl.ds(pl.multiple_of(c * size, size), size))
        return carry

    n = n_rows // size
    lax.fori_loop(0, n, step, 0, unroll=min(NORM_UNROLL, n))


def _norm_rows_into(h_ref, x_ref, g):
    def chunk(rows):
        h_ref[rows, :] = _rms(x_ref[rows, :], g).astype(BF16)

    _for_row_chunks(x_ref.shape[0], chunk)


def _add_normed_rows(o_ref, x_ref, g, scale=1.0):
    def chunk(rows):
        normed = _rms(o_ref[rows, :], g)
        o_ref[rows, :] = x_ref[rows, :] + (normed if scale == 1.0 else scale * normed)

    _for_row_chunks(x_ref.shape[0], chunk)


def _ffn_kernel(x_ref, xs_ref, g0_ref, g1_ref, wg_ref, wu_ref, wd_ref, o_ref, os_ref, h_ref, hs_ref):
    i = pl.program_id(0)
    f = pl.program_id(1)
    g0 = g0_ref[...]
    g1 = g1_ref[...]
    cast = lambda w_ref: w_ref[...].astype(BF16)

    def group(x_ref, o_ref, h_ref):
        def hidden_tile(accumulate):
            h = h_ref[...]
            gate = _dot(h, cast(wg_ref))
            act = (gate * jax.nn.sigmoid(gate) * _dot(h, cast(wu_ref))).astype(BF16)
            if accumulate:
                o_ref[...] += _dot(act, cast(wd_ref))
            else:
                o_ref[...] = _dot(act, cast(wd_ref))

        @pl.when(f == 0)
        def _():
            _norm_rows_into(h_ref, x_ref, g0)
            hidden_tile(False)

        @pl.when(f > 0)
        def _():
            hidden_tile(True)

        @pl.when(f == pl.num_programs(1) - 1)
        def _():
            _add_normed_rows(o_ref, x_ref, g1, 0.5)

    group(x_ref, o_ref, h_ref)

    @pl.when(i == 0)
    def _():
        group(xs_ref, os_ref, hs_ref)


def _ffn(x, xs, norm_g, g_rows, w_gu, w_down, lj, tm, tf):
    m, d = x.shape
    ms = xs.shape[0]
    l, j = lj
    nf = w_down.shape[2] // tf
    return pl.pallas_call(
        _ffn_kernel,
        grid=(m // tm, nf),
        in_specs=[
            pl.BlockSpec((tm, d), lambda i, f: (i, 0)),
            pl.BlockSpec((ms, d), lambda i, f: (0, 0)),
            _layer(norm_g, (g_rows[0],)),
            _layer(norm_g, (g_rows[1],)),
            pl.BlockSpec((None, None, d, tf), lambda i, f: (l, j, 0, f)),
            pl.BlockSpec((None, None, d, tf), lambda i, f: (l, j, 0, nf + f)),
            pl.BlockSpec((None, None, tf, d), lambda i, f: (l, j, f, 0)),
        ],
        out_specs=[pl.BlockSpec((tm, d), lambda i, f: (i, 0)),
                   pl.BlockSpec((ms, d), lambda i, f: (0, 0))],
        out_shape=[jax.ShapeDtypeStruct((m, d), F32), jax.ShapeDtypeStruct((ms, d), F32)],
        scratch_shapes=[pltpu.VMEM((tm, d), BF16), pltpu.VMEM((ms, d), BF16)],
        compiler_params=_params(("arbitrary", "arbitrary"), WIDE_VMEM_LIMIT),
        name="ffn",
    )(x, xs, norm_g, norm_g, w_gu, w_gu, w_down)


def _norm_mm_kernel(x_ref, g_ref, w_ref, o_ref, h_ref):
    @pl.when(pl.program_id(1) == 0)
    def _():
        _norm_rows_into(h_ref, x_ref, g_ref[...])

    o_ref[...] = _dot(h_ref[...], w_ref[...])


def _norm_mm(x, norm_g, g_row, w, li, tm, tn):
    m, d = x.shape
    n = w.shape[-1]
    return pl.pallas_call(
        _norm_mm_kernel,
        grid=(m // tm, n // tn),
        in_specs=[
            pl.BlockSpec((tm, d), lambda i, j: (i, 0)),
            _layer(norm_g, (g_row,)),
            pl.BlockSpec((None, d, tn), lambda i, j: (li, 0, j)),
        ],
        out_specs=pl.BlockSpec((tm, tn), lambda i, j: (i, j)),
        out_shape=jax.ShapeDtypeStruct((m, n), F32),
        scratch_shapes=[pltpu.VMEM((tm, d), BF16)],
        compiler_params=_params(("parallel", "arbitrary")),
        name="ab_in",
    )(x, norm_g, w)


def _conv_in_kernel(x_ref, g_ref, wb_ref, wc_ref, wh_ref, bg_ref, u_ref, h_ref):
    @pl.when(pl.program_id(1) == 0)
    def _():
        _norm_rows_into(h_ref, x_ref, g_ref[...])

    h = h_ref[...]
    bg_ref[...] = _dot(h, wb_ref[...].astype(BF16))
    u_ref[...] = _dot(h, wc_ref[...].astype(BF16)) * _dot(h, wh_ref[...].astype(BF16))


def _conv_in(x, norm_g, g_row, w_in, li, tm, tn):
    m, d = x.shape
    dc = w_in.shape[-1] // 3
    nj = dc // tn
    return pl.pallas_call(
        _conv_in_kernel,
        grid=(m // tm, nj),
        in_specs=[
            pl.BlockSpec((tm, d), lambda i, j: (i, 0)),
            _layer(norm_g, (g_row,)),
            pl.BlockSpec((None, d, tn), lambda i, j: (li, 0, j)),
            pl.BlockSpec((None, d, tn), lambda i, j: (li, 0, nj + j)),
            pl.BlockSpec((None, d, tn), lambda i, j: (li, 0, 2 * nj + j)),
        ],
        out_specs=[pl.BlockSpec((tm, tn), lambda i, j: (i, j)),
                   pl.BlockSpec((tm, tn), lambda i, j: (i, j))],
        out_shape=[jax.ShapeDtypeStruct((m, dc), F32), jax.ShapeDtypeStruct((m, dc), F32)],
        scratch_shapes=[pltpu.VMEM((tm, d), BF16)],
        compiler_params=_params(("parallel", "arbitrary"), WIDE_VMEM_LIMIT),
        name="conv_in",
    )(x, norm_g, w_in, w_in, w_in)


def _conv_taps(u, p1, p2, cw):
    return p2 * cw[0:1, :] + p1 * cw[1:2, :] + u * cw[2:3, :]


def _conv_out_seq_kernel(x_ref, bg_ref, u_ref, halo_ref, cw_ref, w_ref, g_ref, o_ref, *, tiles_per_seq):
    u = u_ref[...]
    first = (pl.program_id(0) % tiles_per_seq) == 0
    h1 = jnp.where(first, 0.0, halo_ref[7:8, :])
    h2 = jnp.where(first, 0.0, halo_ref[6:7, :])
    row = lax.broadcasted_iota(jnp.int32, u.shape, 0)
    p1 = jnp.where(row == 0, h1, pltpu.roll(u, 1, 0))
    p2 = jnp.where(row == 0, h2, jnp.where(row == 1, h1, pltpu.roll(u, 2, 0)))
    z = (bg_ref[...] * _conv_taps(u, p1, p2, cw_ref[...])).astype(BF16)
    o_ref[...] = _dot(z, w_ref[...])
    _add_normed_rows(o_ref, x_ref, g_ref[...])


def _conv_out_step_kernel(x_ref, bg_ref, u_ref, p1_ref, p2_ref, cw_ref, w_ref, g_ref, o_ref):
    z = (bg_ref[...] * _conv_taps(u_ref[...], p1_ref[...], p2_ref[...], cw_ref[...])).astype(BF16)
    o_ref[...] = _dot(z, w_ref[...])
    _add_normed_rows(o_ref, x_ref, g_ref[...])


def _conv_out_seq(x, bg, u, conv_w, w_out, norm_g, g_row, li, tm, seq):
    m, d = x.shape
    dc = u.shape[1]
    hb = tm // 8
    row_spec = pl.BlockSpec((tm, dc), lambda i: (i, 0))
    return pl.pallas_call(
        functools.partial(_conv_out_seq_kernel, tiles_per_seq=seq // tm),
        grid=(m // tm,),
        in_specs=[
            pl.BlockSpec((tm, d), lambda i: (i, 0)),
            row_spec, row_spec,
            pl.BlockSpec((8, dc), lambda i: (jnp.maximum(i * hb - 1, 0), 0)),
            _layer(conv_w, (li,)), _layer(w_out, (li,)), _layer(norm_g, (g_row,)),
        ],
        out_specs=pl.BlockSpec((tm, d), lambda i: (i, 0)),
        out_shape=jax.ShapeDtypeStruct((m, d), F32),
        compiler_params=_params(("parallel",)),
        name="conv_out",
    )(x, bg, u, u, conv_w, w_out, norm_g)


def _conv_out_step(x, bg, u, p1, p2, conv_w, w_out, norm_g, g_row, li):
    m, d = x.shape
    dc = u.shape[1]
    row_spec = pl.BlockSpec((m, dc), lambda i: (0, 0))
    return pl.pallas_call(
        _conv_out_step_kernel,
        grid=(1,),
        in_specs=[
            pl.BlockSpec((m, d), lambda i: (0, 0)),
            row_spec, row_spec, row_spec, row_spec,
            _layer(conv_w, (li,)), _layer(w_out, (li,)), _layer(norm_g, (g_row,)),
        ],
        out_specs=pl.BlockSpec((m, d), lambda i: (0, 0)),
        out_shape=jax.ShapeDtypeStruct((m, d), F32),
        compiler_params=_params(("arbitrary",)),
        name="conv_out_step",
    )(x, bg, u, p1, p2, conv_w, w_out, norm_g)


def _rope(x, cos, sin):
    width = x.shape[-1]
    half = HEAD_DIM // 2
    lane = lax.broadcasted_iota(jnp.int32, x.shape, x.ndim - 1)
    swapped = jnp.where((lane % HEAD_DIM) < half,
                        pltpu.roll(x, width - half, x.ndim - 1),
                        pltpu.roll(x, half, x.ndim - 1))
    return x * cos + swapped * sin


def _swa_seq_kernel(sink_ref, q_ref, kc_ref, kp_ref, vc_ref, vp_ref, cq_ref, sq_ref, cp_ref, sp_ref,
                    o_ref, nk_ref, nv_ref, *, li):
    n = pl.program_id(1)
    q = _rope(q_ref[...], cq_ref[...], sq_ref[...])
    kc = _rope(kc_ref[...], cq_ref[:, :A_KV], sq_ref[:, :A_KV])
    kp = _rope(kp_ref[...], cp_ref[...], sp_ref[...])
    kw = jnp.concatenate([kp, kc], axis=0).astype(BF16)
    vw = jnp.concatenate([vp_ref[...], vc_ref[...]], axis=0).astype(BF16)
    rows = A_GROUP * WINDOW
    head = lambda x, h: x[:, h * HEAD_DIM:(h + 1) * HEAD_DIM]
    groups = lambda f: jnp.stack([f(g) for g in range(A_KV_HEADS)], axis=0)
    qg = groups(lambda g: jnp.concatenate([head(q, g * A_GROUP + j) for j in range(A_GROUP)], axis=0))
    sink = groups(lambda g: jnp.concatenate(
        [jnp.full((WINDOW, 1), sink_ref[li, g * A_GROUP + j], F32) for j in range(A_GROUP)], axis=0))
    qi = lax.broadcasted_iota(jnp.int32, (1, rows, 2 * WINDOW), 1) % WINDOW
    kj = lax.broadcasted_iota(jnp.int32, (1, rows, 2 * WINDOW), 2)
    diff = WINDOW + qi - kj
    lo = jnp.where(n > 0, 0, WINDOW)
    valid = (diff >= 0) & (diff <= WINDOW) & (kj >= lo)
    s = _bmm_nt(qg, groups(lambda g: head(kw, g)))
    s = jnp.where(valid, s * (HEAD_DIM ** -0.5), -jnp.inf)
    mx = jnp.maximum(jnp.max(s, axis=-1, keepdims=True), sink)
    e = jnp.exp(s - mx)
    p = e / (jnp.sum(e, axis=-1, keepdims=True) + jnp.exp(sink - mx))
    og = _bmm(p, groups(lambda g: head(vw, g)))
    o_ref[...] = jnp.concatenate(
        [og[h // A_GROUP][(h % A_GROUP) * WINDOW:(h % A_GROUP + 1) * WINDOW] for h in range(A_HEADS)], axis=-1)

    @pl.when(n == pl.num_programs(1) - 1)
    def _():
        nk_ref[...] = kc
        nv_ref[...] = vc_ref[...]


def _swa_seq(proj, sinks, li, cos, sin, batch, seq):
    nb = seq // WINDOW
    kcol = A_Q // A_KV
    vcol = kcol + 1
    cur = lambda b, n: b * nb + n
    prev = lambda b, n: b * nb + jnp.maximum(n - 1, 0)
    return pl.pallas_call(
        functools.partial(_swa_seq_kernel, li=li),
        grid=(batch, nb),
        in_specs=[
            pl.BlockSpec(memory_space=pltpu.SMEM),
            pl.BlockSpec((WINDOW, A_Q), lambda b, n: (cur(b, n), 0)),
            pl.BlockSpec((WINDOW, A_KV), lambda b, n: (cur(b, n), kcol)),
            pl.BlockSpec((WINDOW, A_KV), lambda b, n: (prev(b, n), kcol)),
            pl.BlockSpec((WINDOW, A_KV), lambda b, n: (cur(b, n), vcol)),
            pl.BlockSpec((WINDOW, A_KV), lambda b, n: (prev(b, n), vcol)),
            pl.BlockSpec((WINDOW, A_Q), lambda b, n: (n, 0)),
            pl.BlockSpec((WINDOW, A_Q), lambda b, n: (n, 0)),
            pl.BlockSpec((WINDOW, A_KV), lambda b, n: (jnp.maximum(n - 1, 0), 0)),
            pl.BlockSpec((WINDOW, A_KV), lambda b, n: (jnp.maximum(n - 1, 0), 0)),
        ],
        out_specs=[
            pl.BlockSpec((WINDOW, A_Q), lambda b, n: (cur(b, n), 0)),
            pl.BlockSpec((WINDOW, A_KV), lambda b, n: (b, 0)),
            pl.BlockSpec((WINDOW, A_KV), lambda b, n: (b, 0)),
        ],
        out_shape=[
            jax.ShapeDtypeStruct((batch * seq, A_Q), F32),
            jax.ShapeDtypeStruct((batch * WINDOW, A_KV), F32),
            jax.ShapeDtypeStruct((batch * WINDOW, A_KV), F32),
        ],
        compiler_params=_params(("parallel", "arbitrary")),
        name="swa_seq",
    )(sinks, proj, proj, proj, proj, proj, cos, sin, cos, sin)


def _swa_step_kernel(sink_ref, p_ref, ck_ref, cv_ref, cos_ref, sin_ref, o_ref, nk_ref, nv_ref, *, li):
    sb = p_ref.shape[0]
    pairs = A_KV // LANE
    per_pair = A_HEADS // pairs
    q = _rope(p_ref[:, 0:A_Q], cos_ref[...], sin_ref[...]) * (HEAD_DIM ** -0.5)
    kn = _rope(p_ref[:, A_Q:A_Q + A_KV], cos_ref[:, :A_KV], sin_ref[:, :A_KV])
    vn = p_ref[:, A_Q + A_KV:A_COLS]
    zero = jnp.zeros((1, HEAD_DIM), F32)

    def q_row(s, h):
        piece = q[s:s + 1, h * HEAD_DIM:(h + 1) * HEAD_DIM]
        return jnp.concatenate([piece, zero] if (h // A_GROUP) % 2 == 0 else [zero, piece], axis=-1)

    batch = [(s, p) for s in range(sb) for p in range(pairs)]
    qb = jnp.stack([jnp.concatenate([q_row(s, p * per_pair + r) for r in range(per_pair)], axis=0)
                    for s, p in batch], axis=0)
    lanes = lambda x, s, p: x[s:s + 1, p * LANE:(p + 1) * LANE]
    knb = jnp.stack([lanes(kn, s, p) for s, p in batch], axis=0)
    vnb = jnp.stack([lanes(vn, s, p) for s, p in batch], axis=0)
    kb = jnp.stack([ck_ref[s, :, p * LANE:(p + 1) * LANE] for s, p in batch], axis=0)
    vb = jnp.stack([cv_ref[s, :, p * LANE:(p + 1) * LANE] for s, p in batch], axis=0)
    sink = jnp.stack([jnp.concatenate([jnp.full((1, 1), sink_ref[li, p * per_pair + r], F32)
                                       for r in range(per_pair)], axis=0) for s, p in batch], axis=0)
    s_old = _bmm_nt(qb, kb)
    s_new = jnp.sum(qb * knb, axis=-1, keepdims=True)
    mx = jnp.maximum(jnp.maximum(jnp.max(s_old, axis=-1, keepdims=True), s_new), sink)
    e = jnp.exp(s_old - mx)
    e_new = jnp.exp(s_new - mx)
    den = jnp.sum(e, axis=-1, keepdims=True) + e_new + jnp.exp(sink - mx)
    ob = (_bmm(e, vb) + e_new * vnb) / den

    def o_piece(s, h):
        row = ob[s * pairs + h // per_pair][h % per_pair:h % per_pair + 1]
        half = (h // A_GROUP) % 2
        return row[:, half * HEAD_DIM:(half + 1) * HEAD_DIM]

    o_ref[...] = jnp.concatenate(
        [jnp.concatenate([o_piece(s, h) for h in range(A_HEADS)], axis=-1) for s in range(sb)], axis=0)
    nk_ref[:, 0:WINDOW - 1, :] = ck_ref[:, 1:WINDOW, :]
    nv_ref[:, 0:WINDOW - 1, :] = cv_ref[:, 1:WINDOW, :]
    for s in range(sb):
        nk_ref[s, WINDOW - 1:WINDOW, :] = kn[s:s + 1, :]
        nv_ref[s, WINDOW - 1:WINDOW, :] = vn[s:s + 1, :]


def _swa_step(proj, sinks, li, cache_k, cache_v, cos, sin, sb):
    batch, width = proj.shape
    cache_in = pl.BlockSpec((None, sb, WINDOW, A_KV), lambda b: (li, b, 0, 0))
    cache_out = pl.BlockSpec((sb, WINDOW, A_KV), lambda b: (b, 0, 0))
    return pl.pallas_call(
        functools.partial(_swa_step_kernel, li=li),
        grid=(batch // sb,),
        in_specs=[
            pl.BlockSpec(memory_space=pltpu.SMEM),
            pl.BlockSpec((sb, width), lambda b: (b, 0)),
            cache_in, cache_in,
            pl.BlockSpec((1, A_Q), lambda b: (0, 0)),
            pl.BlockSpec((1, A_Q), lambda b: (0, 0)),
        ],
        out_specs=[pl.BlockSpec((sb, A_Q), lambda b: (b, 0)), cache_out, cache_out],
        out_shape=[
            jax.ShapeDtypeStruct((batch, A_Q), F32),
            jax.ShapeDtypeStruct((batch, WINDOW, A_KV), F32),
            jax.ShapeDtypeStruct((batch, WINDOW, A_KV), F32),
        ],
        compiler_params=_params(("parallel",)),
        name="swa_step",
    )(sinks, proj, cache_k, cache_v, cos, sin)


def _softplus(z):
    return jnp.maximum(z, 0.0) + jnp.log(1.0 + jnp.exp(-jnp.abs(z)))


def _rwkv_in_math(pb, prev, mu, w0, wdec, a0, waaa, wgate):
    xm = pb + (prev - pb) * mu
    r = xm[:, 0:B_WIDTH]
    k = xm[:, B_WIDTH:2 * B_WIDTH]
    v = xm[:, 2 * B_WIDTH:3 * B_WIDTH]
    lora = xm[:, 3 * B_WIDTH:3 * B_WIDTH + LORA_IN]
    dg = xm[:, 3 * B_WIDTH + LORA_IN:]
    w_log = -_softplus(-(w0 + _dot(jnp.tanh(lora).astype(BF16), wdec))) - 0.5
    lw = -jnp.exp(w_log)
    a = jax.nn.sigmoid(a0 + _dot(lora.astype(BF16), waaa))
    g = _dot(jax.nn.sigmoid(dg).astype(BF16), wgate)
    return r, k, v, lw, a, g


def _rwkv_in_step_kernel(p_ref, prev_ref, mu_ref, w0_ref, wdec_ref, a0_ref, waaa_ref, wgate_ref, *out_refs):
    vals = _rwkv_in_math(p_ref[:, A_COLS:], prev_ref[...], mu_ref[...], w0_ref[...], wdec_ref[...], a0_ref[...],
                         waaa_ref[...], wgate_ref[...])
    for ref, val in zip(out_refs, vals):
        ref[...] = val


def _rwkv_in_step(proj, prev, wts, li):
    m, width = proj.shape
    out_spec = pl.BlockSpec((m, B_WIDTH), lambda i: (0, 0))
    return pl.pallas_call(
        _rwkv_in_step_kernel,
        grid=(1,),
        in_specs=[pl.BlockSpec((m, width), lambda i: (0, 0)), pl.BlockSpec((m, B_COLS_PAD), lambda i: (0, 0))]
        + [_layer(a, (li,)) for a in wts],
        out_specs=[out_spec] * 6,
        out_shape=[jax.ShapeDtypeStruct((m, B_WIDTH), F32)] * 6,
        compiler_params=_params(("arbitrary",)),
        name="rwkv_in_step",
    )(proj, prev, *wts)


def _bdg(a, b, dn):
    return lax.dot_general(a.astype(BF16), b.astype(BF16), dn, preferred_element_type=F32)


def _bmm(a, b):
    return _bdg(a, b, (((2,), (1,)), ((0,), (0,))))


def _bmm_nt(a, b):
    return _bdg(a, b, (((2,), (2,)), ((0,), (0,))))


def _bmm_tn(a, b):
    return _bdg(a, b, (((1,), (1,)), ((0,), (0,))))


def _unit_lower_inverse(a, eye, same_block):
    d = jnp.where(same_block, a, 0.0)
    e = a - d
    x = eye + d
    dp = d
    p = 1
    while 2 * p < RWKV_INV_BLOCK:
        dp = _bmm(dp, dp)
        x = _bmm(x, eye + dp)
        p *= 2
    nn = _bmm(x, e)
    return _bmm(_bmm(eye + nn, eye + _bmm(nn, nn)), x)


def _head_sums(x, first_half):
    outs = []
    for p in range(HEAD_PAIRS):
        xp = x[:, p * LANE:(p + 1) * LANE]
        s0 = jnp.sum(jnp.where(first_half, xp, 0.0), axis=-1, keepdims=True)
        s1 = jnp.sum(jnp.where(first_half, 0.0, xp), axis=-1, keepdims=True)
        outs.append(jnp.where(first_half, s0, s1))
    return jnp.concatenate(outs, axis=-1)


def _rwkv_scan_kernel(p_ref, halo_ref, mu_ref, w0_ref, wdec_ref, a0_ref, waaa_ref, wgate_ref,
                      kk_ref, ka_ref, rk_ref, gw_ref, gb_ref, y_ref, s_out_ref, st_ref):
    c = pl.program_id(1)

    @pl.when(c == 0)
    def _():
        st_ref[...] = jnp.zeros_like(st_ref)

    pb = p_ref[:, A_COLS:]
    hrow = jnp.where(c == 0, 0.0, halo_ref[7:8, A_COLS:])
    prev = jnp.where(lax.broadcasted_iota(jnp.int32, pb.shape, 0) == 0, hrow, pltpu.roll(pb, 1, 0))
    r, k, v, lw, a, gate = _rwkv_in_math(pb, prev, mu_ref[...], w0_ref[...], wdec_ref[...], a0_ref[...],
                                         waaa_ref[...], wgate_ref[...])
    rows = r.shape[0]
    cs = RWKV_CHUNK
    chunks = rows // cs
    c2 = 2 * cs
    iota = lambda shape, dim: lax.broadcasted_iota(jnp.int32, shape, dim)

    ti = iota((rows, rows), 0)
    si = iota((rows, rows), 1)
    tril = jnp.where((si <= ti) & (si // cs == ti // cs), 1.0, 0.0).astype(BF16)
    lw_hi = lw.astype(BF16)
    rem = lw - lw_hi.astype(F32)
    lw_mid = rem.astype(BF16)
    lw_lo = (rem - lw_mid.astype(F32)).astype(BF16)
    lcum = _dot(tril, lw_hi) + (_dot(tril, lw_mid) + _dot(tril, lw_lo))
    p_t = jnp.exp(lcum)
    p_inv = jnp.exp(-lcum)
    p_prev = jnp.exp(lcum - lw)
    p_ends = [jnp.exp(lcum[(j + 1) * cs - 1:(j + 1) * cs, :]) for j in range(chunks)]
    p_end = jnp.concatenate([jnp.broadcast_to(pe, (cs, B_WIDTH)) for pe in p_ends], axis=0)

    first_half = iota((rows, LANE), 1) < HEAD_DIM
    kk = k * kk_ref[...]
    kk = kk / jnp.maximum(jnp.sqrt(_head_sums(kk * kk, first_half)), 1e-12)
    k2 = k * (1.0 + (a - 1.0) * ka_ref[...])
    al = -kk * p_prev
    be = kk * a * p_inv
    kt = k2 * p_inv
    rt = r * p_t
    be_end = be * p_end
    kt_end = kt * p_end
    bonus = _head_sums(r * k2 * rk_ref[...], first_half) * v

    np_, one = HEAD_PAIRS, (1,)
    own_half = (iota(one + (c2, LANE), 1) // cs) == (iota(one + (c2, LANE), 2) // HEAD_DIM)
    t2 = iota(one + (c2, c2), 1)
    s2 = iota(one + (c2, c2), 2)
    same_head = (t2 // cs) == (s2 // cs)
    strict = same_head & (s2 < t2)
    lower = same_head & (s2 <= t2)
    same_block = (t2 // RWKV_INV_BLOCK) == (s2 // RWKV_INV_BLOCK)
    eye = jnp.where(t2 == s2, 1.0, 0.0)
    kl = iota(one + (LANE, LANE), 1)
    vl = iota(one + (LANE, LANE), 2)
    st_diag = kl == vl
    st_blocks = (kl // HEAD_DIM) == (vl // HEAD_DIM)

    pairs = lambda x: jnp.stack([x[j * cs:(j + 1) * cs, p * LANE:(p + 1) * LANE]
                                 for j in range(chunks) for p in range(np_)], axis=0)
    dup = lambda x: jnp.concatenate([x, x], axis=1)
    own = lambda x: jnp.where(own_half, dup(pairs(x)), 0.0).astype(BF16)
    al2 = own(al)
    rt2 = own(rt)
    v2 = own(v)
    prod = _bmm_nt(jnp.concatenate([al2, rt2], axis=1),
                   jnp.concatenate([dup(pairs(be)), dup(pairs(kt))], axis=1))
    a_ab = jnp.where(strict, prod[:, :c2, :c2], 0.0)
    a_ak = jnp.where(strict, prod[:, :c2, c2:], 0.0)
    m_rb = jnp.where(lower, prod[:, c2:, :c2], 0.0)
    m_rk = jnp.where(lower, prod[:, c2:, c2:], 0.0)
    tinv = _unit_lower_inverse(a_ab, eye, same_block)
    be_end_p, kt_end_p, v_p = pairs(be_end), pairs(kt_end), pairs(v)

    st = st_ref[...]
    ys = []
    for j in range(chunks):
        of = lambda x: x[j * np_:(j + 1) * np_]
        u2 = _bmm(of(tinv), _bmm(of(al2), st) + _bmm(of(a_ak), of(v2)))
        y2 = _bmm(of(rt2), st) + _bmm(of(m_rb), u2) + _bmm(of(m_rk), of(v2))
        y3 = y2[:, :cs] + y2[:, cs:]
        u = u2[:, :cs] + u2[:, cs:]
        pe = jnp.stack([p_ends[j][:, p * LANE:(p + 1) * LANE] for p in range(np_)], axis=0)
        p_col = jnp.sum(jnp.where(st_diag, pe, 0.0), axis=2, keepdims=True)
        inc = _bmm_tn(of(be_end_p), u) + _bmm_tn(of(kt_end_p), of(v_p))
        st = st * p_col + jnp.where(st_blocks, inc, 0.0)
        ys.append(jnp.concatenate([y3[p] for p in range(np_)], axis=-1))
    st_ref[...] = st

    y = jnp.concatenate(ys, axis=0)
    mean = _head_sums(y, first_half) * (1.0 / HEAD_DIM)
    yc = y - mean
    var = _head_sums(yc * yc, first_half) * (1.0 / HEAD_DIM)
    y_ref[...] = (yc * lax.rsqrt(var + GN_EPS) * gw_ref[...] + gb_ref[...] + bonus) * gate

    @pl.when(c == pl.num_programs(1) - 1)
    def _():
        for p in range(np_):
            sp = st_ref[p].T
            s_out_ref[0, 2 * p] = sp[:HEAD_DIM, :HEAD_DIM]
            s_out_ref[0, 2 * p + 1] = sp[HEAD_DIM:, HEAD_DIM:]


def _rwkv_scan(proj, in_wts, head_wts, li, batch, cs):
    m, width = proj.shape
    nc = m // batch // cs
    st_spec = pl.BlockSpec((1, B_HEADS, HEAD_DIM, HEAD_DIM), lambda b, c: (b, 0, 0, 0))
    return pl.pallas_call(
        _rwkv_scan_kernel,
        grid=(batch, nc),
        in_specs=[pl.BlockSpec((cs, width), lambda b, c: (b * nc + c, 0)),
                  pl.BlockSpec((8, width), lambda b, c: (jnp.maximum((b * nc + c) * (cs // 8) - 1, 0), 0))]
        + [_layer(t, (li,)) for t in in_wts + head_wts],
        out_specs=[pl.BlockSpec((cs, B_WIDTH), lambda b, c: (b * nc + c, 0)), st_spec],
        out_shape=[jax.ShapeDtypeStruct((m, B_WIDTH), F32),
                   jax.ShapeDtypeStruct((batch, B_HEADS, HEAD_DIM, HEAD_DIM), F32)],
        scratch_shapes=[pltpu.VMEM((HEAD_PAIRS, LANE, LANE), F32)],
        compiler_params=_params(("parallel", "arbitrary")),
        name="rwkv_scan",
    )(proj, proj, *in_wts, *head_wts)


def _rwkv_step_kernel(r_ref, k_ref, v_ref, lw_ref, a_ref, g_ref, s_ref, kk_ref, ka_ref, rk_ref, gw_ref, gb_ref,
                      y_ref, s_out_ref):
    bb = r_ref.shape[0]
    n = bb * B_HEADS

    def heads(ref):
        x = ref[...]
        return jnp.stack([x[b:b + 1, h * HEAD_DIM:(h + 1) * HEAD_DIM]
                          for b in range(bb) for h in range(B_HEADS)], axis=0)

    per_seq = lambda ref: jnp.concatenate([ref[...]] * bb, axis=0)
    r, k, v, lw, a = heads(r_ref), heads(k_ref), heads(v_ref), heads(lw_ref), heads(a_ref)
    s = s_ref[...].reshape(n, HEAD_DIM, HEAD_DIM)
    kk = k * per_seq(kk_ref)
    kk = kk / jnp.maximum(jnp.sqrt(jnp.sum(kk * kk, axis=-1, keepdims=True)), 1e-12)
    k2 = k * (1.0 + (a - 1.0) * per_seq(ka_ref))
    eye = (lax.broadcasted_iota(jnp.int32, (1, HEAD_DIM, HEAD_DIM), 1)
           == lax.broadcasted_iota(jnp.int32, (1, HEAD_DIM, HEAD_DIM), 2))
    sa = jnp.sum(s * (-kk), axis=-1, keepdims=True)
    v_col = jnp.sum(jnp.where(eye, v, 0.0), axis=-1, keepdims=True)
    s_new = s * jnp.exp(lw) + sa * (kk * a) + v_col * k2
    y_col = jnp.sum(s_new * r, axis=-1, keepdims=True)
    y = jnp.sum(jnp.where(eye, y_col, 0.0), axis=1, keepdims=True)
    mean = jnp.mean(y, axis=-1, keepdims=True)
    yc = y - mean
    var = jnp.mean(yc * yc, axis=-1, keepdims=True)
    out = (yc * lax.rsqrt(var + GN_EPS) * per_seq(gw_ref) + per_seq(gb_ref)
           + jnp.sum(r * k2 * per_seq(rk_ref), axis=-1, keepdims=True) * v)
    s_out_ref[...] = s_new.reshape(bb, B_HEADS, HEAD_DIM, HEAD_DIM)
    y_ref[...] = g_ref[...] * jnp.concatenate(
        [jnp.concatenate([out[b * B_HEADS + h] for h in range(B_HEADS)], axis=-1) for b in range(bb)], axis=0)


def _rwkv_step(r, k, v, lw, a, gate, state, head_wts, li, bb):
    batch = r.shape[0]
    row_spec = pl.BlockSpec((bb, B_WIDTH), lambda b: (b, 0))
    st_shape = (bb, B_HEADS, HEAD_DIM, HEAD_DIM)
    return pl.pallas_call(
        _rwkv_step_kernel,
        grid=(batch // bb,),
        in_specs=[row_spec] * 6 + [pl.BlockSpec((None,) + st_shape, lambda b: (li, b, 0, 0, 0))]
        + [_layer(t, (li,)) for t in head_wts],
        out_specs=[row_spec, pl.BlockSpec(st_shape, lambda b: (b, 0, 0, 0))],
        out_shape=[jax.ShapeDtypeStruct((batch, B_WIDTH), F32),
                   jax.ShapeDtypeStruct((batch, B_HEADS, HEAD_DIM, HEAD_DIM), F32)],
        compiler_params=_params(("parallel",)),
        name="rwkv_step",
    )(r, k, v, lw, a, gate, state, *head_wts)


def _ab_out_kernel(x_ref, ao_ref, yb_ref, wa_ref, wb_ref, g_ref, o_ref):
    mix = _dot(ao_ref[...].astype(BF16), wa_ref[...])
    mix += _dot(yb_ref[...].astype(BF16), wb_ref[...])
    o_ref[...] = mix
    _add_normed_rows(o_ref, x_ref, g_ref[...])


def _ab_out(x, ao, yb, w_out, norm_g, g_row, li, tm):
    m, d = x.shape
    return pl.pallas_call(
        _ab_out_kernel,
        grid=(m // tm,),
        in_specs=[
            pl.BlockSpec((tm, d), lambda i: (i, 0)),
            pl.BlockSpec((tm, A_Q), lambda i: (i, 0)),
            pl.BlockSpec((tm, B_WIDTH), lambda i: (i, 0)),
            pl.BlockSpec((None, A_Q, d), lambda i: (li, 0, 0)),
            pl.BlockSpec((None, B_WIDTH, d), lambda i: (li, 1, 0)),
            _layer(norm_g, (g_row,)),
        ],
        out_specs=pl.BlockSpec((tm, d), lambda i: (i, 0)),
        out_shape=jax.ShapeDtypeStruct((m, d), F32),
        compiler_params=_params(("parallel",)),
        name="ab_out",
    )(x, ao, yb, w_out, w_out, norm_g)


def _rope_tables(pos):
    half = HEAD_DIM // 2
    freqs = ROPE_THETA ** (-jnp.arange(half, dtype=F32) / half)
    ang = pos.astype(F32)[:, None] * freqs[None, :]
    cos = jnp.cos(ang)
    sin = jnp.sin(ang)
    return (jnp.tile(jnp.concatenate([cos, cos], axis=-1), (1, A_HEADS)),
            jnp.tile(jnp.concatenate([-sin, sin], axis=-1), (1, A_HEADS)))


def _trunk(xp3, xs3, past, w):
    bp, seq, d = xp3.shape
    bs = xs3.shape[0]
    mp = bp * seq
    xp = xp3.reshape(mp, d)
    xs = xs3.reshape(bs, d)
    tp = _tiles(mp)
    ts = _tiles(bs)
    assert all(seq % n == 0 for n in (tp["mix_m"], RWKV_CHUNK * RWKV_STEP_CHUNKS, WINDOW))
    assert mp % tp["ffn_m"] == 0
    cos_p, sin_p = _rope_tables(jnp.arange(seq, dtype=jnp.int32))
    cos_s, sin_s = _rope_tables(PAST_LEN + jnp.arange(1, dtype=jnp.int32))
    cache_k, cache_v, state, shift, conv_buf = past
    ng = w["norm_g"]
    outs = {key: [] for key in ("pk", "pv", "ps", "psh", "pc", "sk", "sv", "ss", "ssh", "sc")}
    kv_heads = lambda t, b: t.reshape(b, WINDOW, A_KV_HEADS, HEAD_DIM)
    for l in range(w["depth"]):
        i = l // 2
        grow = lambda j: l * 6 + j
        xp, xs = _ffn(xp, xs, ng, (grow(0), grow(1)), w["ffn_w_gu"], w["ffn_w_down"], (l, 0),
                      tp["ffn_m"], tp["ffn_f"])
        if l % 2 == 0:
            proj = _norm_mm(xp, ng, grow(2), w["ab_w_in"], i, tp["proj_m"], 1280)
            ao, nk, nv = _swa_seq(proj, w["attn_sinks"], i, cos_p, sin_p, bp, seq)
            yb, st = _rwkv_scan(proj, w["rwkv_in"], w["rwkv_rows"], i, bp, RWKV_CHUNK * RWKV_STEP_CHUNKS)
            xp = _ab_out(xp, ao, yb, w["ab_w_out"], ng, grow(3), i, tp["mix_m"])
            outs["pk"].append(kv_heads(nk, bp))
            outs["pv"].append(kv_heads(nv, bp))
            outs["ps"].append(st)
            outs["psh"].append(proj.reshape(bp, seq, AB_COLS_PAD)[:, -1, A_COLS:AB_COLS])

            proj = _norm_mm(xs, ng, grow(2), w["ab_w_in"], i, ts["proj_m"], 1280)
            ao, nk, nv = _swa_step(proj, w["attn_sinks"], i, cache_k, cache_v, cos_s, sin_s, min(8, bs))
            prev = jnp.pad(shift[i], ((0, 0), (0, B_COLS_PAD - B_COLS)))
            r, k, v, lw, a, gt = _rwkv_in_step(proj, prev, w["rwkv_in"], i)
            yb, st = _rwkv_step(r, k, v, lw, a, gt, state, w["rwkv_heads"], i, min(8, bs))
            xs = _ab_out(xs, ao, yb, w["ab_w_out"], ng, grow(3), i, ts["mix_m"])
            outs["sk"].append(kv_heads(nk, bs))
            outs["sv"].append(kv_heads(nv, bs))
            outs["ss"].append(st)
            outs["ssh"].append(proj[:, A_COLS:AB_COLS])
        else:
            bg, u = _conv_in(xp, ng, grow(2), w["conv_w_in"], i, tp["proj_m"], 512)
            xp = _conv_out_seq(xp, bg, u, w["conv_w"], w["conv_w_out"], ng, grow(3), i, tp["mix_m"], seq)
            outs["pc"].append(u.reshape(bp, seq, -1)[:, -(CONV_W - 1):])

            bg, u = _conv_in(xs, ng, grow(2), w["conv_w_in"], i, ts["proj_m"], 512)
            buf = conv_buf[i]
            xs = _conv_out_step(xs, bg, u, buf[:, 1], buf[:, 0], w["conv_w"], w["conv_w_out"], ng, grow(3), i)
            outs["sc"].append(jnp.stack([buf[:, 1], u], axis=1))
        xp, xs = _ffn(xp, xs, ng, (grow(4), grow(5)), w["ffn_w_gu"], w["ffn_w_down"], (l, 1),
                      tp["ffn_m"], tp["ffn_f"])
    stacked = {key: jnp.stack(val) for key, val in outs.items()}
    return xp.reshape(bp, seq, d), xs.reshape(bs, 1, d), stacked


def _prep_weights(norm_g, ffn_w_gu, ffn_w_down, ab_w_in, ab_w_out, attn_sinks, rwkv_mu, rwkv_w0, rwkv_w_decay,
                  rwkv_a0, rwkv_w_aaa, rwkv_w_gate, rwkv_k_k, rwkv_k_a, rwkv_r_k, rwkv_gn_w, rwkv_gn_b,
                  conv_w_in, conv_w, conv_w_out):
    depth, n_norm, d = norm_g.shape
    padc = lambda t, n: jnp.pad(t, [(0, 0)] * (t.ndim - 1) + [(0, n - t.shape[-1])])
    row = lambda t: t[:, None, :]
    wdec = jnp.pad(rwkv_w_decay, ((0, 0), (0, D_AAA), (0, 0))).astype(BF16)
    waaa = jnp.pad(rwkv_w_aaa, ((0, 0), (D_DECAY, 0), (0, 0))).astype(BF16)
    wgate = jnp.pad(rwkv_w_gate, ((0, 0), (0, GATE_PAD - D_GATE), (0, 0))).astype(BF16)
    return {
        "depth": depth,
        "norm_g": norm_g.reshape(depth * n_norm, 1, d),
        "ffn_w_gu": ffn_w_gu,
        "ffn_w_down": ffn_w_down,
        "ab_w_in": padc(ab_w_in, AB_COLS_PAD).astype(BF16),
        "ab_w_out": ab_w_out.astype(BF16),
        "attn_sinks": attn_sinks,
        "rwkv_in": (row(padc(rwkv_mu, B_COLS_PAD)), row(rwkv_w0), wdec, row(rwkv_a0), waaa, wgate),
        "rwkv_rows": tuple(row(t) for t in (rwkv_k_k, rwkv_k_a, rwkv_r_k, rwkv_gn_w, rwkv_gn_b)),
        "rwkv_heads": tuple(t.reshape(-1, B_HEADS, 1, HEAD_DIM)
                            for t in (rwkv_k_k, rwkv_k_a, rwkv_r_k, rwkv_gn_w, rwkv_gn_b)),
        "conv_w_in": conv_w_in,
        "conv_w": conv_w,
        "conv_w_out": conv_w_out.astype(BF16),
    }


def kernel(x_prompt, x_sample, cache_swa_k, cache_swa_v, state_rwkv, state_rwkv_shift, state_conv, norm_g, ffn_w_gu, ffn_w_down, ab_w_in, ab_w_out, attn_sinks, rwkv_mu, rwkv_w0, rwkv_w_decay, rwkv_a0, rwkv_w_aaa, rwkv_w_gate, rwkv_k_k, rwkv_k_a, rwkv_r_k, rwkv_gn_w, rwkv_gn_b, conv_w_in, conv_w, conv_w_out):
    w = _prep_weights(norm_g, ffn_w_gu, ffn_w_down, ab_w_in, ab_w_out, attn_sinks, rwkv_mu, rwkv_w0, rwkv_w_decay,
                      rwkv_a0, rwkv_w_aaa, rwkv_w_gate, rwkv_k_k, rwkv_k_a, rwkv_r_k, rwkv_gn_w, rwkv_gn_b,
                      conv_w_in, conv_w, conv_w_out)
    n_ab, dec_batch = cache_swa_k.shape[:2]
    past = (cache_swa_k.reshape(n_ab, dec_batch, WINDOW, A_KV), cache_swa_v.reshape(n_ab, dec_batch, WINDOW, A_KV),
            state_rwkv, state_rwkv_shift, state_conv)
    y_prompt, y_sample, o = _trunk(x_prompt, x_sample, past, w)
    return (y_prompt, y_sample, o["pk"], o["pv"], o["ps"], o["psh"], o["pc"],
            o["sk"], o["sv"], o["ss"], o["ssh"], o["sc"])
```

```python
import functools

import jax
import jax.numpy as jnp
from jax import lax
from jax.experimental import pallas as pl
from jax.experimental.pallas import tpu as pltpu

F32 = jnp.float32
BF16 = jnp.bfloat16

HEAD_DIM = 64
A_HEADS = 16
A_KV_HEADS = 4
A_GROUP = A_HEADS // A_KV_HEADS
WINDOW = 128
ROPE_THETA = 10000.0
PAST_LEN = 16384
A_Q = A_HEADS * HEAD_DIM
A_KV = A_KV_HEADS * HEAD_DIM
A_COLS = A_Q + 2 * A_KV
B_HEADS = 16
B_WIDTH = B_HEADS * HEAD_DIM
D_DECAY = 64
D_AAA = 64
D_GATE = 160
B_COLS = 3 * B_WIDTH + D_DECAY + D_AAA + D_GATE
GN_EPS = 64e-5
AB_COLS = A_COLS + B_COLS
CONV_W = 3
NORM_EPS = 1e-6

LANE = 128
HEAD_PAIRS = B_WIDTH // LANE
AB_COLS_PAD = 5120
B_COLS_PAD = AB_COLS_PAD - A_COLS
LORA_IN = D_DECAY + D_AAA
GATE_PAD = AB_COLS_PAD - A_COLS - 3 * B_WIDTH - LORA_IN
NORM_ROWS = 16
NORM_UNROLL = 64
SWA_STEP_BLOCKS = 2
RWKV_CHUNK = 64
RWKV_STEP_CHUNKS = 4
RWKV_INV_BLOCK = 16
VMEM_LIMIT = 56 * 1024 * 1024
WIDE_VMEM_LIMIT = 60 * 1024 * 1024


def _tiles(m):
    return dict(ffn_m=min(1024, m), ffn_f=256 if m >= 1024 else 512, proj_m=min(1024, m), mix_m=min(512, m))


def _params(sem, vmem_limit=VMEM_LIMIT):
    return pltpu.CompilerParams(dimension_semantics=sem, vmem_limit_bytes=vmem_limit)


def _layer(arr, idx):
    idx = tuple(idx)
    rest = arr.shape[len(idx):]
    return pl.BlockSpec((None,) * len(idx) + rest, lambda *_: idx + (0,) * len(rest))


def _rms(x, g):
    return x * lax.rsqrt(jnp.mean(x * x, axis=-1, keepdims=True) + NORM_EPS) * g


def _dot(a, b):
    return jnp.dot(a, b, preferred_element_type=F32)


def _for_row_chunks(n_rows, body):
    size = min(NORM_ROWS, n_rows)
    assert n_rows % size == 0

    def step(c, carry):
        body(pl.ds(pl.multiple_of(c * size, size), size))
        return carry

    n = n_rows // size
    lax.fori_loop(0, n, step, 0, unroll=min(NORM_UNROLL, n))


def _norm_rows_into(h_ref, x_ref, g):
    def chunk(rows):
        h_ref[rows, :] = _rms(x_ref[rows, :], g).astype(BF16)

    _for_row_chunks(x_ref.shape[0], chunk)


def _add_normed_rows(o_ref, x_ref, g, scale=1.0):
    def chunk(rows):
        normed = _rms(o_ref[rows, :], g)
        o_ref[rows, :] = x_ref[rows, :] + (normed if scale == 1.0 else scale * normed)

    _for_row_chunks(x_ref.shape[0], chunk)


def _ffn_kernel(x_ref, xs_ref, g0_ref, g1_ref, wg_ref, wu_ref, wd_ref, o_ref, os_ref, h_ref, hs_ref):
    i = pl.program_id(0)
    f = pl.program_id(1)
    g0 = g0_ref[...]
    g1 = g1_ref[...]
    cast = lambda w_ref: w_ref[...].astype(BF16)

    def group(x_ref, o_ref, h_ref):
        def hidden_tile(accumulate):
            h = h_ref[...]
            gate = _dot(h, cast(wg_ref))
            act = (gate * jax.nn.sigmoid(gate) * _dot(h, cast(wu_ref))).astype(BF16)
            if accumulate:
                o_ref[...] += _dot(act, cast(wd_ref))
            else:
                o_ref[...] = _dot(act, cast(wd_ref))

        @pl.when(f == 0)
        def _():
            _norm_rows_into(h_ref, x_ref, g0)
            hidden_tile(False)

        @pl.when(f > 0)
        def _():
            hidden_tile(True)

        @pl.when(f == pl.num_programs(1) - 1)
        def _():
            _add_normed_rows(o_ref, x_ref, g1, 0.5)

    group(x_ref, o_ref, h_ref)

    @pl.when(i == 0)
    def _():
        group(xs_ref, os_ref, hs_ref)


def _ffn(x, xs, norm_g, g_rows, w_gu, w_down, lj, tm, tf):
    m, d = x.shape
    ms = xs.shape[0]
    l, j = lj
    nf = w_down.shape[2] // tf
    return pl.pallas_call(
        _ffn_kernel,
        grid=(m // tm, nf),
        in_specs=[
            pl.BlockSpec((tm, d), lambda i, f: (i, 0)),
            pl.BlockSpec((ms, d), lambda i, f: (0, 0)),
            _layer(norm_g, (g_rows[0],)),
            _layer(norm_g, (g_rows[1],)),
            pl.BlockSpec((None, None, d, tf), lambda i, f: (l, j, 0, f)),
            pl.BlockSpec((None, None, d, tf), lambda i, f: (l, j, 0, nf + f)),
            pl.BlockSpec((None, None, tf, d), lambda i, f: (l, j, f, 0)),
        ],
        out_specs=[pl.BlockSpec((tm, d), lambda i, f: (i, 0)),
                   pl.BlockSpec((ms, d), lambda i, f: (0, 0))],
        out_shape=[jax.ShapeDtypeStruct((m, d), F32), jax.ShapeDtypeStruct((ms, d), F32)],
        scratch_shapes=[pltpu.VMEM((tm, d), BF16), pltpu.VMEM((ms, d), BF16)],
        compiler_params=_params(("arbitrary", "arbitrary"), WIDE_VMEM_LIMIT),
        name="ffn",
    )(x, xs, norm_g, norm_g, w_gu, w_gu, w_down)


def _norm_mm_kernel(x_ref, g_ref, w_ref, o_ref, h_ref):
    @pl.when(pl.program_id(1) == 0)
    def _():
        _norm_rows_into(h_ref, x_ref, g_ref[...])

    o_ref[...] = _dot(h_ref[...], w_ref[...])


def _norm_mm(x, norm_g, g_row, w, li, tm, tn):
    m, d = x.shape
    n = w.shape[-1]
    return pl.pallas_call(
        _norm_mm_kernel,
        grid=(m // tm, n // tn),
        in_specs=[
            pl.BlockSpec((tm, d), lambda i, j: (i, 0)),
            _layer(norm_g, (g_row,)),
            pl.BlockSpec((None, d, tn), lambda i, j: (li, 0, j)),
        ],
        out_specs=pl.BlockSpec((tm, tn), lambda i, j: (i, j)),
        out_shape=jax.ShapeDtypeStruct((m, n), F32),
        scratch_shapes=[pltpu.VMEM((tm, d), BF16)],
        compiler_params=_params(("parallel", "arbitrary")),
        name="ab_in",
    )(x, norm_g, w)


def _conv_in_kernel(x_ref, g_ref, wb_ref, wc_ref, wh_ref, bg_ref, u_ref, h_ref):
    @pl.when(pl.program_id(1) == 0)
    def _():
        _norm_rows_into(h_ref, x_ref, g_ref[...])

    h = h_ref[...]
    bg_ref[...] = _dot(h, wb_ref[...].astype(BF16))
    u_ref[...] = _dot(h, wc_ref[...].astype(BF16)) * _dot(h, wh_ref[...].astype(BF16))


def _conv_in(x, norm_g, g_row, w_in, li, tm, tn):
    m, d = x.shape
    dc = w_in.shape[-1] // 3
    nj = dc // tn
    return pl.pallas_call(
        _conv_in_kernel,
        grid=(m // tm, nj),
        in_specs=[
            pl.BlockSpec((tm, d), lambda i, j: (i, 0)),
            _layer(norm_g, (g_row,)),
            pl.BlockSpec((None, d, tn), lambda i, j: (li, 0, j)),
            pl.BlockSpec((None, d, tn), lambda i, j: (li, 0, nj + j)),
            pl.BlockSpec((None, d, tn), lambda i, j: (li, 0, 2 * nj + j)),
        ],
        out_specs=[pl.BlockSpec((tm, tn), lambda i, j: (i, j)),
                   pl.BlockSpec((tm, tn), lambda i, j: (i, j))],
        out_shape=[jax.ShapeDtypeStruct((m, dc), F32), jax.ShapeDtypeStruct((m, dc), F32)],
        scratch_shapes=[pltpu.VMEM((tm, d), BF16)],
        compiler_params=_params(("parallel", "arbitrary"), WIDE_VMEM_LIMIT),
        name="conv_in",
    )(x, norm_g, w_in, w_in, w_in)


def _conv_taps(u, p1, p2, cw):
    return p2 * cw[0:1, :] + p1 * cw[1:2, :] + u * cw[2:3, :]


def _conv_out_seq_kernel(x_ref, bg_ref, u_ref, halo_ref, cw_ref, w_ref, g_ref, o_ref, *, tiles_per_seq):
    u = u_ref[...]
    first = (pl.program_id(0) % tiles_per_seq) == 0
    h1 = jnp.where(first, 0.0, halo_ref[7:8, :])
    h2 = jnp.where(first, 0.0, halo_ref[6:7, :])
    row = lax.broadcasted_iota(jnp.int32, u.shape, 0)
    p1 = jnp.where(row == 0, h1, pltpu.roll(u, 1, 0))
    p2 = jnp.where(row == 0, h2, jnp.where(row == 1, h1, pltpu.roll(u, 2, 0)))
    z = (bg_ref[...] * _conv_taps(u, p1, p2, cw_ref[...])).astype(BF16)
    o_ref[...] = _dot(z, w_ref[...])
    _add_normed_rows(o_ref, x_ref, g_ref[...])


def _conv_out_step_kernel(x_ref, bg_ref, u_ref, p1_ref, p2_ref, cw_ref, w_ref, g_ref, o_ref):
    z = (bg_ref[...] * _conv_taps(u_ref[...], p1_ref[...], p2_ref[...], cw_ref[...])).astype(BF16)
    o_ref[...] = _dot(z, w_ref[...])
    _add_normed_rows(o_ref, x_ref, g_ref[...])


def _conv_out_seq(x, bg, u, conv_w, w_out, norm_g, g_row, li, tm, seq):
    m, d = x.shape
    dc = u.shape[1]
    hb = tm // 8
    row_spec = pl.BlockSpec((tm, dc), lambda i: (i, 0))
    return pl.pallas_call(
        functools.partial(_conv_out_seq_kernel, tiles_per_seq=seq // tm),
        grid=(m // tm,),
        in_specs=[
            pl.BlockSpec((tm, d), lambda i: (i, 0)),
            row_spec, row_spec,
            pl.BlockSpec((8, dc), lambda i: (jnp.maximum(i * hb - 1, 0), 0)),
            _layer(conv_w, (li,)), _layer(w_out, (li,)), _layer(norm_g, (g_row,)),
        ],
        out_specs=pl.BlockSpec((tm, d), lambda i: (i, 0)),
        out_shape=jax.ShapeDtypeStruct((m, d), F32),
        compiler_params=_params(("parallel",)),
        name="conv_out",
    )(x, bg, u, u, conv_w, w_out, norm_g)


def _conv_out_step(x, bg, u, p1, p2, conv_w, w_out, norm_g, g_row, li):
    m, d = x.shape
    dc = u.shape[1]
    row_spec = pl.BlockSpec((m, dc), lambda i: (0, 0))
    return pl.pallas_call(
        _conv_out_step_kernel,
        grid=(1,),
        in_specs=[
            pl.BlockSpec((m, d), lambda i: (0, 0)),
            row_spec, row_spec, row_spec, row_spec,
            _layer(conv_w, (li,)), _layer(w_out, (li,)), _layer(norm_g, (g_row,)),
        ],
        out_specs=pl.BlockSpec((m, d), lambda i: (0, 0)),
        out_shape=jax.ShapeDtypeStruct((m, d), F32),
        compiler_params=_params(("arbitrary",)),
        name="conv_out_step",
    )(x, bg, u, p1, p2, conv_w, w_out, norm_g)


def _rope(x, cos, sin):
    width = x.shape[-1]
    half = HEAD_DIM // 2
    lane = lax.broadcasted_iota(jnp.int32, x.shape, x.ndim - 1)
    swapped = jnp.where((lane % HEAD_DIM) < half,
                        pltpu.roll(x, width - half, x.ndim - 1),
                        pltpu.roll(x, half, x.ndim - 1))
    return x * cos + swapped * sin


def _swa_seq_kernel(sink_ref, q_ref, kc_ref, kp_ref, vc_ref, vp_ref, cq_ref, sq_ref, cp_ref, sp_ref,
                    o_ref, nk_ref, nv_ref, *, li):
    n = pl.program_id(1)
    nblk = q_ref.shape[0] // WINDOW
    q = _rope(q_ref[...], cq_ref[...], sq_ref[...])
    kc = _rope(kc_ref[...], cq_ref[:, :A_KV], sq_ref[:, :A_KV])
    kp = _rope(kp_ref[...], cp_ref[...], sp_ref[...])
    kall = jnp.concatenate([kp, kc], axis=0).astype(BF16)
    vall = jnp.concatenate([vp_ref[...], vc_ref[...]], axis=0).astype(BF16)
    rows = A_GROUP * WINDOW
    head = lambda x, h: x[:, h * HEAD_DIM:(h + 1) * HEAD_DIM]
    batch = [(j, g) for j in range(nblk) for g in range(A_KV_HEADS)]
    blk = lambda x, j: x[j * WINDOW:(j + 1) * WINDOW]
    window = lambda x, j: x[j * WINDOW:(j + 2) * WINDOW]
    qg = jnp.stack([jnp.concatenate([head(blk(q, j), g * A_GROUP + i) for i in range(A_GROUP)], axis=0)
                    for j, g in batch], axis=0)
    sink = jnp.stack([jnp.concatenate([jnp.full((WINDOW, 1), sink_ref[li, g * A_GROUP + i], F32)
                                       for i in range(A_GROUP)], axis=0) for j, g in batch], axis=0)
    qi = lax.broadcasted_iota(jnp.int32, (1, rows, 2 * WINDOW), 1) % WINDOW
    kj = lax.broadcasted_iota(jnp.int32, (1, rows, 2 * WINDOW), 2)
    diff = WINDOW + qi - kj
    first = lax.broadcasted_iota(jnp.int32, (len(batch), 1, 1), 0) < A_KV_HEADS
    lo = jnp.where(first & (n == 0), WINDOW, 0)
    valid = (diff >= 0) & (diff <= WINDOW) & (kj >= lo)
    s = _bmm_nt(qg, jnp.stack([head(window(kall, j), g) for j, g in batch], axis=0))
    s = jnp.where(valid, s * (HEAD_DIM ** -0.5), -jnp.inf)
    mx = jnp.maximum(jnp.max(s, axis=-1, keepdims=True), sink)
    e = jnp.exp(s - mx)
    p = e / (jnp.sum(e, axis=-1, keepdims=True) + jnp.exp(sink - mx))
    og = _bmm(p, jnp.stack([head(window(vall, j), g) for j, g in batch], axis=0))
    o_ref[...] = jnp.concatenate(
        [jnp.concatenate([og[j * A_KV_HEADS + h // A_GROUP][(h % A_GROUP) * WINDOW:(h % A_GROUP + 1) * WINDOW]
                          for h in range(A_HEADS)], axis=-1) for j in range(nblk)], axis=0)

    @pl.when(n == pl.num_programs(1) - 1)
    def _():
        nk_ref[...] = blk(kc, nblk - 1)
        nv_ref[...] = vc_ref[(nblk - 1) * WINDOW:nblk * WINDOW, :]


def _swa_seq(proj, sinks, li, cos, sin, batch, seq):
    rows = SWA_STEP_BLOCKS * WINDOW
    ns = seq // rows
    kcol = A_Q // A_KV
    vcol = kcol + 1
    cur = lambda b, n: b * ns + n
    prev = lambda b, n: (b * ns + n) * SWA_STEP_BLOCKS - jnp.minimum(n, 1)
    tprev = lambda b, n: jnp.maximum(n * SWA_STEP_BLOCKS - 1, 0)
    return pl.pallas_call(
        functools.partial(_swa_seq_kernel, li=li),
        grid=(batch, ns),
        in_specs=[
            pl.BlockSpec(memory_space=pltpu.SMEM),
            pl.BlockSpec((rows, A_Q), lambda b, n: (cur(b, n), 0)),
            pl.BlockSpec((rows, A_KV), lambda b, n: (cur(b, n), kcol)),
            pl.BlockSpec((WINDOW, A_KV), lambda b, n: (prev(b, n), kcol)),
            pl.BlockSpec((rows, A_KV), lambda b, n: (cur(b, n), vcol)),
            pl.BlockSpec((WINDOW, A_KV), lambda b, n: (prev(b, n), vcol)),
            pl.BlockSpec((rows, A_Q), lambda b, n: (n, 0)),
            pl.BlockSpec((rows, A_Q), lambda b, n: (n, 0)),
            pl.BlockSpec((WINDOW, A_KV), lambda b, n: (tprev(b, n), 0)),
            pl.BlockSpec((WINDOW, A_KV), lambda b, n: (tprev(b, n), 0)),
        ],
        out_specs=[
            pl.BlockSpec((rows, A_Q), lambda b, n: (cur(b, n), 0)),
            pl.BlockSpec((WINDOW, A_KV), lambda b, n: (b, 0)),
            pl.BlockSpec((WINDOW, A_KV), lambda b, n: (b, 0)),
        ],
        out_shape=[
            jax.ShapeDtypeStruct((batch * seq, A_Q), F32),
            jax.ShapeDtypeStruct((batch * WINDOW, A_KV), F32),
            jax.ShapeDtypeStruct((batch * WINDOW, A_KV), F32),
        ],
        compiler_params=_params(("parallel", "arbitrary")),
        name="swa_seq",
    )(sinks, proj, proj, proj, proj, proj, cos, sin, cos, sin)


def _swa_step_kernel(sink_ref, p_ref, ck_ref, cv_ref, cos_ref, sin_ref, o_ref, nk_ref, nv_ref, *, li):
    sb = p_ref.shape[0]
    pairs = A_KV // LANE
    per_pair = A_HEADS // pairs
    q = _rope(p_ref[:, 0:A_Q], cos_ref[...], sin_ref[...]) * (HEAD_DIM ** -0.5)
    kn = _rope(p_ref[:, A_Q:A_Q + A_KV], cos_ref[:, :A_KV], sin_ref[:, :A_KV])
    vn = p_ref[:, A_Q + A_KV:A_COLS]
    zero = jnp.zeros((1, HEAD_DIM), F32)

    def q_row(s, h):
        piece = q[s:s + 1, h * HEAD_DIM:(h + 1) * HEAD_DIM]
        return jnp.concatenate([piece, zero] if (h // A_GROUP) % 2 == 0 else [zero, piece], axis=-1)

    batch = [(s, p) for s in range(sb) for p in range(pairs)]
    qb = jnp.stack([jnp.concatenate([q_row(s, p * per_pair + r) for r in range(per_pair)], axis=0)
                    for s, p in batch], axis=0)
    lanes = lambda x, s, p: x[s:s + 1, p * LANE:(p + 1) * LANE]
    knb = jnp.stack([lanes(kn, s, p) for s, p in batch], axis=0)
    vnb = jnp.stack([lanes(vn, s, p) for s, p in batch], axis=0)
    kb = jnp.stack([ck_ref[s, :, p * LANE:(p + 1) * LANE] for s, p in batch], axis=0)
    vb = jnp.stack([cv_ref[s, :, p * LANE:(p + 1) * LANE] for s, p in batch], axis=0)
    sink = jnp.stack([jnp.concatenate([jnp.full((1, 1), sink_ref[li, p * per_pair + r], F32)
                                       for r in range(per_pair)], axis=0) for s, p in batch], axis=0)
    s_old = _bmm_nt(qb, kb)
    s_new = jnp.sum(qb * knb, axis=-1, keepdims=True)
    mx = jnp.maximum(jnp.maximum(jnp.max(s_old, axis=-1, keepdims=True), s_new), sink)
    e = jnp.exp(s_old - mx)
    e_new = jnp.exp(s_new - mx)
    den = jnp.sum(e, axis=-1, keepdims=True) + e_new + jnp.exp(sink - mx)
    ob = (_bmm(e, vb) + e_new * vnb) / den

    def o_piece(s, h):
        row = ob[s * pairs + h // per_pair][h % per_pair:h % per_pair + 1]
        half = (h // A_GROUP) % 2
        return row[:, half * HEAD_DIM:(half + 1) * HEAD_DIM]

    o_ref[...] = jnp.concatenate(
        [jnp.concatenate([o_piece(s, h) for h in range(A_HEADS)], axis=-1) for s in range(sb)], axis=0)
    nk_ref[:, 0:WINDOW - 1, :] = ck_ref[:, 1:WINDOW, :]
    nv_ref[:, 0:WINDOW - 1, :] = cv_ref[:, 1:WINDOW, :]
    for s in range(sb):
        nk_ref[s, WINDOW - 1:WINDOW, :] = kn[s:s + 1, :]
        nv_ref[s, WINDOW - 1:WINDOW, :] = vn[s:s + 1, :]


def _swa_step(proj, sinks, li, cache_k, cache_v, cos, sin, sb):
    batch, width = proj.shape
    cache_in = pl.BlockSpec((None, sb, WINDOW, A_KV), lambda b: (li, b, 0, 0))
    cache_out = pl.BlockSpec((sb, WINDOW, A_KV), lambda b: (b, 0, 0))
    return pl.pallas_call(
        functools.partial(_swa_step_kernel, li=li),
        grid=(batch // sb,),
        in_specs=[
            pl.BlockSpec(memory_space=pltpu.SMEM),
            pl.BlockSpec((sb, width), lambda b: (b, 0)),
            cache_in, cache_in,
            pl.BlockSpec((1, A_Q), lambda b: (0, 0)),
            pl.BlockSpec((1, A_Q), lambda b: (0, 0)),
        ],
        out_specs=[pl.BlockSpec((sb, A_Q), lambda b: (b, 0)), cache_out, cache_out],
        out_shape=[
            jax.ShapeDtypeStruct((batch, A_Q), F32),
            jax.ShapeDtypeStruct((batch, WINDOW, A_KV), F32),
            jax.ShapeDtypeStruct((batch, WINDOW, A_KV), F32),
        ],
        compiler_params=_params(("parallel",)),
        name="swa_step",
    )(sinks, proj, cache_k, cache_v, cos, sin)


def _softplus(z):
    return jnp.maximum(z, 0.0) + jnp.log(1.0 + jnp.exp(-jnp.abs(z)))


def _rwkv_in_math(pb, prev, mu, w0, wdec, a0, waaa, wgate):
    xm = pb + (prev - pb) * mu
    r = xm[:, 0:B_WIDTH]
    k = xm[:, B_WIDTH:2 * B_WIDTH]
    v = xm[:, 2 * B_WIDTH:3 * B_WIDTH]
    lora = xm[:, 3 * B_WIDTH:3 * B_WIDTH + LORA_IN]
    dg = xm[:, 3 * B_WIDTH + LORA_IN:]
    w_log = -_softplus(-(w0 + _dot(jnp.tanh(lora).astype(BF16), wdec))) - 0.5
    lw = -jnp.exp(w_log)
    a = jax.nn.sigmoid(a0 + _dot(lora.astype(BF16), waaa))
    g = _dot(jax.nn.sigmoid(dg).astype(BF16), wgate)
    return r, k, v, lw, a, g


def _rwkv_in_step_kernel(p_ref, prev_ref, mu_ref, w0_ref, wdec_ref, a0_ref, waaa_ref, wgate_ref, *out_refs):
    vals = _rwkv_in_math(p_ref[:, A_COLS:], prev_ref[...], mu_ref[...], w0_ref[...], wdec_ref[...], a0_ref[...],
                         waaa_ref[...], wgate_ref[...])
    for ref, val in zip(out_refs, vals):
        ref[...] = val


def _rwkv_in_step(proj, prev, wts, li):
    m, width = proj.shape
    out_spec = pl.BlockSpec((m, B_WIDTH), lambda i: (0, 0))
    return pl.pallas_call(
        _rwkv_in_step_kernel,
        grid=(1,),
        in_specs=[pl.BlockSpec((m, width), lambda i: (0, 0)), pl.BlockSpec((m, B_COLS_PAD), lambda i: (0, 0))]
        + [_layer(a, (li,)) for a in wts],
        out_specs=[out_spec] * 6,
        out_shape=[jax.ShapeDtypeStruct((m, B_WIDTH), F32)] * 6,
        compiler_params=_params(("arbitrary",)),
        name="rwkv_in_step",
    )(proj, prev, *wts)


def _bdg(a, b, dn):
    return lax.dot_general(a.astype(BF16), b.astype(BF16), dn, preferred_element_type=F32)


def _bmm(a, b):
    return _bdg(a, b, (((2,), (1,)), ((0,), (0,))))


def _bmm_nt(a, b):
    return _bdg(a, b, (((2,), (2,)), ((0,), (0,))))


def _bmm_tn(a, b):
    return _bdg(a, b, (((1,), (1,)), ((0,), (0,))))


def _unit_lower_inverse(a, eye, same_block):
    d = jnp.where(same_block, a, 0.0)
    e = a - d
    x = eye + d
    dp = d
    p = 1
    while 2 * p < RWKV_INV_BLOCK:
        dp = _bmm(dp, dp)
        x = _bmm(x, eye + dp)
        p *= 2
    nn = _bmm(x, e)
    return _bmm(_bmm(eye + nn, eye + _bmm(nn, nn)), x)


def _head_sums(x, first_half):
    outs = []
    for p in range(HEAD_PAIRS):
        xp = x[:, p * LANE:(p + 1) * LANE]
        s0 = jnp.sum(jnp.where(first_half, xp, 0.0), axis=-1, keepdims=True)
        s1 = jnp.sum(jnp.where(first_half, 0.0, xp), axis=-1, keepdims=True)
        outs.append(jnp.where(first_half, s0, s1))
    return jnp.concatenate(outs, axis=-1)


def _rwkv_scan_kernel(p_ref, halo_ref, mu_ref, w0_ref, wdec_ref, a0_ref, waaa_ref, wgate_ref,
                      kk_ref, ka_ref, rk_ref, gw_ref, gb_ref, y_ref, s_out_ref, st_ref):
    c = pl.program_id(1)

    @pl.when(c == 0)
    def _():
        st_ref[...] = jnp.zeros_like(st_ref)

    pb = p_ref[:, A_COLS:]
    hrow = jnp.where(c == 0, 0.0, halo_ref[7:8, A_COLS:])
    prev = jnp.where(lax.broadcasted_iota(jnp.int32, pb.shape, 0) == 0, hrow, pltpu.roll(pb, 1, 0))
    r, k, v, lw, a, gate = _rwkv_in_math(pb, prev, mu_ref[...], w0_ref[...], wdec_ref[...], a0_ref[...],
                                         waaa_ref[...], wgate_ref[...])
    rows = r.shape[0]
    cs = RWKV_CHUNK
    chunks = rows // cs
    c2 = 2 * cs
    iota = lambda shape, dim: lax.broadcasted_iota(jnp.int32, shape, dim)

    ti = iota((rows, rows), 0)
    si = iota((rows, rows), 1)
    tril = jnp.where((si <= ti) & (si // cs == ti // cs), 1.0, 0.0).astype(BF16)
    lw_hi = lw.astype(BF16)
    rem = lw - lw_hi.astype(F32)
    lw_mid = rem.astype(BF16)
    lw_lo = (rem - lw_mid.astype(F32)).astype(BF16)
    lcum = _dot(tril, lw_hi) + (_dot(tril, lw_mid) + _dot(tril, lw_lo))
    p_t = jnp.exp(lcum)
    p_inv = jnp.exp(-lcum)
    p_prev = jnp.exp(lcum - lw)
    p_ends = [jnp.exp(lcum[(j + 1) * cs - 1:(j + 1) * cs, :]) for j in range(chunks)]
    p_end = jnp.concatenate([jnp.broadcast_to(pe, (cs, B_WIDTH)) for pe in p_ends], axis=0)

    first_half = iota((rows, LANE), 1) < HEAD_DIM
    kk = k * kk_ref[...]
    kk = kk / jnp.maximum(jnp.sqrt(_head_sums(kk * kk, first_half)), 1e-12)
    k2 = k * (1.0 + (a - 1.0) * ka_ref[...])
    al = -kk * p_prev
    be = kk * a * p_inv
    kt = k2 * p_inv
    rt = r * p_t
    be_end = be * p_end
    kt_end = kt * p_end
    bonus = _head_sums(r * k2 * rk_ref[...], first_half) * v

    np_, one = HEAD_PAIRS, (1,)
    own_half = (iota(one + (c2, LANE), 1) // cs) == (iota(one + (c2, LANE), 2) // HEAD_DIM)
    t2 = iota(one + (c2, c2), 1)
    s2 = iota(one + (c2, c2), 2)
    same_head = (t2 // cs) == (s2 // cs)
    strict = same_head & (s2 < t2)
    lower = same_head & (s2 <= t2)
    same_block = (t2 // RWKV_INV_BLOCK) == (s2 // RWKV_INV_BLOCK)
    eye = jnp.where(t2 == s2, 1.0, 0.0)
    kl = iota(one + (LANE, LANE), 1)
    vl = iota(one + (LANE, LANE), 2)
    st_diag = kl == vl
    st_blocks = (kl // HEAD_DIM) == (vl // HEAD_DIM)

    pairs = lambda x: jnp.stack([x[j * cs:(j + 1) * cs, p * LANE:(p + 1) * LANE]
                                 for j in range(chunks) for p in range(np_)], axis=0)
    dup = lambda x: jnp.concatenate([x, x], axis=1)
    own = lambda x: jnp.where(own_half, dup(pairs(x)), 0.0).astype(BF16)
    al2 = own(al)
    rt2 = own(rt)
    v2 = own(v)
    prod = _bmm_nt(jnp.concatenate([al2, rt2], axis=1),
                   jnp.concatenate([dup(pairs(be)), dup(pairs(kt))], axis=1))
    a_ab = jnp.where(strict, prod[:, :c2, :c2], 0.0)
    a_ak = jnp.where(strict, prod[:, :c2, c2:], 0.0)
    m_rb = jnp.where(lower, prod[:, c2:, :c2], 0.0)
    m_rk = jnp.where(lower, prod[:, c2:, c2:], 0.0)
    tinv = _unit_lower_inverse(a_ab, eye, same_block)
    be_end_p, kt_end_p, v_p = pairs(be_end), pairs(kt_end), pairs(v)

    st = st_ref[...]
    ys = []
    for j in range(chunks):
        of = lambda x: x[j * np_:(j + 1) * np_]
        u2 = _bmm(of(tinv), _bmm(of(al2), st) + _bmm(of(a_ak), of(v2)))
        y2 = _bmm(of(rt2), st) + _bmm(of(m_rb), u2) + _bmm(of(m_rk), of(v2))
        y3 = y2[:, :cs] + y2[:, cs:]
        u = u2[:, :cs] + u2[:, cs:]
        pe = jnp.stack([p_ends[j][:, p * LANE:(p + 1) * LANE] for p in range(np_)], axis=0)
        p_col = jnp.sum(jnp.where(st_diag, pe, 0.0), axis=2, keepdims=True)
        inc = _bmm_tn(of(be_end_p), u) + _bmm_tn(of(kt_end_p), of(v_p))
        st = st * p_col + jnp.where(st_blocks, inc, 0.0)
        ys.append(jnp.concatenate([y3[p] for p in range(np_)], axis=-1))
    st_ref[...] = st

    y = jnp.concatenate(ys, axis=0)
    mean = _head_sums(y, first_half) * (1.0 / HEAD_DIM)
    yc = y - mean
    var = _head_sums(yc * yc, first_half) * (1.0 / HEAD_DIM)
    y_ref[...] = (yc * lax.rsqrt(var + GN_EPS) * gw_ref[...] + gb_ref[...] + bonus) * gate

    @pl.when(c == pl.num_programs(1) - 1)
    def _():
        for p in range(np_):
            sp = st_ref[p].T
            s_out_ref[0, 2 * p] = sp[:HEAD_DIM, :HEAD_DIM]
            s_out_ref[0, 2 * p + 1] = sp[HEAD_DIM:, HEAD_DIM:]


def _rwkv_scan(proj, in_wts, head_wts, li, batch, cs):
    m, width = proj.shape
    nc = m // batch // cs
    st_spec = pl.BlockSpec((1, B_HEADS, HEAD_DIM, HEAD_DIM), lambda b, c: (b, 0, 0, 0))
    return pl.pallas_call(
        _rwkv_scan_kernel,
        grid=(batch, nc),
        in_specs=[pl.BlockSpec((cs, width), lambda b, c: (b * nc + c, 0)),
                  pl.BlockSpec((8, width), lambda b, c: (jnp.maximum((b * nc + c) * (cs // 8) - 1, 0), 0))]
        + [_layer(t, (li,)) for t in in_wts + head_wts],
        out_specs=[pl.BlockSpec((cs, B_WIDTH), lambda b, c: (b * nc + c, 0)), st_spec],
        out_shape=[jax.ShapeDtypeStruct((m, B_WIDTH), F32),
                   jax.ShapeDtypeStruct((batch, B_HEADS, HEAD_DIM, HEAD_DIM), F32)],
        scratch_shapes=[pltpu.VMEM((HEAD_PAIRS, LANE, LANE), F32)],
        compiler_params=_params(("parallel", "arbitrary")),
        name="rwkv_scan",
    )(proj, proj, *in_wts, *head_wts)


def _rwkv_step_kernel(r_ref, k_ref, v_ref, lw_ref, a_ref, g_ref, s_ref, kk_ref, ka_ref, rk_ref, gw_ref, gb_ref,
                      y_ref, s_out_ref):
    bb = r_ref.shape[0]
    n = bb * B_HEADS

    def heads(ref):
        x = ref[...]
        return jnp.stack([x[b:b + 1, h * HEAD_DIM:(h + 1) * HEAD_DIM]
                          for b in range(bb) for h in range(B_HEADS)], axis=0)

    per_seq = lambda ref: jnp.concatenate([ref[...]] * bb, axis=0)
    r, k, v, lw, a = heads(r_ref), heads(k_ref), heads(v_ref), heads(lw_ref), heads(a_ref)
    s = s_ref[...].reshape(n, HEAD_DIM, HEAD_DIM)
    kk = k * per_seq(kk_ref)
    kk = kk / jnp.maximum(jnp.sqrt(jnp.sum(kk * kk, axis=-1, keepdims=True)), 1e-12)
    k2 = k * (1.0 + (a - 1.0) * per_seq(ka_ref))
    eye = (lax.broadcasted_iota(jnp.int32, (1, HEAD_DIM, HEAD_DIM), 1)
           == lax.broadcasted_iota(jnp.int32, (1, HEAD_DIM, HEAD_DIM), 2))
    sa = jnp.sum(s * (-kk), axis=-1, keepdims=True)
    v_col = jnp.sum(jnp.where(eye, v, 0.0), axis=-1, keepdims=True)
    s_new = s * jnp.exp(lw) + sa * (kk * a) + v_col * k2
    y_col = jnp.sum(s_new * r, axis=-1, keepdims=True)
    y = jnp.sum(jnp.where(eye, y_col, 0.0), axis=1, keepdims=True)
    mean = jnp.mean(y, axis=-1, keepdims=True)
    yc = y - mean
    var = jnp.mean(yc * yc, axis=-1, keepdims=True)
    out = (yc * lax.rsqrt(var + GN_EPS) * per_seq(gw_ref) + per_seq(gb_ref)
           + jnp.sum(r * k2 * per_seq(rk_ref), axis=-1, keepdims=True) * v)
    s_out_ref[...] = s_new.reshape(bb, B_HEADS, HEAD_DIM, HEAD_DIM)
    y_ref[...] = g_ref[...] * jnp.concatenate(
        [jnp.concatenate([out[b * B_HEADS + h] for h in range(B_HEADS)], axis=-1) for b in range(bb)], axis=0)


def _rwkv_step(r, k, v, lw, a, gate, state, head_wts, li, bb):
    batch = r.shape[0]
    row_spec = pl.BlockSpec((bb, B_WIDTH), lambda b: (b, 0))
    st_shape = (bb, B_HEADS, HEAD_DIM, HEAD_DIM)
    return pl.pallas_call(
        _rwkv_step_kernel,
        grid=(batch // bb,),
        in_specs=[row_spec] * 6 + [pl.BlockSpec((None,) + st_shape, lambda b: (li, b, 0, 0, 0))]
        + [_layer(t, (li,)) for t in head_wts],
        out_specs=[row_spec, pl.BlockSpec(st_shape, lambda b: (b, 0, 0, 0))],
        out_shape=[jax.ShapeDtypeStruct((batch, B_WIDTH), F32),
                   jax.ShapeDtypeStruct((batch, B_HEADS, HEAD_DIM, HEAD_DIM), F32)],
        compiler_params=_params(("parallel",)),
        name="rwkv_step",
    )(r, k, v, lw, a, gate, state, *head_wts)


def _ab_out_kernel(x_ref, ao_ref, yb_ref, wa_ref, wb_ref, g_ref, o_ref):
    mix = _dot(ao_ref[...].astype(BF16), wa_ref[...])
    mix += _dot(yb_ref[...].astype(BF16), wb_ref[...])
    o_ref[...] = mix
    _add_normed_rows(o_ref, x_ref, g_ref[...])


def _ab_out(x, ao, yb, w_out, norm_g, g_row, li, tm):
    m, d = x.shape
    return pl.pallas_call(
        _ab_out_kernel,
        grid=(m // tm,),
        in_specs=[
            pl.BlockSpec((tm, d), lambda i: (i, 0)),
            pl.BlockSpec((tm, A_Q), lambda i: (i, 0)),
            pl.BlockSpec((tm, B_WIDTH), lambda i: (i, 0)),
            pl.BlockSpec((None, A_Q, d), lambda i: (li, 0, 0)),
            pl.BlockSpec((None, B_WIDTH, d), lambda i: (li, 1, 0)),
            _layer(norm_g, (g_row,)),
        ],
        out_specs=pl.BlockSpec((tm, d), lambda i: (i, 0)),
        out_shape=jax.ShapeDtypeStruct((m, d), F32),
        compiler_params=_params(("parallel",)),
        name="ab_out",
    )(x, ao, yb, w_out, w_out, norm_g)


def _rope_tables(pos):
    half = HEAD_DIM // 2
    freqs = ROPE_THETA ** (-jnp.arange(half, dtype=F32) / half)
    ang = pos.astype(F32)[:, None] * freqs[None, :]
    cos = jnp.cos(ang)
    sin = jnp.sin(ang)
    return (jnp.tile(jnp.concatenate([cos, cos], axis=-1), (1, A_HEADS)),
            jnp.tile(jnp.concatenate([-sin, sin], axis=-1), (1, A_HEADS)))


def _trunk(xp3, xs3, past, w):
    bp, seq, d = xp3.shape
    bs = xs3.shape[0]
    mp = bp * seq
    xp = xp3.reshape(mp, d)
    xs = xs3.reshape(bs, d)
    tp = _tiles(mp)
    ts = _tiles(bs)
    assert all(seq % n == 0 for n in (tp["mix_m"], RWKV_CHUNK * RWKV_STEP_CHUNKS, WINDOW * SWA_STEP_BLOCKS))
    assert mp % tp["ffn_m"] == 0
    cos_p, sin_p = _rope_tables(jnp.arange(seq, dtype=jnp.int32))
    cos_s, sin_s = _rope_tables(PAST_LEN + jnp.arange(1, dtype=jnp.int32))
    cache_k, cache_v, state, shift, conv_buf = past
    ng = w["norm_g"]
    outs = {key: [] for key in ("pk", "pv", "ps", "psh", "pc", "sk", "sv", "ss", "ssh", "sc")}
    kv_heads = lambda t, b: t.reshape(b, WINDOW, A_KV_HEADS, HEAD_DIM)
    for l in range(w["depth"]):
        i = l // 2
        grow = lambda j: l * 6 + j
        xp, xs = _ffn(xp, xs, ng, (grow(0), grow(1)), w["ffn_w_gu"], w["ffn_w_down"], (l, 0),
                      tp["ffn_m"], tp["ffn_f"])
        if l % 2 == 0:
            proj = _norm_mm(xp, ng, grow(2), w["ab_w_in"], i, tp["proj_m"], 1280)
            ao, nk, nv = _swa_seq(proj, w["attn_sinks"], i, cos_p, sin_p, bp, seq)
            yb, st = _rwkv_scan(proj, w["rwkv_in"], w["rwkv_rows"], i, bp, RWKV_CHUNK * RWKV_STEP_CHUNKS)
            xp = _ab_out(xp, ao, yb, w["ab_w_out"], ng, grow(3), i, tp["mix_m"])
            outs["pk"].append(kv_heads(nk, bp))
            outs["pv"].append(kv_heads(nv, bp))
            outs["ps"].append(st)
            outs["psh"].append(proj.reshape(bp, seq, AB_COLS_PAD)[:, -1, A_COLS:AB_COLS])

            proj = _norm_mm(xs, ng, grow(2), w["ab_w_in"], i, ts["proj_m"], 1280)
            ao, nk, nv = _swa_step(proj, w["attn_sinks"], i, cache_k, cache_v, cos_s, sin_s, min(8, bs))
            prev = jnp.pad(shift[i], ((0, 0), (0, B_COLS_PAD - B_COLS)))
            r, k, v, lw, a, gt = _rwkv_in_step(proj, prev, w["rwkv_in"], i)
            yb, st = _rwkv_step(r, k, v, lw, a, gt, state, w["rwkv_heads"], i, min(8, bs))
            xs = _ab_out(xs, ao, yb, w["ab_w_out"], ng, grow(3), i, ts["mix_m"])
            outs["sk"].append(kv_heads(nk, bs))
            outs["sv"].append(kv_heads(nv, bs))
            outs["ss"].append(st)
            outs["ssh"].append(proj[:, A_COLS:AB_COLS])
        else:
            bg, u = _conv_in(xp, ng, grow(2), w["conv_w_in"], i, tp["proj_m"], 512)
            xp = _conv_out_seq(xp, bg, u, w["conv_w"], w["conv_w_out"], ng, grow(3), i, tp["mix_m"], seq)
            outs["pc"].append(u.reshape(bp, seq, -1)[:, -(CONV_W - 1):])

            bg, u = _conv_in(xs, ng, grow(2), w["conv_w_in"], i, ts["proj_m"], 512)
            buf = conv_buf[i]
            xs = _conv_out_step(xs, bg, u, buf[:, 1], buf[:, 0], w["conv_w"], w["conv_w_out"], ng, grow(3), i)
            outs["sc"].append(jnp.stack([buf[:, 1], u], axis=1))
        xp, xs = _ffn(xp, xs, ng, (grow(4), grow(5)), w["ffn_w_gu"], w["ffn_w_down"], (l, 1),
                      tp["ffn_m"], tp["ffn_f"])
    stacked = {key: jnp.stack(val) for key, val in outs.items()}
    return xp.reshape(bp, seq, d), xs.reshape(bs, 1, d), stacked


def _prep_weights(norm_g, ffn_w_gu, ffn_w_down, ab_w_in, ab_w_out, attn_sinks, rwkv_mu, rwkv_w0, rwkv_w_decay,
                  rwkv_a0, rwkv_w_aaa, rwkv_w_gate, rwkv_k_k, rwkv_k_a, rwkv_r_k, rwkv_gn_w, rwkv_gn_b,
                  conv_w_in, conv_w, conv_w_out):
    depth, n_norm, d = norm_g.shape
    padc = lambda t, n: jnp.pad(t, [(0, 0)] * (t.ndim - 1) + [(0, n - t.shape[-1])])
    row = lambda t: t[:, None, :]
    wdec = jnp.pad(rwkv_w_decay, ((0, 0), (0, D_AAA), (0, 0))).astype(BF16)
    waaa = jnp.pad(rwkv_w_aaa, ((0, 0), (D_DECAY, 0), (0, 0))).astype(BF16)
    wgate = jnp.pad(rwkv_w_gate, ((0, 0), (0, GATE_PAD - D_GATE), (0, 0))).astype(BF16)
    return {
        "depth": depth,
        "norm_g": norm_g.reshape(depth * n_norm, 1, d),
        "ffn_w_gu": ffn_w_gu,
        "ffn_w_down": ffn_w_down,
        "ab_w_in": padc(ab_w_in, AB_COLS_PAD).astype(BF16),
        "ab_w_out": ab_w_out.astype(BF16),
        "attn_sinks": attn_sinks,
        "rwkv_in": (row(padc(rwkv_mu, B_COLS_PAD)), row(rwkv_w0), wdec, row(rwkv_a0), waaa, wgate),
        "rwkv_rows": tuple(row(t) for t in (rwkv_k_k, rwkv_k_a, rwkv_r_k, rwkv_gn_w, rwkv_gn_b)),
        "rwkv_heads": tuple(t.reshape(-1, B_HEADS, 1, HEAD_DIM)
                            for t in (rwkv_k_k, rwkv_k_a, rwkv_r_k, rwkv_gn_w, rwkv_gn_b)),
        "conv_w_in": conv_w_in,
        "conv_w": conv_w,
        "conv_w_out": conv_w_out.astype(BF16),
    }


def kernel(x_prompt, x_sample, cache_swa_k, cache_swa_v, state_rwkv, state_rwkv_shift, state_conv, norm_g, ffn_w_gu, ffn_w_down, ab_w_in, ab_w_out, attn_sinks, rwkv_mu, rwkv_w0, rwkv_w_decay, rwkv_a0, rwkv_w_aaa, rwkv_w_gate, rwkv_k_k, rwkv_k_a, rwkv_r_k, rwkv_gn_w, rwkv_gn_b, conv_w_in, conv_w, conv_w_out):
    w = _prep_weights(norm_g, ffn_w_gu, ffn_w_down, ab_w_in, ab_w_out, attn_sinks, rwkv_mu, rwkv_w0, rwkv_w_decay,
                      rwkv_a0, rwkv_w_aaa, rwkv_w_gate, rwkv_k_k, rwkv_k_a, rwkv_r_k, rwkv_gn_w, rwkv_gn_b,
                      conv_w_in, conv_w, conv_w_out)
    n_ab, dec_batch = cache_swa_k.shape[:2]
    past = (cache_swa_k.reshape(n_ab, dec_batch, WINDOW, A_KV), cache_swa_v.reshape(n_ab, dec_batch, WINDOW, A_KV),
            state_rwkv, state_rwkv_shift, state_conv)
    y_prompt, y_sample, o = _trunk(x_prompt, x_sample, past, w)
    return (y_prompt, y_sample, o["pk"], o["pv"], o["ps"], o["psh"], o["pc"],
            o["sk"], o["sv"], o["ss"], o["ssh"], o["sc"])
```

```python
import functools

import jax
import jax.numpy as jnp
from jax import lax
from jax.experimental import pallas as pl
from jax.experimental.pallas import tpu as pltpu

F32 = jnp.float32
BF16 = jnp.bfloat16

HEAD_DIM = 64
A_HEADS = 16
A_KV_HEADS = 4
A_GROUP = A_HEADS // A_KV_HEADS
WINDOW = 128
ROPE_THETA = 10000.0
PAST_LEN = 16384
A_Q = A_HEADS * HEAD_DIM
A_KV = A_KV_HEADS * HEAD_DIM
A_COLS = A_Q + 2 * A_KV
B_HEADS = 16
B_WIDTH = B_HEADS * HEAD_DIM
D_DECAY = 64
D_AAA = 64
D_GATE = 160
B_COLS = 3 * B_WIDTH + D_DECAY + D_AAA + D_GATE
GN_EPS = 64e-5
AB_COLS = A_COLS + B_COLS
CONV_W = 3
NORM_EPS = 1e-6

LANE = 128
HEAD_PAIRS = B_WIDTH // LANE
AB_COLS_PAD = 5120
B_COLS_PAD = AB_COLS_PAD - A_COLS
LORA_IN = D_DECAY + D_AAA
GATE_PAD = AB_COLS_PAD - A_COLS - 3 * B_WIDTH - LORA_IN
NORM_ROWS = 16
NORM_UNROLL = 64
SWA_STEP_BLOCKS = 2
RWKV_CHUNK = 64
RWKV_STEP_CHUNKS = 4
RWKV_INV_BLOCK = 16
VMEM_LIMIT = 56 * 1024 * 1024
WIDE_VMEM_LIMIT = 60 * 1024 * 1024


def _tiles(m):
    return dict(ffn_m=min(1024, m), ffn_f=256 if m >= 1024 else 512, proj_m=min(1024, m), mix_m=min(512, m))


def _params(sem, vmem_limit=VMEM_LIMIT):
    return pltpu.CompilerParams(dimension_semantics=sem, vmem_limit_bytes=vmem_limit)


def _layer(arr, idx):
    idx = tuple(idx)
    rest = arr.shape[len(idx):]
    return pl.BlockSpec((None,) * len(idx) + rest, lambda *_: idx + (0,) * len(rest))


def _rms(x, g):
    return x * lax.rsqrt(jnp.mean(x * x, axis=-1, keepdims=True) + NORM_EPS) * g


def _dot(a, b):
    return jnp.dot(a, b, preferred_element_type=F32)


def _for_row_chunks(n_rows, body):
    size = min(NORM_ROWS, n_rows)
    assert n_rows % size == 0

    def step(c, carry):
        body(pl.ds(pl.multiple_of(c * size, size), size))
        return carry

    n = n_rows // size
    lax.fori_loop(0, n, step, 0, unroll=min(NORM_UNROLL, n))


def _norm_rows_into(h_ref, x_ref, g):
    def chunk(rows):
        h_ref[rows, :] = _rms(x_ref[rows, :], g).astype(BF16)

    _for_row_chunks(x_ref.shape[0], chunk)


def _add_normed_rows(o_ref, x_ref, g, scale=1.0):
    def chunk(rows):
        normed = _rms(o_ref[rows, :], g)
        o_ref[rows, :] = x_ref[rows, :] + (normed if scale == 1.0 else scale * normed)

    _for_row_chunks(x_ref.shape[0], chunk)


def _ffn_kernel(x_ref, xs_ref, g0_ref, g1_ref, wg_ref, wu_ref, wd_ref, o_ref, os_ref, h_ref, hs_ref):
    i = pl.program_id(0)
    f = pl.program_id(1)
    g0 = g0_ref[...]
    g1 = g1_ref[...]
    cast = lambda w_ref: w_ref[...].astype(BF16)

    def group(x_ref, o_ref, h_ref):
        def hidden_tile(accumulate):
            wg, wu, wd = cast(wg_ref), cast(wu_ref), cast(wd_ref)
            n_rows = h_ref.shape[0]
            piece = n_rows // 2 if n_rows >= 512 else n_rows
            for start in range(0, n_rows, piece):
                rows = pl.ds(start, piece)
                h = h_ref[rows, :]
                gate = _dot(h, wg)
                act = (gate * jax.nn.sigmoid(gate) * _dot(h, wu)).astype(BF16)
                if accumulate:
                    o_ref[rows, :] += _dot(act, wd)
                else:
                    o_ref[rows, :] = _dot(act, wd)

        @pl.when(f == 0)
        def _():
            _norm_rows_into(h_ref, x_ref, g0)
            hidden_tile(False)

        @pl.when(f > 0)
        def _():
            hidden_tile(True)

        @pl.when(f == pl.num_programs(1) - 1)
        def _():
            _add_normed_rows(o_ref, x_ref, g1, 0.5)

    group(x_ref, o_ref, h_ref)

    @pl.when(i == 0)
    def _():
        group(xs_ref, os_ref, hs_ref)


def _ffn(x, xs, norm_g, g_rows, w_gu, w_down, lj, tm, tf):
    m, d = x.shape
    ms = xs.shape[0]
    l, j = lj
    nf = w_down.shape[2] // tf
    return pl.pallas_call(
        _ffn_kernel,
        grid=(m // tm, nf),
        in_specs=[
            pl.BlockSpec((tm, d), lambda i, f: (i, 0)),
            pl.BlockSpec((ms, d), lambda i, f: (0, 0)),
            _layer(norm_g, (g_rows[0],)),
            _layer(norm_g, (g_rows[1],)),
            pl.BlockSpec((None, None, d, tf), lambda i, f: (l, j, 0, f)),
            pl.BlockSpec((None, None, d, tf), lambda i, f: (l, j, 0, nf + f)),
            pl.BlockSpec((None, None, tf, d), lambda i, f: (l, j, f, 0)),
        ],
        out_specs=[pl.BlockSpec((tm, d), lambda i, f: (i, 0)),
                   pl.BlockSpec((ms, d), lambda i, f: (0, 0))],
        out_shape=[jax.ShapeDtypeStruct((m, d), F32), jax.ShapeDtypeStruct((ms, d), F32)],
        scratch_shapes=[pltpu.VMEM((tm, d), BF16), pltpu.VMEM((ms, d), BF16)],
        compiler_params=_params(("arbitrary", "arbitrary"), WIDE_VMEM_LIMIT),
        name="ffn",
    )(x, xs, norm_g, norm_g, w_gu, w_gu, w_down)


def _norm_mm_kernel(x_ref, g_ref, w_ref, o_ref, h_ref):
    @pl.when(pl.program_id(1) == 0)
    def _():
        _norm_rows_into(h_ref, x_ref, g_ref[...])

    o_ref[...] = _dot(h_ref[...], w_ref[...])


def _norm_mm(x, norm_g, g_row, w, li, tm, tn):
    m, d = x.shape
    n = w.shape[-1]
    return pl.pallas_call(
        _norm_mm_kernel,
        grid=(m // tm, n // tn),
        in_specs=[
            pl.BlockSpec((tm, d), lambda i, j: (i, 0)),
            _layer(norm_g, (g_row,)),
            pl.BlockSpec((None, d, tn), lambda i, j: (li, 0, j)),
        ],
        out_specs=pl.BlockSpec((tm, tn), lambda i, j: (i, j)),
        out_shape=jax.ShapeDtypeStruct((m, n), F32),
        scratch_shapes=[pltpu.VMEM((tm, d), BF16)],
        compiler_params=_params(("parallel", "arbitrary")),
        name="ab_in",
    )(x, norm_g, w)


def _conv_in_kernel(x_ref, g_ref, wb_ref, wc_ref, wh_ref, bg_ref, u_ref, h_ref):
    @pl.when(pl.program_id(1) == 0)
    def _():
        _norm_rows_into(h_ref, x_ref, g_ref[...])

    h = h_ref[...]
    bg_ref[...] = _dot(h, wb_ref[...].astype(BF16))
    u_ref[...] = _dot(h, wc_ref[...].astype(BF16)) * _dot(h, wh_ref[...].astype(BF16))


def _conv_in(x, norm_g, g_row, w_in, li, tm, tn):
    m, d = x.shape
    dc = w_in.shape[-1] // 3
    nj = dc // tn
    return pl.pallas_call(
        _conv_in_kernel,
        grid=(m // tm, nj),
        in_specs=[
            pl.BlockSpec((tm, d), lambda i, j: (i, 0)),
            _layer(norm_g, (g_row,)),
            pl.BlockSpec((None, d, tn), lambda i, j: (li, 0, j)),
            pl.BlockSpec((None, d, tn), lambda i, j: (li, 0, nj + j)),
            pl.BlockSpec((None, d, tn), lambda i, j: (li, 0, 2 * nj + j)),
        ],
        out_specs=[pl.BlockSpec((tm, tn), lambda i, j: (i, j)),
                   pl.BlockSpec((tm, tn), lambda i, j: (i, j))],
        out_shape=[jax.ShapeDtypeStruct((m, dc), F32), jax.ShapeDtypeStruct((m, dc), F32)],
        scratch_shapes=[pltpu.VMEM((tm, d), BF16)],
        compiler_params=_params(("parallel", "arbitrary"), WIDE_VMEM_LIMIT),
        name="conv_in",
    )(x, norm_g, w_in, w_in, w_in)


def _conv_taps(u, p1, p2, cw):
    return p2 * cw[0:1, :] + p1 * cw[1:2, :] + u * cw[2:3, :]


def _conv_out_seq_kernel(x_ref, bg_ref, u_ref, halo_ref, cw_ref, w_ref, g_ref, o_ref, *, tiles_per_seq):
    u = u_ref[...]
    first = (pl.program_id(0) % tiles_per_seq) == 0
    h1 = jnp.where(first, 0.0, halo_ref[7:8, :])
    h2 = jnp.where(first, 0.0, halo_ref[6:7, :])
    row = lax.broadcasted_iota(jnp.int32, u.shape, 0)
    p1 = jnp.where(row == 0, h1, pltpu.roll(u, 1, 0))
    p2 = jnp.where(row == 0, h2, jnp.where(row == 1, h1, pltpu.roll(u, 2, 0)))
    z = (bg_ref[...] * _conv_taps(u, p1, p2, cw_ref[...])).astype(BF16)
    o_ref[...] = _dot(z, w_ref[...])
    _add_normed_rows(o_ref, x_ref, g_ref[...])


def _conv_out_step_kernel(x_ref, bg_ref, u_ref, p1_ref, p2_ref, cw_ref, w_ref, g_ref, o_ref):
    z = (bg_ref[...] * _conv_taps(u_ref[...], p1_ref[...], p2_ref[...], cw_ref[...])).astype(BF16)
    o_ref[...] = _dot(z, w_ref[...])
    _add_normed_rows(o_ref, x_ref, g_ref[...])


def _conv_out_seq(x, bg, u, conv_w, w_out, norm_g, g_row, li, tm, seq):
    m, d = x.shape
    dc = u.shape[1]
    hb = tm // 8
    row_spec = pl.BlockSpec((tm, dc), lambda i: (i, 0))
    return pl.pallas_call(
        functools.partial(_conv_out_seq_kernel, tiles_per_seq=seq // tm),
        grid=(m // tm,),
        in_specs=[
            pl.BlockSpec((tm, d), lambda i: (i, 0)),
            row_spec, row_spec,
            pl.BlockSpec((8, dc), lambda i: (jnp.maximum(i * hb - 1, 0), 0)),
            _layer(conv_w, (li,)), _layer(w_out, (li,)), _layer(norm_g, (g_row,)),
        ],
        out_specs=pl.BlockSpec((tm, d), lambda i: (i, 0)),
        out_shape=jax.ShapeDtypeStruct((m, d), F32),
        compiler_params=_params(("parallel",)),
        name="conv_out",
    )(x, bg, u, u, conv_w, w_out, norm_g)


def _conv_out_step(x, bg, u, p1, p2, conv_w, w_out, norm_g, g_row, li):
    m, d = x.shape
    dc = u.shape[1]
    row_spec = pl.BlockSpec((m, dc), lambda i: (0, 0))
    return pl.pallas_call(
        _conv_out_step_kernel,
        grid=(1,),
        in_specs=[
            pl.BlockSpec((m, d), lambda i: (0, 0)),
            row_spec, row_spec, row_spec, row_spec,
            _layer(conv_w, (li,)), _layer(w_out, (li,)), _layer(norm_g, (g_row,)),
        ],
        out_specs=pl.BlockSpec((m, d), lambda i: (0, 0)),
        out_shape=jax.ShapeDtypeStruct((m, d), F32),
        compiler_params=_params(("arbitrary",)),
        name="conv_out_step",
    )(x, bg, u, p1, p2, conv_w, w_out, norm_g)


def _rope(x, cos, sin):
    width = x.shape[-1]
    half = HEAD_DIM // 2
    lane = lax.broadcasted_iota(jnp.int32, x.shape, x.ndim - 1)
    swapped = jnp.where((lane % HEAD_DIM) < half,
                        pltpu.roll(x, width - half, x.ndim - 1),
                        pltpu.roll(x, half, x.ndim - 1))
    return x * cos + swapped * sin


def _swa_seq_kernel(sink_ref, q_ref, kc_ref, kp_ref, vc_ref, vp_ref, cq_ref, sq_ref, cp_ref, sp_ref,
                    o_ref, nk_ref, nv_ref, *, li):
    n = pl.program_id(1)
    nblk = q_ref.shape[0] // WINDOW
    q = _rope(q_ref[...], cq_ref[...], sq_ref[...])
    kc = _rope(kc_ref[...], cq_ref[:, :A_KV], sq_ref[:, :A_KV])
    kp = _rope(kp_ref[...], cp_ref[...], sp_ref[...])
    kall = jnp.concatenate([kp, kc], axis=0).astype(BF16)
    vall = jnp.concatenate([vp_ref[...], vc_ref[...]], axis=0).astype(BF16)
    rows = A_GROUP * WINDOW
    head = lambda x, h: x[:, h * HEAD_DIM:(h + 1) * HEAD_DIM]
    batch = [(j, g) for j in range(nblk) for g in range(A_KV_HEADS)]
    blk = lambda x, j: x[j * WINDOW:(j + 1) * WINDOW]
    window = lambda x, j: x[j * WINDOW:(j + 2) * WINDOW]
    qg = jnp.stack([jnp.concatenate([head(blk(q, j), g * A_GROUP + i) for i in range(A_GROUP)], axis=0)
                    for j, g in batch], axis=0)
    sink = jnp.stack([jnp.concatenate([jnp.full((WINDOW, 1), sink_ref[li, g * A_GROUP + i], F32)
                                       for i in range(A_GROUP)], axis=0) for j, g in batch], axis=0)
    qi = lax.broadcasted_iota(jnp.int32, (1, rows, 2 * WINDOW), 1) % WINDOW
    kj = lax.broadcasted_iota(jnp.int32, (1, rows, 2 * WINDOW), 2)
    diff = WINDOW + qi - kj
    first = lax.broadcasted_iota(jnp.int32, (len(batch), 1, 1), 0) < A_KV_HEADS
    lo = jnp.where(first & (n == 0), WINDOW, 0)
    valid = (diff >= 0) & (diff <= WINDOW) & (kj >= lo)
    s = _bmm_nt(qg, jnp.stack([head(window(kall, j), g) for j, g in batch], axis=0))
    s = jnp.where(valid, s * (HEAD_DIM ** -0.5), -jnp.inf)
    mx = jnp.maximum(jnp.max(s, axis=-1, keepdims=True), sink)
    e = jnp.exp(s - mx)
    p = e / (jnp.sum(e, axis=-1, keepdims=True) + jnp.exp(sink - mx))
    og = _bmm(p, jnp.stack([head(window(vall, j), g) for j, g in batch], axis=0))
    o_ref[...] = jnp.concatenate(
        [jnp.concatenate([og[j * A_KV_HEADS + h // A_GROUP][(h % A_GROUP) * WINDOW:(h % A_GROUP + 1) * WINDOW]
                          for h in range(A_HEADS)], axis=-1) for j in range(nblk)], axis=0)

    @pl.when(n == pl.num_programs(1) - 1)
    def _():
        nk_ref[...] = blk(kc, nblk - 1)
        nv_ref[...] = vc_ref[(nblk - 1) * WINDOW:nblk * WINDOW, :]


def _swa_seq(proj, sinks, li, cos, sin, batch, seq):
    rows = SWA_STEP_BLOCKS * WINDOW
    ns = seq // rows
    kcol = A_Q // A_KV
    vcol = kcol + 1
    cur = lambda b, n: b * ns + n
    prev = lambda b, n: (b * ns + n) * SWA_STEP_BLOCKS - jnp.minimum(n, 1)
    tprev = lambda b, n: jnp.maximum(n * SWA_STEP_BLOCKS - 1, 0)
    return pl.pallas_call(
        functools.partial(_swa_seq_kernel, li=li),
        grid=(batch, ns),
        in_specs=[
            pl.BlockSpec(memory_space=pltpu.SMEM),
            pl.BlockSpec((rows, A_Q), lambda b, n: (cur(b, n), 0)),
            pl.BlockSpec((rows, A_KV), lambda b, n: (cur(b, n), kcol)),
            pl.BlockSpec((WINDOW, A_KV), lambda b, n: (prev(b, n), kcol)),
            pl.BlockSpec((rows, A_KV), lambda b, n: (cur(b, n), vcol)),
            pl.BlockSpec((WINDOW, A_KV), lambda b, n: (prev(b, n), vcol)),
            pl.BlockSpec((rows, A_Q), lambda b, n: (n, 0)),
            pl.BlockSpec((rows, A_Q), lambda b, n: (n, 0)),
            pl.BlockSpec((WINDOW, A_KV), lambda b, n: (tprev(b, n), 0)),
            pl.BlockSpec((WINDOW, A_KV), lambda b, n: (tprev(b, n), 0)),
        ],
        out_specs=[
            pl.BlockSpec((rows, A_Q), lambda b, n: (cur(b, n), 0)),
            pl.BlockSpec((WINDOW, A_KV), lambda b, n: (b, 0)),
            pl.BlockSpec((WINDOW, A_KV), lambda b, n: (b, 0)),
        ],
        out_shape=[
            jax.ShapeDtypeStruct((batch * seq, A_Q), F32),
            jax.ShapeDtypeStruct((batch * WINDOW, A_KV), F32),
            jax.ShapeDtypeStruct((batch * WINDOW, A_KV), F32),
        ],
        compiler_params=_params(("parallel", "arbitrary")),
        name="swa_seq",
    )(sinks, proj, proj, proj, proj, proj, cos, sin, cos, sin)


def _swa_step_kernel(sink_ref, p_ref, ck_ref, cv_ref, cos_ref, sin_ref, o_ref, nk_ref, nv_ref, *, li):
    sb = p_ref.shape[0]
    pairs = A_KV // LANE
    per_pair = A_HEADS // pairs
    q = _rope(p_ref[:, 0:A_Q], cos_ref[...], sin_ref[...]) * (HEAD_DIM ** -0.5)
    kn = _rope(p_ref[:, A_Q:A_Q + A_KV], cos_ref[:, :A_KV], sin_ref[:, :A_KV])
    vn = p_ref[:, A_Q + A_KV:A_COLS]
    zero = jnp.zeros((1, HEAD_DIM), F32)

    def q_row(s, h):
        piece = q[s:s + 1, h * HEAD_DIM:(h + 1) * HEAD_DIM]
        return jnp.concatenate([piece, zero] if (h // A_GROUP) % 2 == 0 else [zero, piece], axis=-1)

    batch = [(s, p) for s in range(sb) for p in range(pairs)]
    qb = jnp.stack([jnp.concatenate([q_row(s, p * per_pair + r) for r in range(per_pair)], axis=0)
                    for s, p in batch], axis=0)
    lanes = lambda x, s, p: x[s:s + 1, p * LANE:(p + 1) * LANE]
    knb = jnp.stack([lanes(kn, s, p) for s, p in batch], axis=0)
    vnb = jnp.stack([lanes(vn, s, p) for s, p in batch], axis=0)
    kb = jnp.stack([ck_ref[s, :, p * LANE:(p + 1) * LANE] for s, p in batch], axis=0)
    vb = jnp.stack([cv_ref[s, :, p * LANE:(p + 1) * LANE] for s, p in batch], axis=0)
    sink = jnp.stack([jnp.concatenate([jnp.full((1, 1), sink_ref[li, p * per_pair + r], F32)
                                       for r in range(per_pair)], axis=0) for s, p in batch], axis=0)
    s_old = _bmm_nt(qb, kb)
    s_new = jnp.sum(qb * knb, axis=-1, keepdims=True)
    mx = jnp.maximum(jnp.maximum(jnp.max(s_old, axis=-1, keepdims=True), s_new), sink)
    e = jnp.exp(s_old - mx)
    e_new = jnp.exp(s_new - mx)
    den = jnp.sum(e, axis=-1, keepdims=True) + e_new + jnp.exp(sink - mx)
    ob = (_bmm(e, vb) + e_new * vnb) / den

    def o_piece(s, h):
        row = ob[s * pairs + h // per_pair][h % per_pair:h % per_pair + 1]
        half = (h // A_GROUP) % 2
        return row[:, half * HEAD_DIM:(half + 1) * HEAD_DIM]

    o_ref[...] = jnp.concatenate(
        [jnp.concatenate([o_piece(s, h) for h in range(A_HEADS)], axis=-1) for s in range(sb)], axis=0)
    nk_ref[:, 0:WINDOW - 1, :] = ck_ref[:, 1:WINDOW, :]
    nv_ref[:, 0:WINDOW - 1, :] = cv_ref[:, 1:WINDOW, :]
    for s in range(sb):
        nk_ref[s, WINDOW - 1:WINDOW, :] = kn[s:s + 1, :]
        nv_ref[s, WINDOW - 1:WINDOW, :] = vn[s:s + 1, :]


def _swa_step(proj, sinks, li, cache_k, cache_v, cos, sin, sb):
    batch, width = proj.shape
    cache_in = pl.BlockSpec((None, sb, WINDOW, A_KV), lambda b: (li, b, 0, 0))
    cache_out = pl.BlockSpec((sb, WINDOW, A_KV), lambda b: (b, 0, 0))
    return pl.pallas_call(
        functools.partial(_swa_step_kernel, li=li),
        grid=(batch // sb,),
        in_specs=[
            pl.BlockSpec(memory_space=pltpu.SMEM),
            pl.BlockSpec((sb, width), lambda b: (b, 0)),
            cache_in, cache_in,
            pl.BlockSpec((1, A_Q), lambda b: (0, 0)),
            pl.BlockSpec((1, A_Q), lambda b: (0, 0)),
        ],
        out_specs=[pl.BlockSpec((sb, A_Q), lambda b: (b, 0)), cache_out, cache_out],
        out_shape=[
            jax.ShapeDtypeStruct((batch, A_Q), F32),
            jax.ShapeDtypeStruct((batch, WINDOW, A_KV), F32),
            jax.ShapeDtypeStruct((batch, WINDOW, A_KV), F32),
        ],
        compiler_params=_params(("parallel",)),
        name="swa_step",
    )(sinks, proj, cache_k, cache_v, cos, sin)


def _softplus(z):
    return jnp.maximum(z, 0.0) + jnp.log(1.0 + jnp.exp(-jnp.abs(z)))


def _rwkv_in_math(pb, prev, mu, w0, wdec, a0, waaa, wgate):
    xm = pb + (prev - pb) * mu
    r = xm[:, 0:B_WIDTH]
    k = xm[:, B_WIDTH:2 * B_WIDTH]
    v = xm[:, 2 * B_WIDTH:3 * B_WIDTH]
    lora = xm[:, 3 * B_WIDTH:3 * B_WIDTH + LORA_IN]
    dg = xm[:, 3 * B_WIDTH + LORA_IN:]
    w_log = -_softplus(-(w0 + _dot(jnp.tanh(lora).astype(BF16), wdec))) - 0.5
    lw = -jnp.exp(w_log)
    a = jax.nn.sigmoid(a0 + _dot(lora.astype(BF16), waaa))
    g = _dot(jax.nn.sigmoid(dg).astype(BF16), wgate)
    return r, k, v, lw, a, g


def _rwkv_in_step_kernel(p_ref, prev_ref, mu_ref, w0_ref, wdec_ref, a0_ref, waaa_ref, wgate_ref, *out_refs):
    vals = _rwkv_in_math(p_ref[:, A_COLS:], prev_ref[...], mu_ref[...], w0_ref[...], wdec_ref[...], a0_ref[...],
                         waaa_ref[...], wgate_ref[...])
    for ref, val in zip(out_refs, vals):
        ref[...] = val


def _rwkv_in_step(proj, prev, wts, li):
    m, width = proj.shape
    out_spec = pl.BlockSpec((m, B_WIDTH), lambda i: (0, 0))
    return pl.pallas_call(
        _rwkv_in_step_kernel,
        grid=(1,),
        in_specs=[pl.BlockSpec((m, width), lambda i: (0, 0)), pl.BlockSpec((m, B_COLS_PAD), lambda i: (0, 0))]
        + [_layer(a, (li,)) for a in wts],
        out_specs=[out_spec] * 6,
        out_shape=[jax.ShapeDtypeStruct((m, B_WIDTH), F32)] * 6,
        compiler_params=_params(("arbitrary",)),
        name="rwkv_in_step",
    )(proj, prev, *wts)


def _bdg(a, b, dn):
    return lax.dot_general(a.astype(BF16), b.astype(BF16), dn, preferred_element_type=F32)


def _bmm(a, b):
    return _bdg(a, b, (((2,), (1,)), ((0,), (0,))))


def _bmm_nt(a, b):
    return _bdg(a, b, (((2,), (2,)), ((0,), (0,))))


def _bmm_tn(a, b):
    return _bdg(a, b, (((1,), (1,)), ((0,), (0,))))


def _unit_lower_inverse(a, eye, same_block):
    d = jnp.where(same_block, a, 0.0)
    e = a - d
    x = eye + d
    dp = d
    p = 1
    while 2 * p < RWKV_INV_BLOCK:
        dp = _bmm(dp, dp)
        x = _bmm(x, eye + dp)
        p *= 2
    nn = _bmm(x, e)
    return _bmm(_bmm(eye + nn, eye + _bmm(nn, nn)), x)


def _head_sums(x, first_half):
    outs = []
    for p in range(HEAD_PAIRS):
        xp = x[:, p * LANE:(p + 1) * LANE]
        s0 = jnp.sum(jnp.where(first_half, xp, 0.0), axis=-1, keepdims=True)
        s1 = jnp.sum(jnp.where(first_half, 0.0, xp), axis=-1, keepdims=True)
        outs.append(jnp.where(first_half, s0, s1))
    return jnp.concatenate(outs, axis=-1)


def _rwkv_scan_kernel(p_ref, halo_ref, mu_ref, w0_ref, wdec_ref, a0_ref, waaa_ref, wgate_ref,
                      kk_ref, ka_ref, rk_ref, gw_ref, gb_ref, y_ref, s_out_ref, st_ref):
    c = pl.program_id(1)

    @pl.when(c == 0)
    def _():
        st_ref[...] = jnp.zeros_like(st_ref)

    pb = p_ref[:, A_COLS:]
    hrow = jnp.where(c == 0, 0.0, halo_ref[7:8, A_COLS:])
    prev = jnp.where(lax.broadcasted_iota(jnp.int32, pb.shape, 0) == 0, hrow, pltpu.roll(pb, 1, 0))
    r, k, v, lw, a, gate = _rwkv_in_math(pb, prev, mu_ref[...], w0_ref[...], wdec_ref[...], a0_ref[...],
                                         waaa_ref[...], wgate_ref[...])
    rows = r.shape[0]
    cs = RWKV_CHUNK
    chunks = rows // cs
    c2 = 2 * cs
    iota = lambda shape, dim: lax.broadcasted_iota(jnp.int32, shape, dim)

    ti = iota((rows, rows), 0)
    si = iota((rows, rows), 1)
    tril = jnp.where((si <= ti) & (si // cs == ti // cs), 1.0, 0.0).astype(BF16)
    lw_hi = lw.astype(BF16)
    rem = lw - lw_hi.astype(F32)
    lw_mid = rem.astype(BF16)
    lw_lo = (rem - lw_mid.astype(F32)).astype(BF16)
    lcum = _dot(tril, lw_hi) + (_dot(tril, lw_mid) + _dot(tril, lw_lo))
    p_t = jnp.exp(lcum)
    p_inv = jnp.exp(-lcum)
    p_prev = jnp.exp(lcum - lw)
    p_ends = [jnp.exp(lcum[(j + 1) * cs - 1:(j + 1) * cs, :]) for j in range(chunks)]
    p_end = jnp.concatenate([jnp.broadcast_to(pe, (cs, B_WIDTH)) for pe in p_ends], axis=0)

    first_half = iota((rows, LANE), 1) < HEAD_DIM
    kk = k * kk_ref[...]
    kk = kk / jnp.maximum(jnp.sqrt(_head_sums(kk * kk, first_half)), 1e-12)
    k2 = k * (1.0 + (a - 1.0) * ka_ref[...])
    al = -kk * p_prev
    be = kk * a * p_inv
    kt = k2 * p_inv
    rt = r * p_t
    be_end = be * p_end
    kt_end = kt * p_end
    bonus = _head_sums(r * k2 * rk_ref[...], first_half) * v

    np_, one = HEAD_PAIRS, (1,)
    own_half = (iota(one + (c2, LANE), 1) // cs) == (iota(one + (c2, LANE), 2) // HEAD_DIM)
    t2 = iota(one + (c2, c2), 1)
    s2 = iota(one + (c2, c2), 2)
    same_head = (t2 // cs) == (s2 // cs)
    strict = same_head & (s2 < t2)
    lower = same_head & (s2 <= t2)
    same_block = (t2 // RWKV_INV_BLOCK) == (s2 // RWKV_INV_BLOCK)
    eye = jnp.where(t2 == s2, 1.0, 0.0)
    kl = iota(one + (LANE, LANE), 1)
    vl = iota(one + (LANE, LANE), 2)
    st_diag = kl == vl
    st_blocks = (kl // HEAD_DIM) == (vl // HEAD_DIM)

    pairs = lambda x: jnp.stack([x[j * cs:(j + 1) * cs, p * LANE:(p + 1) * LANE]
                                 for j in range(chunks) for p in range(np_)], axis=0)
    dup = lambda x: jnp.concatenate([x, x], axis=1)
    own = lambda x: jnp.where(own_half, dup(pairs(x)), 0.0).astype(BF16)
    al2 = own(al)
    rt2 = own(rt)
    v2 = own(v)
    prod = _bmm_nt(jnp.concatenate([al2, rt2], axis=1),
                   jnp.concatenate([dup(pairs(be)), dup(pairs(kt))], axis=1))
    a_ab = jnp.where(strict, prod[:, :c2, :c2], 0.0)
    a_ak = jnp.where(strict, prod[:, :c2, c2:], 0.0)
    m_rb = jnp.where(lower, prod[:, c2:, :c2], 0.0)
    m_rk = jnp.where(lower, prod[:, c2:, c2:], 0.0)
    tinv = _unit_lower_inverse(a_ab, eye, same_block)
    be_end_p, kt_end_p, v_p = pairs(be_end), pairs(kt_end), pairs(v)

    st = st_ref[...]
    ys = []
    for j in range(chunks):
        of = lambda x: x[j * np_:(j + 1) * np_]
        u2 = _bmm(of(tinv), _bmm(of(al2), st) + _bmm(of(a_ak), of(v2)))
        y2 = _bmm(of(rt2), st) + _bmm(of(m_rb), u2) + _bmm(of(m_rk), of(v2))
        y3 = y2[:, :cs] + y2[:, cs:]
        u = u2[:, :cs] + u2[:, cs:]
        pe = jnp.stack([p_ends[j][:, p * LANE:(p + 1) * LANE] for p in range(np_)], axis=0)
        p_col = jnp.sum(jnp.where(st_diag, pe, 0.0), axis=2, keepdims=True)
        inc = _bmm_tn(of(be_end_p), u) + _bmm_tn(of(kt_end_p), of(v_p))
        st = st * p_col + jnp.where(st_blocks, inc, 0.0)
        ys.append(jnp.concatenate([y3[p] for p in range(np_)], axis=-1))
    st_ref[...] = st

    y = jnp.concatenate(ys, axis=0)
    mean = _head_sums(y, first_half) * (1.0 / HEAD_DIM)
    yc = y - mean
    var = _head_sums(yc * yc, first_half) * (1.0 / HEAD_DIM)
    y_ref[...] = (yc * lax.rsqrt(var + GN_EPS) * gw_ref[...] + gb_ref[...] + bonus) * gate

    @pl.when(c == pl.num_programs(1) - 1)
    def _():
        for p in range(np_):
            sp = st_ref[p].T
            s_out_ref[0, 2 * p] = sp[:HEAD_DIM, :HEAD_DIM]
            s_out_ref[0, 2 * p + 1] = sp[HEAD_DIM:, HEAD_DIM:]


def _rwkv_scan(proj, in_wts, head_wts, li, batch, cs):
    m, width = proj.shape
    nc = m // batch // cs
    st_spec = pl.BlockSpec((1, B_HEADS, HEAD_DIM, HEAD_DIM), lambda b, c: (b, 0, 0, 0))
    return pl.pallas_call(
        _rwkv_scan_kernel,
        grid=(batch, nc),
        in_specs=[pl.BlockSpec((cs, width), lambda b, c: (b * nc + c, 0)),
                  pl.BlockSpec((8, width), lambda b, c: (jnp.maximum((b * nc + c) * (cs // 8) - 1, 0), 0))]
        + [_layer(t, (li,)) for t in in_wts + head_wts],
        out_specs=[pl.BlockSpec((cs, B_WIDTH), lambda b, c: (b * nc + c, 0)), st_spec],
        out_shape=[jax.ShapeDtypeStruct((m, B_WIDTH), F32),
                   jax.ShapeDtypeStruct((batch, B_HEADS, HEAD_DIM, HEAD_DIM), F32)],
        scratch_shapes=[pltpu.VMEM((HEAD_PAIRS, LANE, LANE), F32)],
        compiler_params=_params(("parallel", "arbitrary")),
        name="rwkv_scan",
    )(proj, proj, *in_wts, *head_wts)


def _rwkv_step_kernel(r_ref, k_ref, v_ref, lw_ref, a_ref, g_ref, s_ref, kk_ref, ka_ref, rk_ref, gw_ref, gb_ref,
                      y_ref, s_out_ref):
    bb = r_ref.shape[0]
    n = bb * B_HEADS

    def heads(ref):
        x = ref[...]
        return jnp.stack([x[b:b + 1, h * HEAD_DIM:(h + 1) * HEAD_DIM]
                          for b in range(bb) for h in range(B_HEADS)], axis=0)

    per_seq = lambda ref: jnp.concatenate([ref[...]] * bb, axis=0)
    r, k, v, lw, a = heads(r_ref), heads(k_ref), heads(v_ref), heads(lw_ref), heads(a_ref)
    s = s_ref[...].reshape(n, HEAD_DIM, HEAD_DIM)
    kk = k * per_seq(kk_ref)
    kk = kk / jnp.maximum(jnp.sqrt(jnp.sum(kk * kk, axis=-1, keepdims=True)), 1e-12)
    k2 = k * (1.0 + (a - 1.0) * per_seq(ka_ref))
    eye = (lax.broadcasted_iota(jnp.int32, (1, HEAD_DIM, HEAD_DIM), 1)
           == lax.broadcasted_iota(jnp.int32, (1, HEAD_DIM, HEAD_DIM), 2))
    sa = jnp.sum(s * (-kk), axis=-1, keepdims=True)
    v_col = jnp.sum(jnp.where(eye, v, 0.0), axis=-1, keepdims=True)
    s_new = s * jnp.exp(lw) + sa * (kk * a) + v_col * k2
    y_col = jnp.sum(s_new * r, axis=-1, keepdims=True)
    y = jnp.sum(jnp.where(eye, y_col, 0.0), axis=1, keepdims=True)
    mean = jnp.mean(y, axis=-1, keepdims=True)
    yc = y - mean
    var = jnp.mean(yc * yc, axis=-1, keepdims=True)
    out = (yc * lax.rsqrt(var + GN_EPS) * per_seq(gw_ref) + per_seq(gb_ref)
           + jnp.sum(r * k2 * per_seq(rk_ref), axis=-1, keepdims=True) * v)
    s_out_ref[...] = s_new.reshape(bb, B_HEADS, HEAD_DIM, HEAD_DIM)
    y_ref[...] = g_ref[...] * jnp.concatenate(
        [jnp.concatenate([out[b * B_HEADS + h] for h in range(B_HEADS)], axis=-1) for b in range(bb)], axis=0)


def _rwkv_step(r, k, v, lw, a, gate, state, head_wts, li, bb):
    batch = r.shape[0]
    row_spec = pl.BlockSpec((bb, B_WIDTH), lambda b: (b, 0))
    st_shape = (bb, B_HEADS, HEAD_DIM, HEAD_DIM)
    return pl.pallas_call(
        _rwkv_step_kernel,
        grid=(batch // bb,),
        in_specs=[row_spec] * 6 + [pl.BlockSpec((None,) + st_shape, lambda b: (li, b, 0, 0, 0))]
        + [_layer(t, (li,)) for t in head_wts],
        out_specs=[row_spec, pl.BlockSpec(st_shape, lambda b: (b, 0, 0, 0))],
        out_shape=[jax.ShapeDtypeStruct((batch, B_WIDTH), F32),
                   jax.ShapeDtypeStruct((batch, B_HEADS, HEAD_DIM, HEAD_DIM), F32)],
        compiler_params=_params(("parallel",)),
        name="rwkv_step",
    )(r, k, v, lw, a, gate, state, *head_wts)


def _ab_out_kernel(x_ref, ao_ref, yb_ref, wa_ref, wb_ref, g_ref, o_ref):
    mix = _dot(ao_ref[...].astype(BF16), wa_ref[...])
    mix += _dot(yb_ref[...].astype(BF16), wb_ref[...])
    o_ref[...] = mix
    _add_normed_rows(o_ref, x_ref, g_ref[...])


def _ab_out(x, ao, yb, w_out, norm_g, g_row, li, tm):
    m, d = x.shape
    return pl.pallas_call(
        _ab_out_kernel,
        grid=(m // tm,),
        in_specs=[
            pl.BlockSpec((tm, d), lambda i: (i, 0)),
            pl.BlockSpec((tm, A_Q), lambda i: (i, 0)),
            pl.BlockSpec((tm, B_WIDTH), lambda i: (i, 0)),
            pl.BlockSpec((None, A_Q, d), lambda i: (li, 0, 0)),
            pl.BlockSpec((None, B_WIDTH, d), lambda i: (li, 1, 0)),
            _layer(norm_g, (g_row,)),
        ],
        out_specs=pl.BlockSpec((tm, d), lambda i: (i, 0)),
        out_shape=jax.ShapeDtypeStruct((m, d), F32),
        compiler_params=_params(("parallel",)),
        name="ab_out",
    )(x, ao, yb, w_out, w_out, norm_g)


def _rope_tables(pos):
    half = HEAD_DIM // 2
    freqs = ROPE_THETA ** (-jnp.arange(half, dtype=F32) / half)
    ang = pos.astype(F32)[:, None] * freqs[None, :]
    cos = jnp.cos(ang)
    sin = jnp.sin(ang)
    return (jnp.tile(jnp.concatenate([cos, cos], axis=-1), (1, A_HEADS)),
            jnp.tile(jnp.concatenate([-sin, sin], axis=-1), (1, A_HEADS)))


def _trunk(xp3, xs3, past, w):
    bp, seq, d = xp3.shape
    bs = xs3.shape[0]
    mp = bp * seq
    xp = xp3.reshape(mp, d)
    xs = xs3.reshape(bs, d)
    tp = _tiles(mp)
    ts = _tiles(bs)
    assert all(seq % n == 0 for n in (tp["mix_m"], RWKV_CHUNK * RWKV_STEP_CHUNKS, WINDOW * SWA_STEP_BLOCKS))
    assert mp % tp["ffn_m"] == 0
    cos_p, sin_p = _rope_tables(jnp.arange(seq, dtype=jnp.int32))
    cos_s, sin_s = _rope_tables(PAST_LEN + jnp.arange(1, dtype=jnp.int32))
    cache_k, cache_v, state, shift, conv_buf = past
    ng = w["norm_g"]
    outs = {key: [] for key in ("pk", "pv", "ps", "psh", "pc", "sk", "sv", "ss", "ssh", "sc")}
    kv_heads = lambda t, b: t.reshape(b, WINDOW, A_KV_HEADS, HEAD_DIM)
    for l in range(w["depth"]):
        i = l // 2
        grow = lambda j: l * 6 + j
        xp, xs = _ffn(xp, xs, ng, (grow(0), grow(1)), w["ffn_w_gu"], w["ffn_w_down"], (l, 0),
                      tp["ffn_m"], tp["ffn_f"])
        if l % 2 == 0:
            proj = _norm_mm(xp, ng, grow(2), w["ab_w_in"], i, tp["proj_m"], 1280)
            ao, nk, nv = _swa_seq(proj, w["attn_sinks"], i, cos_p, sin_p, bp, seq)
            yb, st = _rwkv_scan(proj, w["rwkv_in"], w["rwkv_rows"], i, bp, RWKV_CHUNK * RWKV_STEP_CHUNKS)
            xp = _ab_out(xp, ao, yb, w["ab_w_out"], ng, grow(3), i, tp["mix_m"])
            outs["pk"].append(kv_heads(nk, bp))
            outs["pv"].append(kv_heads(nv, bp))
            outs["ps"].append(st)
            outs["psh"].append(proj.reshape(bp, seq, AB_COLS_PAD)[:, -1, A_COLS:AB_COLS])

            proj = _norm_mm(xs, ng, grow(2), w["ab_w_in"], i, ts["proj_m"], 1280)
            ao, nk, nv = _swa_step(proj, w["attn_sinks"], i, cache_k, cache_v, cos_s, sin_s, min(8, bs))
            prev = jnp.pad(shift[i], ((0, 0), (0, B_COLS_PAD - B_COLS)))
            r, k, v, lw, a, gt = _rwkv_in_step(proj, prev, w["rwkv_in"], i)
            yb, st = _rwkv_step(r, k, v, lw, a, gt, state, w["rwkv_heads"], i, min(8, bs))
            xs = _ab_out(xs, ao, yb, w["ab_w_out"], ng, grow(3), i, ts["mix_m"])
            outs["sk"].append(kv_heads(nk, bs))
            outs["sv"].append(kv_heads(nv, bs))
            outs["ss"].append(st)
            outs["ssh"].append(proj[:, A_COLS:AB_COLS])
        else:
            bg, u = _conv_in(xp, ng, grow(2), w["conv_w_in"], i, tp["proj_m"], 512)
            xp = _conv_out_seq(xp, bg, u, w["conv_w"], w["conv_w_out"], ng, grow(3), i, tp["mix_m"], seq)
            outs["pc"].append(u.reshape(bp, seq, -1)[:, -(CONV_W - 1):])

            bg, u = _conv_in(xs, ng, grow(2), w["conv_w_in"], i, ts["proj_m"], 512)
            buf = conv_buf[i]
            xs = _conv_out_step(xs, bg, u, buf[:, 1], buf[:, 0], w["conv_w"], w["conv_w_out"], ng, grow(3), i)
            outs["sc"].append(jnp.stack([buf[:, 1], u], axis=1))
        xp, xs = _ffn(xp, xs, ng, (grow(4), grow(5)), w["ffn_w_gu"], w["ffn_w_down"], (l, 1),
                      tp["ffn_m"], tp["ffn_f"])
    stacked = {key: jnp.stack(val) for key, val in outs.items()}
    return xp.reshape(bp, seq, d), xs.reshape(bs, 1, d), stacked


def _prep_weights(norm_g, ffn_w_gu, ffn_w_down, ab_w_in, ab_w_out, attn_sinks, rwkv_mu, rwkv_w0, rwkv_w_decay,
                  rwkv_a0, rwkv_w_aaa, rwkv_w_gate, rwkv_k_k, rwkv_k_a, rwkv_r_k, rwkv_gn_w, rwkv_gn_b,
                  conv_w_in, conv_w, conv_w_out):
    depth, n_norm, d = norm_g.shape
    padc = lambda t, n: jnp.pad(t, [(0, 0)] * (t.ndim - 1) + [(0, n - t.shape[-1])])
    row = lambda t: t[:, None, :]
    wdec = jnp.pad(rwkv_w_decay, ((0, 0), (0, D_AAA), (0, 0))).astype(BF16)
    waaa = jnp.pad(rwkv_w_aaa, ((0, 0), (D_DECAY, 0), (0, 0))).astype(BF16)
    wgate = jnp.pad(rwkv_w_gate, ((0, 0), (0, GATE_PAD - D_GATE), (0, 0))).astype(BF16)
    return {
        "depth": depth,
        "norm_g": norm_g.reshape(depth * n_norm, 1, d),
        "ffn_w_gu": ffn_w_gu,
        "ffn_w_down": ffn_w_down,
        "ab_w_in": padc(ab_w_in, AB_COLS_PAD).astype(BF16),
        "ab_w_out": ab_w_out.astype(BF16),
        "attn_sinks": attn_sinks,
        "rwkv_in": (row(padc(rwkv_mu, B_COLS_PAD)), row(rwkv_w0), wdec, row(rwkv_a0), waaa, wgate),
        "rwkv_rows": tuple(row(t) for t in (rwkv_k_k, rwkv_k_a, rwkv_r_k, rwkv_gn_w, rwkv_gn_b)),
        "rwkv_heads": tuple(t.reshape(-1, B_HEADS, 1, HEAD_DIM)
                            for t in (rwkv_k_k, rwkv_k_a, rwkv_r_k, rwkv_gn_w, rwkv_gn_b)),
        "conv_w_in": conv_w_in,
        "conv_w": conv_w,
        "conv_w_out": conv_w_out.astype(BF16),
    }


def kernel(x_prompt, x_sample, cache_swa_k, cache_swa_v, state_rwkv, state_rwkv_shift, state_conv, norm_g, ffn_w_gu, ffn_w_down, ab_w_in, ab_w_out, attn_sinks, rwkv_mu, rwkv_w0, rwkv_w_decay, rwkv_a0, rwkv_w_aaa, rwkv_w_gate, rwkv_k_k, rwkv_k_a, rwkv_r_k, rwkv_gn_w, rwkv_gn_b, conv_w_in, conv_w, conv_w_out):
    w = _prep_weights(norm_g, ffn_w_gu, ffn_w_down, ab_w_in, ab_w_out, attn_sinks, rwkv_mu, rwkv_w0, rwkv_w_decay,
                      rwkv_a0, rwkv_w_aaa, rwkv_w_gate, rwkv_k_k, rwkv_k_a, rwkv_r_k, rwkv_gn_w, rwkv_gn_b,
                      conv_w_in, conv_w, conv_w_out)
    n_ab, dec_batch = cache_swa_k.shape[:2]
    past = (cache_swa_k.reshape(n_ab, dec_batch, WINDOW, A_KV), cache_swa_v.reshape(n_ab, dec_batch, WINDOW, A_KV),
            state_rwkv, state_rwkv_shift, state_conv)
    y_prompt, y_sample, o = _trunk(x_prompt, x_sample, past, w)
    return (y_prompt, y_sample, o["pk"], o["pv"], o["ps"], o["psh"], o["pc"],
            o["sk"], o["sv"], o["ss"], o["ssh"], o["sc"])
```
